```python
import math
import jax, jax.numpy as jnp
from jax import lax
import numpy as np

D_MODEL = 2048
BATCH = 4
SEQ = 2048
DEPTH = 4
DEC_BATCH = 128
DEC_SEQ = 1
PAST_LEN = 16384
PAGE_SIZE = 128

MIX_W = D_MODEL
N_MIXERS = 4
GROUP_W = MIX_W // N_MIXERS
HEAD_DIM = 128
POOL_WINDOWS = (2, 4, 8, 16)
POOL_GROUPS = len(POOL_WINDOWS)
POOL_GW = GROUP_W // POOL_GROUPS
POOL_STATE = max(POOL_WINDOWS) - 1
DN_HEADS = GROUP_W // HEAD_DIM
DN_CONV = 4
ML_HEADS = GROUP_W // HEAD_DIM
SC_WIDTH = 3
CHUNK = 64
MEM_LEN = 256
X_HEADS = 4
X_HEAD_DIM = 128
X_W = X_HEADS * X_HEAD_DIM
D_FF = ((8 * D_MODEL // 3 + 127) // 128) * 128
EPS = 1e-6
N_FFN1_PRE = 0
N_FFN1_POST = 1
N_MIX_PRE = 2
N_MIX_POST = 3
N_X_PRE = 4
N_X_POST = 5
N_FFN2_PRE = 6
N_FFN2_POST = 7
N_MEM = 8
N_NORMS = 9
SPLIT_SIZES = (GROUP_W,
               GROUP_W, GROUP_W, GROUP_W, GROUP_W, DN_HEADS, DN_HEADS,
               GROUP_W, GROUP_W, GROUP_W, GROUP_W, ML_HEADS, ML_HEADS,
               GROUP_W, GROUP_W, GROUP_W)
IN_W = sum(SPLIT_SIZES)

kernel_name = 'hybrid_pool_deltanet_mlstm_conv_decoder_step'


def rmsnorm(x, g):
    xf = x.astype(jnp.float32)
    y = xf * lax.rsqrt(jnp.mean(xf * xf, axis=-1, keepdims=True) + EPS)
    return (y * g.astype(jnp.float32)).astype(x.dtype)


def l2norm(x):
    return x * lax.rsqrt(jnp.sum(x * x, axis=-1, keepdims=True) + EPS)


def swiglu(x, w_gate, w_up, w_down):
    return (jax.nn.silu(x @ w_gate) * (x @ w_up)) @ w_down


def causal_dwconv(u, buf, w):
    W = w.shape[0]
    T = u.shape[1]
    ext = jnp.concatenate([buf.astype(u.dtype), u], axis=1)
    y = sum(ext[:, j:j + T] * w[j] for j in range(W))
    return y, ext[:, T:]


def _to_chunks(a, L):
    B_, T, H = a.shape[:3]
    a = a.reshape(B_, T // L, L, H, *a.shape[3:])
    return jnp.moveaxis(a, (1, 3), (0, 2))


def _from_chunks(a):
    N, B_, H, L = a.shape[:4]
    return jnp.moveaxis(a, (0, 2), (1, 3)).reshape(B_, N * L, H, *a.shape[4:])


def pool_mixer(u, prev, w_grp, scale, start_pos):
    B_, T, _ = u.shape
    ext = jnp.concatenate([prev.astype(u.dtype), u], axis=1).astype(jnp.float32)
    cs = jnp.concatenate([jnp.zeros_like(ext[:, :1]), jnp.cumsum(ext, axis=1)], axis=1)
    end = cs[:, POOL_STATE + 1:]
    pos = start_pos + jnp.arange(T)
    means = []
    for g, w in enumerate(POOL_WINDOWS):
        sl = slice(g * POOL_GW, (g + 1) * POOL_GW)
        begin = cs[:, POOL_STATE + 1 - w:POOL_STATE + 1 - w + T, sl]
        cnt = jnp.minimum(pos + 1, w).astype(jnp.float32)[None, :, None]
        means.append((end[..., sl] - begin) / cnt)
    d = jnp.concatenate(means, axis=-1) - ext[:, POOL_STATE:]
    d = d.reshape(B_, T, POOL_GROUPS, POOL_GW)
    y = jnp.einsum('btgc,gcd->btgd', d, w_grp.astype(jnp.float32)).reshape(B_, T, GROUP_W)
    y = y * scale.astype(jnp.float32)
    return y.astype(u.dtype), ext[:, T:]


def gated_delta_chunked(q, k, v, beta, g, S0):
    T = q.shape[1]
    L = math.gcd(T, CHUNK)
    qc, kc, vc = _to_chunks(q, L), _to_chunks(k, L), _to_chunks(v, L)
    bc, gc = _to_chunks(beta, L), _to_chunks(g, L)
    gcum = jnp.cumsum(gc, axis=-1)
    causal = jnp.tril(jnp.ones((L, L), bool))
    strict = jnp.tril(jnp.ones((L, L), bool), -1)
    diff = gcum[..., :, None] - gcum[..., None, :]
    decay = jnp.where(causal, jnp.exp(jnp.where(causal, diff, 0.0)), 0.0)
    kb = kc * bc[..., None]
    A = jnp.where(strict, jnp.einsum('nbhtk,nbhsk->nbhts', kb, kc) * decay, 0.0)
    eye = jnp.eye(L, dtype=A.dtype)
    Tinv = lax.linalg.triangular_solve(eye + A, jnp.broadcast_to(eye, A.shape),
                                       left_side=True, lower=True, unit_diagonal=True)
    u_c = Tinv @ (vc * bc[..., None])
    w_c = Tinv @ (kb * jnp.exp(gcum)[..., None])
    qk = jnp.where(causal, jnp.einsum('nbhtk,nbhsk->nbhts', qc, kc) * decay, 0.0)
    g_last = gcum[..., -1]
    k_tail = kc * jnp.exp(g_last[..., None] - gcum)[..., None]
    q_dec = qc * jnp.exp(gcum)[..., None]

    def step(S, xs):
        qd, ui, wi, qki, gl, kt = xs
        v_new = ui - jnp.einsum('bhlk,bhkv->bhlv', wi, S)
        o = jnp.einsum('bhlk,bhkv->bhlv', qd, S) + jnp.einsum('bhts,bhsv->bhtv', qki, v_new)
        S = S * jnp.exp(gl)[..., None, None] + jnp.einsum('bhlk,bhlv->bhkv', kt, v_new)
        return S, o

    S, o = lax.scan(step, S0, (q_dec, u_c, w_c, qk, g_last, k_tail))
    return _from_chunks(o), S


def deltanet_mixer(q, k, v, z, b, a, conv_buf, S0, conv_w, A_log, dt_bias, norm_g):
    B_, T, _ = q.shape
    f32 = jnp.float32
    qkv, new_buf = causal_dwconv(jnp.concatenate([q, k, v], axis=-1), conv_buf, conv_w)
    qkv = jax.nn.silu(qkv.astype(f32))
    qs, ks, vs = jnp.split(qkv, 3, axis=-1)
    heads = lambda t: t.astype(f32).reshape(B_, T, DN_HEADS, HEAD_DIM)
    qh = l2norm(heads(qs)) * HEAD_DIM ** -0.5
    kh = l2norm(heads(ks))
    beta = jax.nn.sigmoid(b.astype(f32))
    g = -jnp.exp(A_log.astype(f32)) * jax.nn.softplus(a.astype(f32) + dt_bias.astype(f32))
    o, S = gated_delta_chunked(qh, kh, heads(vs), beta, g, S0.astype(f32))
    o = rmsnorm(o, norm_g) * jax.nn.silu(heads(z))
    return o.reshape(B_, T, GROUP_W).astype(q.dtype), new_buf, S


def mlstm_chunked(q, k, v, log_i, log_f, C0, n0, m0):
    T = q.shape[1]
    L = math.gcd(T, CHUNK)
    qc, kc, vc = _to_chunks(q, L), _to_chunks(k, L), _to_chunks(v, L)
    lic, lfc = _to_chunks(log_i, L), _to_chunks(log_f, L)
    causal = jnp.tril(jnp.ones((L, L), bool))
    F = jnp.cumsum(lfc, axis=-1)
    Dm = jnp.where(causal, F[..., :, None] - F[..., None, :] + lic[..., None, :], -jnp.inf)
    qk = jnp.einsum('nbhtd,nbhsd->nbhts', qc, kc)
    F_last = F[..., -1]
    a_w = F_last[..., None] - F + lic
    a_max = jnp.max(a_w, axis=-1)

    def step(carry, xs):
        C, n, m = carry
        qi, ki, vi, Fi, Di, qki, ai, amax, FL = xs
        b = Fi + m[..., None]
        m_t = jnp.maximum(b, jnp.max(Di, axis=-1))
        w_intra = jnp.exp(Di - m_t[..., None]) * qki
        w_inter = jnp.exp(b - m_t)
        num = w_inter[..., None] * jnp.einsum('bhtk,bhkv->bhtv', qi, C) + jnp.einsum('bhts,bhsv->bhtv', w_intra, vi)
        den = w_inter * jnp.einsum('bhtk,bhk->bht', qi, n) + jnp.sum(w_intra, axis=-1)
        h = num / jnp.maximum(jnp.abs(den), jnp.exp(-m_t))[..., None]
        m_new = jnp.maximum(m + FL, amax)
        dec = jnp.exp(m + FL - m_new)
        wk = jnp.exp(ai - m_new[..., None])[..., None] * ki
        C = dec[..., None, None] * C + jnp.einsum('bhlk,bhlv->bhkv', wk, vi)
        n = dec[..., None] * n + jnp.sum(wk, axis=-2)
        return (C, n, m_new), h

    (C, n, m), h = lax.scan(step, (C0, n0, m0), (qc, kc, vc, F, Dm, qk, a_w, a_max, F_last))
    return _from_chunks(h), C, n, m


def mlstm_mixer(q, k, v, o, i_pre, f_pre, C0, n0, m0, b_i, b_f, norm_g):
    B_, T, _ = q.shape
    f32 = jnp.float32
    heads = lambda t: t.astype(f32).reshape(B_, T, ML_HEADS, HEAD_DIM)
    log_i = i_pre.astype(f32) + b_i.astype(f32)
    log_f = jax.nn.log_sigmoid(f_pre.astype(f32) + b_f.astype(f32))
    h, C, n, m = mlstm_chunked(heads(q), heads(k) * HEAD_DIM ** -0.5, heads(v), log_i, log_f,
                               C0.astype(f32), n0.astype(f32), m0.astype(f32))
    h = jax.nn.sigmoid(heads(o)) * h
    h = rmsnorm(h, norm_g)
    return h.reshape(B_, T, GROUP_W).astype(q.dtype), C, n, m


def shortconv_mixer(bg, cg, hin, buf, w):
    u = cg * hin
    y, new_buf = causal_dwconv(u, buf, w)
    return bg * y, new_buf


def memory_kv(mem, g, w_k, w_v):
    B_, M, _ = mem.shape
    m = rmsnorm(mem, g)
    return ((m @ w_k).reshape(B_, M, X_HEADS, X_HEAD_DIM),
            (m @ w_v).reshape(B_, M, X_HEADS, X_HEAD_DIM))


def cross_attention(h, mem_k, mem_v, w_q, w_o):
    B_, T, _ = h.shape
    q = (h @ w_q).reshape(B_, T, X_HEADS, X_HEAD_DIM).astype(jnp.float32)
    s = jnp.einsum('bthd,bmhd->bhtm', q, mem_k.astype(jnp.float32)) * X_HEAD_DIM ** -0.5
    p = jax.nn.softmax(s, axis=-1)
    o = jnp.einsum('bhtm,bmhd->bthd', p, mem_v.astype(jnp.float32)).reshape(B_, T, X_W)
    return o.astype(h.dtype) @ w_o


def decoder_layer(x, mem_k, mem_v, st, lw, start_pos):
    pool_buf, dn_buf, dn_S, ml_C, ml_n, ml_m, sc_buf = st
    g = lw['norm_g']
    h = rmsnorm(x, g[N_FFN1_PRE])
    x = x + 0.5 * rmsnorm(swiglu(h, lw['ffn1_wg'], lw['ffn1_wu'], lw['ffn1_wd']), g[N_FFN1_POST])
    h = rmsnorm(x, g[N_MIX_PRE])
    split_at = [int(s) for s in np.cumsum(SPLIT_SIZES)[:-1]]
    (uA, qB, kB, vB, zB, bB, aB, qC, kC, vC, oC, iC, fC, bD, cD, hD) = jnp.split(h @ lw['w_in'], split_at, axis=-1)
    yA, pool_new = pool_mixer(uA, pool_buf, lw['pool_w'], lw['pool_scale'], start_pos)
    yB, dn_buf_new, dn_S_new = deltanet_mixer(qB, kB, vB, zB, bB, aB, dn_buf, dn_S, lw['dn_conv_w'],
                                              lw['dn_A_log'], lw['dn_dt_bias'], lw['dn_norm_g'])
    yC, C_new, n_new, m_new = mlstm_mixer(qC, kC, vC, oC, iC, fC, ml_C, ml_n, ml_m,
                                          lw['ml_b_i'], lw['ml_b_f'], lw['ml_norm_g'])
    yD, sc_new = shortconv_mixer(bD, cD, hD, sc_buf, lw['sc_conv_w'])
    mix = jnp.concatenate([yA, yB, yC, yD], axis=-1) @ lw['w_out']
    x = x + rmsnorm(mix, g[N_MIX_POST])
    h = rmsnorm(x, g[N_X_PRE])
    x = x + rmsnorm(cross_attention(h, mem_k, mem_v, lw['x_wq'], lw['x_wo']), g[N_X_POST])
    h = rmsnorm(x, g[N_FFN2_PRE])
    x = x + 0.5 * rmsnorm(swiglu(h, lw['ffn2_wg'], lw['ffn2_wu'], lw['ffn2_wd']), g[N_FFN2_POST])
    new = tuple(s.astype(x.dtype) for s in (pool_new, dn_buf_new, dn_S_new, C_new, n_new, m_new, sc_new))
    return x, new


def setup_inputs(seed: int = 0) -> dict:
    key = jax.random.key(seed)
    ks = iter(jax.random.split(key, 48))
    f32 = jnp.float32
    nrm = lambda shape, s=1.0: jax.random.normal(next(ks), shape, f32) * s
    inp = {}
    inp['x_prompt'] = nrm((BATCH, SEQ, D_MODEL))
    inp['x_sample'] = nrm((DEC_BATCH, DEC_SEQ, D_MODEL))
    inp['mem_prompt'] = nrm((BATCH, MEM_LEN, D_MODEL))
    inp['state_pool'] = nrm((DEPTH, DEC_BATCH, POOL_STATE, GROUP_W))
    inp['state_dn_conv'] = nrm((DEPTH, DEC_BATCH, DN_CONV - 1, 3 * GROUP_W))
    inp['state_dn_S'] = nrm((DEPTH, DEC_BATCH, DN_HEADS, HEAD_DIM, HEAD_DIM), 0.1)
    inp['state_ml_C'] = nrm((DEPTH, DEC_BATCH, ML_HEADS, HEAD_DIM, HEAD_DIM), 0.1)
    inp['state_ml_n'] = nrm((DEPTH, DEC_BATCH, ML_HEADS, HEAD_DIM), 0.1)
    inp['state_ml_m'] = nrm((DEPTH, DEC_BATCH, ML_HEADS))
    inp['state_sc_conv'] = nrm((DEPTH, DEC_BATCH, SC_WIDTH - 1, GROUP_W))
    inp['cache_mem_k'] = nrm((DEPTH, DEC_BATCH, MEM_LEN, X_HEADS, X_HEAD_DIM))
    inp['cache_mem_v'] = nrm((DEPTH, DEC_BATCH, MEM_LEN, X_HEADS, X_HEAD_DIM))
    inp['norm_g'] = 1.0 + nrm((DEPTH, N_NORMS, D_MODEL), 0.05)
    inp['w_in'] = nrm((DEPTH, D_MODEL, IN_W), D_MODEL ** -0.5)
    inp['w_out'] = nrm((DEPTH, MIX_W, D_MODEL), MIX_W ** -0.5)
    inp['pool_w'] = nrm((DEPTH, POOL_GROUPS, POOL_GW, POOL_GW), POOL_GW ** -0.5)
    inp['pool_scale'] = 1.0 + nrm((DEPTH, GROUP_W), 0.05)
    inp['dn_conv_w'] = nrm((DEPTH, DN_CONV, 3 * GROUP_W), DN_CONV ** -0.5)
    inp['dn_A_log'] = jnp.log(jax.random.uniform(next(ks), (DEPTH, DN_HEADS), f32, 1.0, 16.0))
    dt = jnp.exp(jax.random.uniform(next(ks), (DEPTH, DN_HEADS), f32, math.log(1e-3), math.log(1e-1)))
    inp['dn_dt_bias'] = dt + jnp.log(-jnp.expm1(-dt))
    inp['dn_norm_g'] = 1.0 + nrm((DEPTH, HEAD_DIM), 0.05)
    inp['ml_b_i'] = nrm((DEPTH, ML_HEADS), 0.1)
    inp['ml_b_f'] = jnp.linspace(3.0, 6.0, ML_HEADS, dtype=f32)[None, :] + nrm((DEPTH, ML_HEADS), 0.1)
    inp['ml_norm_g'] = 1.0 + nrm((DEPTH, ML_HEADS, HEAD_DIM), 0.05)
    inp['sc_conv_w'] = nrm((DEPTH, SC_WIDTH, GROUP_W), SC_WIDTH ** -0.5)
    inp['x_wq'] = nrm((DEPTH, D_MODEL, X_W), D_MODEL ** -0.5)
    inp['x_wk'] = nrm((DEPTH, D_MODEL, X_W), D_MODEL ** -0.5)
    inp['x_wv'] = nrm((DEPTH, D_MODEL, X_W), D_MODEL ** -0.5)
    inp['x_wo'] = nrm((DEPTH, X_W, D_MODEL), X_W ** -0.5)
    inp['ffn1_wg'] = nrm((DEPTH, D_MODEL, D_FF), D_MODEL ** -0.5)
    inp['ffn1_wu'] = nrm((DEPTH, D_MODEL, D_FF), D_MODEL ** -0.5)
    inp['ffn1_wd'] = nrm((DEPTH, D_FF, D_MODEL), D_FF ** -0.5)
    inp['ffn2_wg'] = nrm((DEPTH, D_MODEL, D_FF), D_MODEL ** -0.5)
    inp['ffn2_wu'] = nrm((DEPTH, D_MODEL, D_FF), D_MODEL ** -0.5)
    inp['ffn2_wd'] = nrm((DEPTH, D_FF, D_MODEL), D_FF ** -0.5)
    return inp


def reference(x_prompt, x_sample, mem_prompt, state_pool, state_dn_conv, state_dn_S, state_ml_C, state_ml_n,
              state_ml_m, state_sc_conv, cache_mem_k, cache_mem_v, norm_g, w_in, w_out, pool_w, pool_scale,
              dn_conv_w, dn_A_log, dn_dt_bias, dn_norm_g, ml_b_i, ml_b_f, ml_norm_g, sc_conv_w,
              x_wq, x_wk, x_wv, x_wo, ffn1_wg, ffn1_wu, ffn1_wd, ffn2_wg, ffn2_wu, ffn2_wd):
    def layer_weights(l):
        return dict(norm_g=norm_g[l], w_in=w_in[l], w_out=w_out[l], pool_w=pool_w[l], pool_scale=pool_scale[l],
                    dn_conv_w=dn_conv_w[l], dn_A_log=dn_A_log[l], dn_dt_bias=dn_dt_bias[l],
                    dn_norm_g=dn_norm_g[l], ml_b_i=ml_b_i[l], ml_b_f=ml_b_f[l], ml_norm_g=ml_norm_g[l],
                    sc_conv_w=sc_conv_w[l], x_wq=x_wq[l], x_wo=x_wo[l],
                    ffn1_wg=ffn1_wg[l], ffn1_wu=ffn1_wu[l], ffn1_wd=ffn1_wd[l],
                    ffn2_wg=ffn2_wg[l], ffn2_wu=ffn2_wu[l], ffn2_wd=ffn2_wd[l])

    bp, dt = x_prompt.shape[0], x_prompt.dtype
    zero_state = (jnp.zeros((bp, POOL_STATE, GROUP_W), dt),
                  jnp.zeros((bp, DN_CONV - 1, 3 * GROUP_W), dt),
                  jnp.zeros((bp, DN_HEADS, HEAD_DIM, HEAD_DIM), dt),
                  jnp.zeros((bp, ML_HEADS, HEAD_DIM, HEAD_DIM), dt),
                  jnp.zeros((bp, ML_HEADS, HEAD_DIM), dt),
                  jnp.zeros((bp, ML_HEADS), dt),
                  jnp.zeros((bp, SC_WIDTH - 1, GROUP_W), dt))
    h = x_prompt
    p_states, mem_k_list, mem_v_list = [], [], []
    for l in range(DEPTH):
        mk, mv = memory_kv(mem_prompt, norm_g[l, N_MEM], x_wk[l], x_wv[l])
        h, ns = decoder_layer(h, mk, mv, zero_state, layer_weights(l), 0)
        p_states.append(ns)
        mem_k_list.append(mk)
        mem_v_list.append(mv)
    y_prompt = h

    s_inputs = (state_pool, state_dn_conv, state_dn_S, state_ml_C, state_ml_n, state_ml_m, state_sc_conv)
    h = x_sample
    s_states = []
    for l in range(DEPTH):
        st = tuple(s[l] for s in s_inputs)
        h, ns = decoder_layer(h, cache_mem_k[l], cache_mem_v[l], st, layer_weights(l), PAST_LEN)
        s_states.append(ns)
    y_sample = h

    pool_p, dn_conv_p, dn_S_p, ml_C_p, ml_n_p, ml_m_p, sc_conv_p = [jnp.stack(z) for z in zip(*p_states)]
    pool_s, dn_conv_s, dn_S_s, ml_C_s, ml_n_s, ml_m_s, sc_conv_s = [jnp.stack(z) for z in zip(*s_states)]
    mem_k_p = jnp.stack(mem_k_list)
    mem_v_p = jnp.stack(mem_v_list)
    return (y_prompt, y_sample, pool_p, pool_s, dn_conv_p, dn_conv_s, dn_S_p, dn_S_s, ml_C_p, ml_C_s,
            ml_n_p, ml_n_s, ml_m_p, ml_m_s, sc_conv_p, sc_conv_s, mem_k_p, mem_v_p)
```

```python
import functools

import numpy as np
import jax
import jax.numpy as jnp
from jax import lax
from jax.experimental import pallas as pl
from jax.experimental.pallas import tpu as pltpu

F32 = jnp.float32
BF16 = jnp.bfloat16

EPS = 1e-6
D_MODEL = 2048
GROUP_W = 512
HEAD_DIM = 128
N_HEADS = GROUP_W // HEAD_DIM
CHUNK = 64
POOL_WINDOWS = (2, 4, 8, 16)
POOL_STATE = 15
DN_CONV = 4
SC_WIDTH = 3
MEM_LEN = 256
D_FF = 5504
D_FF_PAD = 5632
FFN_TN = 512
COL_A, COL_BQ, COL_BK, COL_BV, COL_BZ = 0, 1, 2, 3, 4
COL_CQ, COL_CK, COL_CV, COL_CO = 5, 6, 7, 8
COL_DB, COL_DC, COL_DH = 9, 10, 11
GATE_COL0 = 12 * GROUP_W
IN_W_PAD = GATE_COL0 + 128
LANE_BETA, LANE_A, LANE_I, LANE_F = 0, 4, 8, 12
TB = 256
VMEM_LIMIT = 56 * 2**20


def _cparams(sem, vmem=None):
    return pltpu.CompilerParams(dimension_semantics=sem, vmem_limit_bytes=vmem)


def _bdot(a, b):
    return jnp.dot(a.astype(BF16), b.astype(BF16), preferred_element_type=F32)


def _bdot_nt(a, b):
    return lax.dot_general(a.astype(BF16), b.astype(BF16), (((1,), (1,)), ((), ())),
                           preferred_element_type=F32)


def _sigmoid(x):
    return 1.0 / (1.0 + jnp.exp(-x))


def _silu(x):
    return x * _sigmoid(x)


def _softplus(x):
    return jnp.maximum(x, 0.0) + jnp.log(1.0 + jnp.exp(-jnp.abs(x)))


def _col(x, idx):
    return x[:, idx:idx + 1]


def _resident(shape):
    nd = len(shape)
    return pl.BlockSpec(shape, lambda *_: (0,) * nd, pipeline_mode=pl.Buffered(1))


def _norm_into(x_ref, g_ref, h_ref):
    tm = x_ref.shape[0]
    rs = min(tm, 128)

    def body(i, carry):
        r = pl.multiple_of(i * rs, rs)
        x = x_ref[pl.ds(r, rs), :]
        ms = jnp.mean(x * x, axis=-1, keepdims=True)
        h_ref[pl.ds(r, rs), :] = (x * lax.rsqrt(ms + EPS) * g_ref[...]).astype(BF16)
        return carry

    lax.fori_loop(0, tm // rs, body, 0)


def _norm_mm_kernel(x_ref, g_ref, w_ref, o_ref, h_ref):
    @pl.when(pl.program_id(1) == 0)
    def _():
        _norm_into(x_ref, g_ref, h_ref)

    o_ref[...] = jnp.dot(h_ref[...], w_ref[...], preferred_element_type=F32).astype(o_ref.dtype)


def _norm_swiglu_kernel(x_ref, g_ref, wg_ref, wu_ref, o_ref, h_ref):
    @pl.when(pl.program_id(1) == 0)
    def _():
        _norm_into(x_ref, g_ref, h_ref)

    h = h_ref[...]
    a = jnp.dot(h, wg_ref[...], preferred_element_type=F32)
    b = jnp.dot(h, wu_ref[...], preferred_element_type=F32)
    o_ref[...] = (_silu(a) * b).astype(o_ref.dtype)


def norm_matmul(x, g, ws, *, tm, tn, out_dtype):
    M, K = x.shape
    N = ws[0].shape[1]
    tm = min(tm, M)
    kern = _norm_swiglu_kernel if len(ws) == 2 else _norm_mm_kernel
    return pl.pallas_call(
        kern,
        grid=(M // tm, N // tn),
        in_specs=[pl.BlockSpec((tm, K), lambda i, j: (i, 0)),
                  pl.BlockSpec((1, K), lambda i, j: (0, 0))]
        + [pl.BlockSpec((K, tn), lambda i, j: (0, j)) for _ in ws],
        out_specs=pl.BlockSpec((tm, tn), lambda i, j: (i, j)),
        out_shape=jax.ShapeDtypeStruct((M, N), out_dtype),
        scratch_shapes=[pltpu.VMEM((tm, K), BF16)],
        compiler_params=_cparams(("parallel", "arbitrary"), VMEM_LIMIT),
    )(x, g.reshape(1, K), *ws)


def _mm_resnorm_kernel(*refs, n_a, scale):
    a_refs = refs[:n_a]
    w_ref, g_ref, res_ref, o_ref = refs[n_a:]
    parts = [r[...].astype(BF16) for r in a_refs]
    a = parts[0] if n_a == 1 else jnp.concatenate(parts, axis=1)
    y = jnp.dot(a, w_ref[...], preferred_element_type=F32)
    ms = jnp.mean(y * y, axis=-1, keepdims=True)
    o_ref[...] = res_ref[...] + scale * (y * lax.rsqrt(ms + EPS) * g_ref[...])


def matmul_resnorm(a_list, w, g, res, *, scale, tm):
    M, N = res.shape
    tm = min(tm, M)
    K = w.shape[0]
    return pl.pallas_call(
        functools.partial(_mm_resnorm_kernel, n_a=len(a_list), scale=scale),
        grid=(M // tm,),
        in_specs=[pl.BlockSpec((tm, a.shape[1]), lambda i: (i, 0)) for a in a_list]
        + [_resident((K, N)), _resident((1, N)), pl.BlockSpec((tm, N), lambda i: (i, 0))],
        out_specs=pl.BlockSpec((tm, N), lambda i: (i, 0)),
        out_shape=jax.ShapeDtypeStruct((M, N), F32),
        compiler_params=_cparams(("parallel",), VMEM_LIMIT),
    )(*a_list, w, g.reshape(1, N), res)


def _chunk_cumsum(x):
    n = x.shape[0]
    row = lax.broadcasted_iota(jnp.int32, (n, 1), 0) % CHUNK
    s = 1
    while s < CHUNK:
        x = x + jnp.where(row >= s, pltpu.roll(x, s, 0), 0.0)
        s *= 2
    return x


def _chunk_last(x):
    n = x.shape[0]
    parts = [jnp.broadcast_to(x[c * CHUNK + CHUNK - 1:c * CHUNK + CHUNK, :], (CHUNK, x.shape[1]))
             for c in range(n // CHUNK)]
    return jnp.concatenate(parts, axis=0)


def _pad_rows(x, c, n_chunks):
    z = jnp.zeros_like(x)
    return jnp.concatenate([x if i == c else z for i in range(n_chunks)], axis=0)


def _blk_masks(n):
    r = lax.broadcasted_iota(jnp.int32, (n, n), 0)
    c = lax.broadcasted_iota(jnp.int32, (n, n), 1)
    same = (r // CHUNK) == (c // CHUNK)
    return same & (c <= r), same & (c < r), r == c


def _split_dot(a, b):
    ah = a.astype(BF16)
    al = (a - ah.astype(F32)).astype(BF16)
    bh = b.astype(BF16)
    bl = (b - bh.astype(F32)).astype(BF16)
    lhs = jnp.concatenate([ah, ah, al], axis=1)
    rhs = jnp.concatenate([bh, bl, bh], axis=0)
    return jnp.dot(lhs, rhs, preferred_element_type=F32)


def _unit_lower_inverse(a, eye):
    n = -a
    p = eye + n
    s = 2
    while s < CHUNK:
        n = _split_dot(n, n)
        p = p + _split_dot(p, n)
        s *= 2
    return p


def _pool_sconv_kernel(ua_ref, db_ref, dc_ref, dh_ref, pw_ref, ps_ref, sw_ref,
                       ya_ref, yd_ref, ptail_ref, stail_ref, hist_a, hist_d, *, tb):
    i = pl.program_id(1)

    @pl.when(i == 0)
    def _():
        hist_a[...] = jnp.zeros_like(hist_a)
        hist_d[...] = jnp.zeros_like(hist_d)

    u = ua_ref[...]
    ext = jnp.concatenate([hist_a[...], u], axis=0)
    a2 = ext + pltpu.roll(ext, 1, 0)
    a4 = a2[:, 128:] + pltpu.roll(a2[:, 128:], 2, 0)
    a8 = a4[:, 128:] + pltpu.roll(a4[:, 128:], 4, 0)
    a16 = a8[:, 128:] + pltpu.roll(a8[:, 128:], 8, 0)
    sums = (a2[16:, :128], a4[16:, :128], a8[16:, :128], a16[16:, :])
    pos = i * tb + lax.broadcasted_iota(jnp.int32, (tb, 1), 0)
    ys = []
    for g, w in enumerate(POOL_WINDOWS):
        cnt = jnp.minimum(pos + 1, w).astype(F32)
        d = sums[g] / cnt - u[:, g * 128:(g + 1) * 128]
        ys.append(_bdot(d, pw_ref[g]))
    ya_ref[...] = (jnp.concatenate(ys, axis=1) * ps_ref[...]).astype(ya_ref.dtype)
    hist_a[...] = u[tb - 16:, :]

    us = dc_ref[...] * dh_ref[...]
    ext = jnp.concatenate([hist_d[...], us], axis=0)
    sw = sw_ref[...]
    y = sw[0:1] * pltpu.roll(ext, 2, 0) + sw[1:2] * pltpu.roll(ext, 1, 0) + sw[2:3] * ext
    yd_ref[...] = (db_ref[...] * y[8:]).astype(yd_ref.dtype)
    hist_d[...] = us[tb - 8:, :]

    @pl.when(i == pl.num_programs(1) - 1)
    def _():
        ptail_ref[0] = u[tb - 16:, :]
        stail_ref[0] = us[tb - 8:, :]


def pool_sconv_prompt(P, B, T, pool_w, pool_scale, sc_w):
    nT = T // TB
    M = B * T
    col = lambda c: pl.BlockSpec((TB, GROUP_W), lambda b, i: (b * nT + i, c))
    return pl.pallas_call(
        functools.partial(_pool_sconv_kernel, tb=TB),
        grid=(B, nT),
        in_specs=[col(COL_A), col(COL_DB), col(COL_DC), col(COL_DH),
                  pl.BlockSpec((4, 128, 128), lambda b, i: (0, 0, 0)),
                  pl.BlockSpec((1, GROUP_W), lambda b, i: (0, 0)),
                  pl.BlockSpec((SC_WIDTH, GROUP_W), lambda b, i: (0, 0))],
        out_specs=[pl.BlockSpec((TB, GROUP_W), lambda b, i: (b * nT + i, 0)),
                   pl.BlockSpec((TB, GROUP_W), lambda b, i: (b * nT + i, 0)),
                   pl.BlockSpec((1, 16, GROUP_W), lambda b, i: (b, 0, 0)),
                   pl.BlockSpec((1, 8, GROUP_W), lambda b, i: (b, 0, 0))],
        out_shape=[jax.ShapeDtypeStruct((M, GROUP_W), BF16),
                   jax.ShapeDtypeStruct((M, GROUP_W), BF16),
                   jax.ShapeDtypeStruct((B, 16, GROUP_W), F32),
                   jax.ShapeDtypeStruct((B, 8, GROUP_W), F32)],
        scratch_shapes=[pltpu.VMEM((16, GROUP_W), F32), pltpu.VMEM((8, GROUP_W), F32)],
        compiler_params=_cparams(("parallel", "arbitrary"), VMEM_LIMIT),
    )(P, P, P, P, pool_w, pool_scale.reshape(1, GROUP_W), sc_w)


def _deltanet_kernel(q_ref, k_ref, v_ref, z_ref, gt_ref, cw_ref, alog_ref, dtb_ref, ng_ref,
                     y_ref, s_out_ref, ctail_ref, s_ref, hist_ref, *, tb):
    i = pl.program_id(1)
    nc = tb // CHUNK

    @pl.when(i == 0)
    def _():
        s_ref[...] = jnp.zeros_like(s_ref)
        hist_ref[...] = jnp.zeros_like(hist_ref)

    qkv = jnp.concatenate([q_ref[...], k_ref[...], v_ref[...]], axis=1)
    ext = jnp.concatenate([hist_ref[...], qkv], axis=0)
    cw = cw_ref[...]
    conv = (cw[0:1] * pltpu.roll(ext, 3, 0) + cw[1:2] * pltpu.roll(ext, 2, 0)
            + cw[2:3] * pltpu.roll(ext, 1, 0) + cw[3:4] * ext)[8:]
    act = _silu(conv)
    hist_ref[...] = qkv[tb - 8:, :]

    gates = gt_ref[...]
    beta_all = _sigmoid(gates)
    g_all = -jnp.exp(alog_ref[...]) * _softplus(gates + dtb_ref[...])
    gcum_all = _chunk_cumsum(g_all)
    glast_all = _chunk_last(gcum_all)
    gcum_t = gcum_all.T
    causal, strict, diag = _blk_masks(tb)
    eye = diag.astype(F32)

    ys = []
    for h in range(N_HEADS):
        sl = slice(h * HEAD_DIM, (h + 1) * HEAD_DIM)
        qh = act[:, sl]
        kh = act[:, GROUP_W + h * HEAD_DIM:GROUP_W + (h + 1) * HEAD_DIM]
        vh = act[:, 2 * GROUP_W + h * HEAD_DIM:2 * GROUP_W + (h + 1) * HEAD_DIM]
        qh = qh * lax.rsqrt(jnp.sum(qh * qh, axis=-1, keepdims=True) + EPS) * HEAD_DIM ** -0.5
        kh = kh * lax.rsqrt(jnp.sum(kh * kh, axis=-1, keepdims=True) + EPS)
        beta = _col(beta_all, LANE_BETA + h)
        gc = _col(gcum_all, LANE_A + h)
        gl = _col(glast_all, LANE_A + h)
        gr = gcum_t[LANE_A + h:LANE_A + h + 1, :]
        decay = jnp.where(causal, jnp.exp(jnp.where(causal, gc - gr, 0.0)), 0.0)
        kb = kh * beta
        k_t = kh.T
        a = jnp.where(strict, _bdot(kb, k_t) * decay, 0.0)
        tinv = _unit_lower_inverse(a, eye)
        egc = jnp.exp(gc)
        uw = _bdot(tinv, jnp.concatenate([vh * beta, kb * egc], axis=1))
        u = uw[:, :HEAD_DIM]
        w = uw[:, HEAD_DIM:]
        qk = jnp.where(causal, _bdot(qh, k_t) * decay, 0.0)
        q_dec = qh * egc
        tail = jnp.exp(gl - gc)
        s_h = s_ref[h]
        outs = []
        for c in range(nc):
            r = slice(c * CHUNK, (c + 1) * CHUNK)
            v_new = u[r] - _bdot(w[r], s_h)
            o_c = _bdot(q_dec[r], s_h) + _bdot(qk[r], _pad_rows(v_new, c, nc))
            outs.append(o_c)
            s_dec = jnp.exp(jnp.broadcast_to(gl[c * CHUNK:c * CHUNK + 1], (HEAD_DIM, 1)))
            s_h = s_h * s_dec + _bdot(k_t, _pad_rows(v_new * tail[r], c, nc))
        s_ref[h] = s_h
        o = jnp.concatenate(outs, axis=0)
        o = o * lax.rsqrt(jnp.mean(o * o, axis=-1, keepdims=True) + EPS) * ng_ref[...]
        ys.append(o * _silu(z_ref[:, sl]))
    y_ref[...] = jnp.concatenate(ys, axis=1).astype(y_ref.dtype)

    @pl.when(i == pl.num_programs(1) - 1)
    def _():
        s_out_ref[0] = s_ref[...]
        ctail_ref[0] = qkv[tb - 8:, :]


def deltanet_prompt(P, B, T, conv_w, a_log, dt_bias, norm_g):
    nT = T // TB
    M = B * T
    col = lambda c: pl.BlockSpec((TB, GROUP_W), lambda b, i: (b * nT + i, c))
    lane_row = lambda v, lane: jnp.zeros((1, 128), F32).at[0, lane:lane + N_HEADS].set(v)
    const = lambda shape: pl.BlockSpec(shape, lambda b, i: (0,) * len(shape))
    return pl.pallas_call(
        functools.partial(_deltanet_kernel, tb=TB),
        grid=(B, nT),
        in_specs=[col(COL_BQ), col(COL_BK), col(COL_BV), col(COL_BZ),
                  pl.BlockSpec((TB, 128), lambda b, i: (b * nT + i, GATE_COL0 // 128)),
                  const((DN_CONV, 3 * GROUP_W)), const((1, 128)), const((1, 128)), const((1, HEAD_DIM))],
        out_specs=[pl.BlockSpec((TB, GROUP_W), lambda b, i: (b * nT + i, 0)),
                   pl.BlockSpec((1, N_HEADS, HEAD_DIM, HEAD_DIM), lambda b, i: (b, 0, 0, 0)),
                   pl.BlockSpec((1, 8, 3 * GROUP_W), lambda b, i: (b, 0, 0))],
        out_shape=[jax.ShapeDtypeStruct((M, GROUP_W), BF16),
                   jax.ShapeDtypeStruct((B, N_HEADS, HEAD_DIM, HEAD_DIM), F32),
                   jax.ShapeDtypeStruct((B, 8, 3 * GROUP_W), F32)],
        scratch_shapes=[pltpu.VMEM((N_HEADS, HEAD_DIM, HEAD_DIM), F32),
                        pltpu.VMEM((8, 3 * GROUP_W), F32)],
        compiler_params=_cparams(("parallel", "arbitrary"), VMEM_LIMIT),
    )(P, P, P, P, P, conv_w, lane_row(a_log, LANE_A), lane_row(dt_bias, LANE_A),
      norm_g.reshape(1, HEAD_DIM))


def _mlstm_kernel(q_ref, k_ref, v_ref, og_ref, gt_ref, bi_ref, bf_ref, ng_ref,
                  y_ref, c_out_ref, nm_out_ref, c_ref, nm_ref, *, tb):
    i = pl.program_id(1)
    nc = tb // CHUNK

    @pl.when(i == 0)
    def _():
        c_ref[...] = jnp.zeros_like(c_ref)
        nm_ref[...] = jnp.zeros_like(nm_ref)

    gates = gt_ref[...]
    li_all = gates + bi_ref[...]
    lf_all = -_softplus(-(gates + bf_ref[...]))
    f_all = _chunk_cumsum(lf_all)
    flast_all = _chunk_last(f_all)
    f_t = f_all.T
    li_t = li_all.T
    causal, _, _ = _blk_masks(tb)

    ys = []
    for h in range(N_HEADS):
        sl = slice(h * HEAD_DIM, (h + 1) * HEAD_DIM)
        qh = q_ref[:, sl]
        kh = k_ref[:, sl] * HEAD_DIM ** -0.5
        vh = v_ref[:, sl]
        fc = _col(f_all, LANE_F + h)
        fl = _col(flast_all, LANE_F + h)
        lic = _col(li_all, LANE_I + h)
        fr = f_t[LANE_F + h:LANE_F + h + 1, :]
        lir = li_t[LANE_I + h:LANE_I + h + 1, :]
        dm = jnp.where(causal, fc - fr + lir, -jnp.inf)
        mx = jnp.max(dm, axis=1, keepdims=True)
        k_t = kh.T
        qk = _bdot(qh, k_t)
        a_w = fl - fc + lic
        c_h = c_ref[h]
        n_h = nm_ref[h:h + 1, :]
        m_h = nm_ref[N_HEADS + h:N_HEADS + h + 1, 0:1]
        outs = []
        for c in range(nc):
            r = slice(c * CHUNK, (c + 1) * CHUNK)
            fl_c = fl[c * CHUNK:c * CHUNK + 1]
            b = fc[r] + m_h
            m_t = jnp.maximum(b, mx[r])
            w_intra = jnp.exp(dm[r] - m_t) * qk[r]
            w_inter = jnp.exp(b - m_t)
            num = w_inter * _bdot(qh[r], c_h) + _bdot(w_intra, vh)
            den = (w_inter * jnp.sum(qh[r] * n_h, axis=-1, keepdims=True)
                   + jnp.sum(w_intra, axis=-1, keepdims=True))
            outs.append(num / jnp.maximum(jnp.abs(den), jnp.exp(-m_t)))
            m_new = jnp.maximum(m_h + fl_c, jnp.max(a_w[r], axis=0, keepdims=True))
            dec = jnp.exp(m_h + fl_c - m_new)
            e = jnp.exp(a_w[r] - m_new)
            c_h = c_h * dec + _bdot(k_t, _pad_rows(e * vh[r], c, nc))
            n_h = n_h * dec + jnp.sum(e * kh[r], axis=0, keepdims=True)
            m_h = m_new
        c_ref[h] = c_h
        nm_ref[h:h + 1, :] = n_h
        nm_ref[N_HEADS + h:N_HEADS + h + 1, :] = jnp.broadcast_to(m_h, (1, HEAD_DIM))
        hh = jnp.concatenate(outs, axis=0)
        hh = _sigmoid(og_ref[:, sl]) * hh
        hh = hh * lax.rsqrt(jnp.mean(hh * hh, axis=-1, keepdims=True) + EPS) * ng_ref[h:h + 1, :]
        ys.append(hh)
    y_ref[...] = jnp.concatenate(ys, axis=1).astype(y_ref.dtype)

    @pl.when(i == pl.num_programs(1) - 1)
    def _():
        c_out_ref[0] = c_ref[...]
        nm_out_ref[0] = nm_ref[...]


def mlstm_prompt(P, B, T, b_i, b_f, norm_g):
    nT = T // TB
    M = B * T
    col = lambda c: pl.BlockSpec((TB, GROUP_W), lambda b, i: (b * nT + i, c))
    lane_row = lambda v, lane: jnp.zeros((1, 128), F32).at[0, lane:lane + N_HEADS].set(v)
    const = lambda shape: pl.BlockSpec(shape, lambda b, i: (0,) * len(shape))
    return pl.pallas_call(
        functools.partial(_mlstm_kernel, tb=TB),
        grid=(B, nT),
        in_specs=[col(COL_CQ), col(COL_CK), col(COL_CV), col(COL_CO),
                  pl.BlockSpec((TB, 128), lambda b, i: (b * nT + i, GATE_COL0 // 128)),
                  const((1, 128)), const((1, 128)), const((N_HEADS, HEAD_DIM))],
        out_specs=[pl.BlockSpec((TB, GROUP_W), lambda b, i: (b * nT + i, 0)),
                   pl.BlockSpec((1, N_HEADS, HEAD_DIM, HEAD_DIM), lambda b, i: (b, 0, 0, 0)),
                   pl.BlockSpec((1, 2 * N_HEADS, HEAD_DIM), lambda b, i: (b, 0, 0))],
        out_shape=[jax.ShapeDtypeStruct((M, GROUP_W), BF16),
                   jax.ShapeDtypeStruct((B, N_HEADS, HEAD_DIM, HEAD_DIM), F32),
                   jax.ShapeDtypeStruct((B, 2 * N_HEADS, HEAD_DIM), F32)],
        scratch_shapes=[pltpu.VMEM((N_HEADS, HEAD_DIM, HEAD_DIM), F32),
                        pltpu.VMEM((2 * N_HEADS, HEAD_DIM), F32)],
        compiler_params=_cparams(("parallel", "arbitrary"), VMEM_LIMIT),
    )(P, P, P, P, P, lane_row(b_i, LANE_I), lane_row(b_f, LANE_F), norm_g)


def _xattn_prompt_kernel(x_ref, gpre_ref, wq_ref, kv_ref, wo_ref, gpost_ref, o_ref, h_ref):
    _norm_into(x_ref, gpre_ref, h_ref)
    q = jnp.dot(h_ref[...], wq_ref[...], preferred_element_type=F32)
    outs = []
    for h in range(N_HEADS):
        sl = slice(h * HEAD_DIM, (h + 1) * HEAD_DIM)
        k_h = kv_ref[:, sl]
        v_h = kv_ref[:, GROUP_W + h * HEAD_DIM:GROUP_W + (h + 1) * HEAD_DIM]
        s = _bdot_nt(q[:, sl], k_h) * HEAD_DIM ** -0.5
        p = jnp.exp(s - jnp.max(s, axis=-1, keepdims=True))
        p = p / jnp.sum(p, axis=-1, keepdims=True)
        outs.append(_bdot(p, v_h))
    o = jnp.concatenate(outs, axis=1).astype(BF16)
    y = jnp.dot(o, wo_ref[...], preferred_element_type=F32)
    ms = jnp.mean(y * y, axis=-1, keepdims=True)
    o_ref[...] = x_ref[...] + y * lax.rsqrt(ms + EPS) * gpost_ref[...]


def xattn_prompt(x, B, T, g_pre, wq, kv, wo, g_post, *, tq=512):
    nq = T // tq
    M = B * T
    return pl.pallas_call(
        _xattn_prompt_kernel,
        grid=(B, nq),
        in_specs=[pl.BlockSpec((tq, D_MODEL), lambda b, i: (b * nq + i, 0)),
                  pl.BlockSpec((1, D_MODEL), lambda b, i: (0, 0)),
                  pl.BlockSpec((D_MODEL, GROUP_W), lambda b, i: (0, 0)),
                  pl.BlockSpec((MEM_LEN, 2 * GROUP_W), lambda b, i: (b, 0)),
                  pl.BlockSpec((GROUP_W, D_MODEL), lambda b, i: (0, 0)),
                  pl.BlockSpec((1, D_MODEL), lambda b, i: (0, 0))],
        out_specs=pl.BlockSpec((tq, D_MODEL), lambda b, i: (b * nq + i, 0)),
        out_shape=jax.ShapeDtypeStruct((M, D_MODEL), F32),
        scratch_shapes=[pltpu.VMEM((tq, D_MODEL), BF16)],
        compiler_params=_cparams(("parallel", "parallel"), VMEM_LIMIT),
    )(x, g_pre.reshape(1, D_MODEL), wq, kv, wo, g_post.reshape(1, D_MODEL))


SC_BETA, SC_EG, SC_QKB, SC_LI, SC_LF, SC_QKC = 0, 4, 8, 12, 16, 20


def _sample_pre_kernel(p_ref, pool_ref, dnc_ref, scc_ref, pw_ref, ps_ref, cw_ref, alog_ref, dtb_ref,
                       bi_ref, bf_ref, sw_ref,
                       ya_ref, yd_ref, pool_out_ref, dnc_out_ref, scc_out_ref,
                       rows_b_ref, rows_c_ref, scal_ref, cols_ref, *, start_pos):
    nb = p_ref.shape[0]
    blk = lambda c: p_ref[:, c * GROUP_W:(c + 1) * GROUP_W]

    u = blk(COL_A)
    ys = []
    for g, w in enumerate(POOL_WINDOWS):
        lanes = slice(g * 128, (g + 1) * 128)
        tot = u[:, lanes]
        for r in range(POOL_STATE + 1 - w, POOL_STATE):
            tot = tot + pool_ref[:, r * GROUP_W + g * 128:r * GROUP_W + (g + 1) * 128]
        d = tot / float(min(start_pos + 1, w)) - u[:, lanes]
        ys.append(_bdot(d, pw_ref[g]))
    ya_ref[...] = (jnp.concatenate(ys, axis=1) * ps_ref[...]).astype(ya_ref.dtype)
    pool_out_ref[:, :(POOL_STATE - 1) * GROUP_W] = pool_ref[:, GROUP_W:]
    pool_out_ref[:, (POOL_STATE - 1) * GROUP_W:] = u

    us = blk(COL_DC) * blk(COL_DH)
    sw = sw_ref[...]
    y = sw[0:1] * scc_ref[:, :GROUP_W] + sw[1:2] * scc_ref[:, GROUP_W:] + sw[2:3] * us
    yd_ref[...] = (blk(COL_DB) * y).astype(yd_ref.dtype)
    scc_out_ref[:, :GROUP_W] = scc_ref[:, GROUP_W:]
    scc_out_ref[:, GROUP_W:] = us

    qkv = p_ref[:, COL_BQ * GROUP_W:(COL_BV + 1) * GROUP_W]
    cw = cw_ref[...]
    W3 = 3 * GROUP_W
    conv = cw[DN_CONV - 1:DN_CONV] * qkv
    for j in range(DN_CONV - 1):
        conv = conv + cw[j:j + 1] * dnc_ref[:, j * W3:(j + 1) * W3]
    act = _silu(conv)
    dnc_out_ref[:, :(DN_CONV - 2) * W3] = dnc_ref[:, W3:]
    dnc_out_ref[:, (DN_CONV - 2) * W3:] = qkv

    gates = p_ref[:, GATE_COL0:GATE_COL0 + 128]
    beta_all = _sigmoid(gates)
    eg_all = jnp.exp(-jnp.exp(alog_ref[...]) * _softplus(gates + dtb_ref[...]))
    li_all = gates + bi_ref[...]
    lf_all = -_softplus(-(gates + bf_ref[...]))

    lane = lax.broadcasted_iota(jnp.int32, (nb, 128), 1)
    scal = jnp.zeros((nb, 128), F32)

    def put(tab, lane_idx, colv):
        return jnp.where(lane == lane_idx, colv, tab)

    qs, ks = [], []
    for h in range(N_HEADS):
        sl = slice(h * HEAD_DIM, (h + 1) * HEAD_DIM)
        qh = act[:, sl]
        kh = act[:, GROUP_W + h * HEAD_DIM:GROUP_W + (h + 1) * HEAD_DIM]
        qh = qh * lax.rsqrt(jnp.sum(qh * qh, axis=-1, keepdims=True) + EPS) * HEAD_DIM ** -0.5
        kh = kh * lax.rsqrt(jnp.sum(kh * kh, axis=-1, keepdims=True) + EPS)
        qs.append(qh)
        ks.append(kh)
        scal = put(scal, SC_BETA + h, _col(beta_all, LANE_BETA + h))
        scal = put(scal, SC_EG + h, _col(eg_all, LANE_A + h))
        scal = put(scal, SC_QKB + h, jnp.sum(qh * kh, axis=-1, keepdims=True))
        cols_ref[(0 * N_HEADS + h) * HEAD_DIM:(0 * N_HEADS + h + 1) * HEAD_DIM, :] = qh.T.astype(BF16)
        cols_ref[(1 * N_HEADS + h) * HEAD_DIM:(1 * N_HEADS + h + 1) * HEAD_DIM, :] = kh.T.astype(BF16)
    rows_b_ref[...] = jnp.concatenate(qs + ks + [act[:, 2 * GROUP_W:]], axis=1)

    qc = blk(COL_CQ)
    kc = blk(COL_CK) * HEAD_DIM ** -0.5
    for h in range(N_HEADS):
        sl = slice(h * HEAD_DIM, (h + 1) * HEAD_DIM)
        scal = put(scal, SC_LI + h, _col(li_all, LANE_I + h))
        scal = put(scal, SC_LF + h, _col(lf_all, LANE_F + h))
        scal = put(scal, SC_QKC + h, jnp.sum(qc[:, sl] * kc[:, sl], axis=-1, keepdims=True))
        cols_ref[(2 * N_HEADS + h) * HEAD_DIM:(2 * N_HEADS + h + 1) * HEAD_DIM, :] = qc[:, sl].T.astype(BF16)
        cols_ref[(3 * N_HEADS + h) * HEAD_DIM:(3 * N_HEADS + h + 1) * HEAD_DIM, :] = kc[:, sl].T.astype(BF16)
    rows_c_ref[...] = jnp.concatenate([qc, kc, blk(COL_CV)], axis=1)
    scal_ref[...] = scal


def sample_pre(P, pool_st, dnc_st, scc_st, pool_w, pool_scale, conv_w, a_log, dt_bias, b_i, b_f, sc_w,
               start_pos):
    nb = P.shape[0]
    lane_row = lambda v, lane: jnp.zeros((1, 128), F32).at[0, lane:lane + N_HEADS].set(v)
    W3 = 3 * GROUP_W
    out_shape = [jax.ShapeDtypeStruct((nb, GROUP_W), BF16),
                 jax.ShapeDtypeStruct((nb, GROUP_W), BF16),
                 jax.ShapeDtypeStruct(pool_st.shape, F32),
                 jax.ShapeDtypeStruct(dnc_st.shape, F32),
                 jax.ShapeDtypeStruct(scc_st.shape, F32),
                 jax.ShapeDtypeStruct((nb, W3), F32),
                 jax.ShapeDtypeStruct((nb, W3), F32),
                 jax.ShapeDtypeStruct((nb, 128), F32),
                 jax.ShapeDtypeStruct((4 * N_HEADS * HEAD_DIM, nb), BF16)]
    return pl.pallas_call(
        functools.partial(_sample_pre_kernel, start_pos=start_pos),
        out_shape=out_shape,
        compiler_params=pltpu.CompilerParams(vmem_limit_bytes=VMEM_LIMIT),
    )(P, pool_st, dnc_st, scc_st, pool_w, pool_scale.reshape(1, GROUP_W), conv_w,
      lane_row(a_log, LANE_A), lane_row(dt_bias, LANE_A), lane_row(b_i, LANE_I), lane_row(b_f, LANE_F), sc_w)


def _sample_rec_kernel(cols_ref, rows_b_ref, rows_c_ref, scal_ref, z_ref, og_ref, n_ref, m_ref,
                       dng_ref, mlg_ref, s_ref, c_ref,
                       yb_ref, yc_ref, s_out_ref, c_out_ref, n_out_ref, m_out_ref,
                       ob_ref, hc_ref, *, tb):
    i = pl.program_id(0)
    nb = cols_ref.shape[1]
    row_id = lax.broadcasted_iota(jnp.int32, (nb, 128), 0)
    lane_id = lax.broadcasted_iota(jnp.int32, (1, 128), 1)

    def body(j, carry):
        b = i * tb + j
        onehot = (row_id == b).astype(BF16)
        cols = jnp.dot(cols_ref[...], onehot, preferred_element_type=F32)
        scal = scal_ref[pl.ds(b, 1), :]
        sc = lambda idx: _col(scal, idx)
        m_row = m_ref[pl.ds(b, 1), :]
        rb = rows_b_ref[pl.ds(b, 1), :]
        rc = rows_c_ref[pl.ds(b, 1), :]
        n_all = n_ref[pl.ds(b, 1), :]
        m_new_row = jnp.zeros((1, 128), F32)
        o_rows, h_rows, n_rows = [], [], []
        for h in range(N_HEADS):
            sl = slice(h * HEAD_DIM, (h + 1) * HEAD_DIM)
            colblk = lambda v: cols[(v * N_HEADS + h) * HEAD_DIM:(v * N_HEADS + h + 1) * HEAD_DIM, :]
            s = s_ref[j, h]
            ks = jnp.sum(colblk(1) * s, axis=0, keepdims=True)
            qs = jnp.sum(colblk(0) * s, axis=0, keepdims=True)
            beta, eg, qk = sc(SC_BETA + h), sc(SC_EG + h), sc(SC_QKB + h)
            v_row = rb[:, 2 * GROUP_W + h * HEAD_DIM:2 * GROUP_W + (h + 1) * HEAD_DIM]
            v_new = beta * v_row - (beta * eg) * ks
            o_rows.append(eg * qs + qk * v_new)
            s_out_ref[j, h] = s * eg + colblk(1) * v_new
            cm = c_ref[j, h]
            qc = jnp.sum(colblk(2) * cm, axis=0, keepdims=True)
            q_row = rc[:, sl]
            k_row = rc[:, GROUP_W + h * HEAD_DIM:GROUP_W + (h + 1) * HEAD_DIM]
            vc_row = rc[:, 2 * GROUP_W + h * HEAD_DIM:2 * GROUP_W + (h + 1) * HEAD_DIM]
            n_row = n_all[:, sl]
            li, lf, qkc = sc(SC_LI + h), sc(SC_LF + h), sc(SC_QKC + h)
            m_old = _col(m_row, h)
            bb = lf + m_old
            m_t = jnp.maximum(bb, li)
            w_intra = jnp.exp(li - m_t) * qkc
            w_inter = jnp.exp(bb - m_t)
            num = w_inter * qc + w_intra * vc_row
            den = w_inter * jnp.sum(q_row * n_row, axis=-1, keepdims=True) + w_intra
            h_rows.append(num / jnp.maximum(jnp.abs(den), jnp.exp(-m_t)))
            m_new = jnp.maximum(m_old + lf, li)
            dec = jnp.exp(m_old + lf - m_new)
            e = jnp.exp(li - m_new)
            c_out_ref[j, h] = cm * dec + colblk(3) * (e * vc_row)
            n_rows.append(n_row * dec + e * k_row)
            m_new_row = jnp.where(lane_id == h, m_new, m_new_row)
        ob_ref[pl.ds(b, 1), :] = jnp.concatenate(o_rows, axis=1)
        hc_ref[pl.ds(b, 1), :] = jnp.concatenate(h_rows, axis=1)
        n_out_ref[pl.ds(b, 1), :] = jnp.concatenate(n_rows, axis=1)
        m_out_ref[pl.ds(b, 1), :] = m_new_row
        return carry

    lax.fori_loop(0, tb, body, 0)

    @pl.when(i == pl.num_programs(0) - 1)
    def _():
        ys_b, ys_c = [], []
        for h in range(N_HEADS):
            sl = slice(h * HEAD_DIM, (h + 1) * HEAD_DIM)
            o = ob_ref[:, sl]
            o = o * lax.rsqrt(jnp.mean(o * o, axis=-1, keepdims=True) + EPS) * dng_ref[...]
            ys_b.append(o * _silu(z_ref[:, sl]))
            hh = _sigmoid(og_ref[:, sl]) * hc_ref[:, sl]
            hh = hh * lax.rsqrt(jnp.mean(hh * hh, axis=-1, keepdims=True) + EPS) * mlg_ref[h:h + 1, :]
            ys_c.append(hh)
        yb_ref[...] = jnp.concatenate(ys_b, axis=1).astype(yb_ref.dtype)
        yc_ref[...] = jnp.concatenate(ys_c, axis=1).astype(yc_ref.dtype)


def sample_rec(cols, rows_b, rows_c, scal, P, n_st, m_st, dn_norm_g, ml_norm_g, s_all, c_all, layer, *, tb=8):
    nb = P.shape[0]
    W3 = 3 * GROUP_W
    full = lambda shape: pl.BlockSpec(shape, lambda i: (0,) * len(shape))
    state_in = pl.BlockSpec((None, tb, N_HEADS, HEAD_DIM, HEAD_DIM), lambda i: (layer, i, 0, 0, 0))
    state = pl.BlockSpec((tb, N_HEADS, HEAD_DIM, HEAD_DIM), lambda i: (i, 0, 0, 0))
    s_st = jax.ShapeDtypeStruct(s_all.shape[1:], F32)
    c_st = jax.ShapeDtypeStruct(c_all.shape[1:], F32)
    return pl.pallas_call(
        functools.partial(_sample_rec_kernel, tb=tb),
        grid=(nb // tb,),
        in_specs=[full(cols.shape), full((nb, W3)), full((nb, W3)), full((nb, 128)),
                  pl.BlockSpec((nb, GROUP_W), lambda i: (0, COL_BZ)),
                  pl.BlockSpec((nb, GROUP_W), lambda i: (0, COL_CO)),
                  full((nb, GROUP_W)), full((nb, 128)), full((1, HEAD_DIM)), full((N_HEADS, HEAD_DIM)),
                  state_in, state_in],
        out_specs=[full((nb, GROUP_W)), full((nb, GROUP_W)), state, state,
                   full((nb, GROUP_W)), full((nb, 128))],
        out_shape=[jax.ShapeDtypeStruct((nb, GROUP_W), BF16),
                   jax.ShapeDtypeStruct((nb, GROUP_W), BF16),
                   jax.ShapeDtypeStruct(s_st.shape, F32),
                   jax.ShapeDtypeStruct(c_st.shape, F32),
                   jax.ShapeDtypeStruct((nb, GROUP_W), F32),
                   jax.ShapeDtypeStruct((nb, 128), F32)],
        scratch_shapes=[pltpu.VMEM((nb, GROUP_W), F32), pltpu.VMEM((nb, GROUP_W), F32)],
        compiler_params=_cparams(("arbitrary",), VMEM_LIMIT),
    )(cols, rows_b, rows_c, scal, P, P, n_st, m_st, dn_norm_g.reshape(1, HEAD_DIM), ml_norm_g, s_all, c_all)


def _sample_xattn_kernel(q_ref, k_ref, v_ref, o_ref, *, tb):
    i = pl.program_id(0)
    n_rows = k_ref.shape[1]
    row = lax.broadcasted_iota(jnp.int32, (2 * N_HEADS, n_rows), 0)
    lane = lax.broadcasted_iota(jnp.int32, (2 * N_HEADS, n_rows), 1)
    valid = (lane % N_HEADS) == (row % N_HEADS)

    def body(j, carry):
        b = i * tb + j
        q_row = q_ref[pl.ds(b, 1), :]
        heads = [q_row[:, h * HEAD_DIM:(h + 1) * HEAD_DIM] for h in range(N_HEADS)]
        q8 = jnp.concatenate(heads + heads, axis=0)
        s = _bdot_nt(q8, k_ref[j]) * HEAD_DIM ** -0.5
        s = jnp.where(valid, s, -jnp.inf)
        p = jnp.exp(s - jnp.max(s, axis=-1, keepdims=True))
        p = p / jnp.sum(p, axis=-1, keepdims=True)
        o8 = _bdot(p, v_ref[j])
        o_ref[pl.ds(b, 1), :] = jnp.concatenate([o8[h:h + 1, :] for h in range(N_HEADS)], axis=1)
        return carry

    lax.fori_loop(0, tb, body, 0)


def sample_xattn(q, k_all, v_all, layer, *, tb=8):
    nb = q.shape[0]
    n_rows = k_all.shape[2]
    kv = pl.BlockSpec((None, tb, n_rows, HEAD_DIM), lambda i: (layer, i, 0, 0))
    return pl.pallas_call(
        functools.partial(_sample_xattn_kernel, tb=tb),
        grid=(nb // tb,),
        in_specs=[pl.BlockSpec((nb, GROUP_W), lambda i: (0, 0)), kv, kv],
        out_specs=pl.BlockSpec((nb, GROUP_W), lambda i: (0, 0)),
        out_shape=jax.ShapeDtypeStruct((nb, GROUP_W), F32),
        compiler_params=_cparams(("arbitrary",), VMEM_LIMIT),
    )(q, k_all, v_all)


def _prep_w_in(w_in):
    b0 = 5 * GROUP_W
    c0 = b0 + 2 * N_HEADS
    c1 = c0 + 4 * GROUP_W
    d0 = c1 + 2 * N_HEADS
    d1 = d0 + 3 * GROUP_W
    pad = jnp.zeros(w_in.shape[:-1] + (128 - 4 * N_HEADS,), w_in.dtype)
    return jnp.concatenate([w_in[..., :b0], w_in[..., c0:c1], w_in[..., d0:d1],
                            w_in[..., b0:c0], w_in[..., c1:d0], pad], axis=-1).astype(BF16)


def _pad_ff(w, axis):
    pad = [(0, 0)] * w.ndim
    pad[axis] = (0, D_FF_PAD - D_FF)
    return jnp.pad(w, pad).astype(BF16)


def _ffn(x, g_pre, g_post, wg, wu, wd, *, tm_up, tm_down):
    act = norm_matmul(x, g_pre, [wg, wu], tm=tm_up, tn=FFN_TN, out_dtype=BF16)
    return matmul_resnorm([act], wd, g_post, x, scale=0.5, tm=tm_down)


def _prompt_layer(x, mem2d, B, T, lw):
    g = lw['norm_g']
    x = _ffn(x, g[0], g[1], lw['ffn1_wg'], lw['ffn1_wu'], lw['ffn1_wd'], tm_up=1024, tm_down=256)
    P = norm_matmul(x, g[2], [lw['w_in']], tm=1024, tn=896, out_dtype=F32)
    ya, yd, pool_tail, sc_tail = pool_sconv_prompt(P, B, T, lw['pool_w'], lw['pool_scale'], lw['sc_conv_w'])
    yb, dn_s, dn_tail = deltanet_prompt(P, B, T, lw['dn_conv_w'], lw['dn_A_log'], lw['dn_dt_bias'],
                                        lw['dn_norm_g'])
    yc, ml_c, ml_nm = mlstm_prompt(P, B, T, lw['ml_b_i'], lw['ml_b_f'], lw['ml_norm_g'])
    x = matmul_resnorm([ya, yb, yc, yd], lw['w_out'], g[3], x, scale=1.0, tm=512)
    kv = norm_matmul(mem2d, g[8], [lw['x_wkv']], tm=1024, tn=512, out_dtype=F32)
    x = xattn_prompt(x, B, T, g[4], lw['x_wq'], kv, lw['x_wo'], g[5])
    x = _ffn(x, g[6], g[7], lw['ffn2_wg'], lw['ffn2_wu'], lw['ffn2_wd'], tm_up=1024, tm_down=256)
    states = (pool_tail[:, 16 - POOL_STATE:], dn_tail[:, 8 - (DN_CONV - 1):], dn_s, ml_c,
              ml_nm[:, :N_HEADS], ml_nm[:, N_HEADS:, 0], sc_tail[:, 8 - (SC_WIDTH - 1):])
    mem_k = kv[:, :GROUP_W].reshape(B, MEM_LEN, N_HEADS, HEAD_DIM)
    mem_v = kv[:, GROUP_W:].reshape(B, MEM_LEN, N_HEADS, HEAD_DIM)
    return x, states, mem_k, mem_v


def _sample_layer(x, st, big, layer, lw, start_pos):
    pool_st, dnc_st, ml_n, ml_m, scc_st = st
    s_all, c_all, k_all, v_all = big
    nb = x.shape[0]
    g = lw['norm_g']
    x = _ffn(x, g[0], g[1], lw['ffn1_wg'], lw['ffn1_wu'], lw['ffn1_wd'], tm_up=128, tm_down=128)
    P = norm_matmul(x, g[2], [lw['w_in']], tm=128, tn=896, out_dtype=F32)
    (ya, yd, pool_new, dnc_new, scc_new, rows_b, rows_c, scal, cols) = sample_pre(
        P, pool_st.reshape(nb, -1), dnc_st.reshape(nb, -1), scc_st.reshape(nb, -1),
        lw['pool_w'], lw['pool_scale'], lw['dn_conv_w'], lw['dn_A_log'], lw['dn_dt_bias'],
        lw['ml_b_i'], lw['ml_b_f'], lw['sc_conv_w'], start_pos)
    m_pad = jnp.pad(ml_m, ((0, 0), (0, 128 - N_HEADS)))
    yb, yc, s_new, c_new, n_new, m_new = sample_rec(
        cols, rows_b, rows_c, scal, P, ml_n.reshape(nb, GROUP_W), m_pad,
        lw['dn_norm_g'], lw['ml_norm_g'], s_all, c_all, layer)
    x = matmul_resnorm([ya, yb, yc, yd], lw['w_out'], g[3], x, scale=1.0, tm=128)
    q = norm_matmul(x, g[4], [lw['x_wq']], tm=128, tn=GROUP_W, out_dtype=F32)
    o = sample_xattn(q, k_all, v_all, layer)
    x = matmul_resnorm([o], lw['x_wo'], g[5], x, scale=1.0, tm=128)
    x = _ffn(x, g[6], g[7], lw['ffn2_wg'], lw['ffn2_wu'], lw['ffn2_wd'], tm_up=128, tm_down=128)
    states = (pool_new.reshape(pool_st.shape), dnc_new.reshape(dnc_st.shape), s_new, c_new,
              n_new.reshape(ml_n.shape), m_new[:, :N_HEADS], scc_new.reshape(scc_st.shape))
    return x, states


def kernel(x_prompt, x_sample, mem_prompt, state_pool, state_dn_conv, state_dn_S, state_ml_C, state_ml_n,
           state_ml_m, state_sc_conv, cache_mem_k, cache_mem_v, norm_g, w_in, w_out, pool_w, pool_scale,
           dn_conv_w, dn_A_log, dn_dt_bias, dn_norm_g, ml_b_i, ml_b_f, ml_norm_g, sc_conv_w,
           x_wq, x_wk, x_wv, x_wo, ffn1_wg, ffn1_wu, ffn1_wd, ffn2_wg, ffn2_wu, ffn2_wd):
    depth = norm_g.shape[0]
    B, T, _ = x_prompt.shape
    nb, t_dec, _ = x_sample.shape
    assert t_dec == 1
    start_pos = 16384

    w_in_b = _prep_w_in(w_in)
    w_out_b = w_out.astype(BF16)
    wq_b = x_wq.astype(BF16)
    wkv_b = jnp.concatenate([x_wk, x_wv], axis=-1).astype(BF16)
    wo_b = x_wo.astype(BF16)
    ffn_b = [(_pad_ff(wg, 2), _pad_ff(wu, 2), _pad_ff(wd, 1))
             for wg, wu, wd in ((ffn1_wg, ffn1_wu, ffn1_wd), (ffn2_wg, ffn2_wu, ffn2_wd))]

    def layer_weights(l):
        return dict(norm_g=norm_g[l], w_in=w_in_b[l], w_out=w_out_b[l], pool_w=pool_w[l],
                    pool_scale=pool_scale[l], dn_conv_w=dn_conv_w[l], dn_A_log=dn_A_log[l],
                    dn_dt_bias=dn_dt_bias[l], dn_norm_g=dn_norm_g[l], ml_b_i=ml_b_i[l], ml_b_f=ml_b_f[l],
                    ml_norm_g=ml_norm_g[l], sc_conv_w=sc_conv_w[l], x_wq=wq_b[l], x_wkv=wkv_b[l],
                    x_wo=wo_b[l], ffn1_wg=ffn_b[0][0][l], ffn1_wu=ffn_b[0][1][l], ffn1_wd=ffn_b[0][2][l],
                    ffn2_wg=ffn_b[1][0][l], ffn2_wu=ffn_b[1][1][l], ffn2_wd=ffn_b[1][2][l])

    mem2d = mem_prompt.reshape(B * MEM_LEN, D_MODEL)
    h = x_prompt.reshape(B * T, D_MODEL)
    p_states, mem_k_list, mem_v_list = [], [], []
    for l in range(depth):
        h, ns, mk, mv = _prompt_layer(h, mem2d, B, T, layer_weights(l))
        p_states.append(ns)
        mem_k_list.append(mk)
        mem_v_list.append(mv)
    y_prompt = h.reshape(B, T, D_MODEL)

    s_inputs = (state_pool, state_dn_conv, state_ml_n, state_ml_m, state_sc_conv)
    big = (state_dn_S, state_ml_C,
           cache_mem_k.reshape(depth, nb, MEM_LEN * N_HEADS, HEAD_DIM),
           cache_mem_v.reshape(depth, nb, MEM_LEN * N_HEADS, HEAD_DIM))
    h = x_sample.reshape(nb, D_MODEL)
    s_states = []
    for l in range(depth):
        st = tuple(s[l] for s in s_inputs)
        h, ns = _sample_layer(h, st, big, l, layer_weights(l), start_pos)
        s_states.append(ns)
    y_sample = h.reshape(nb, 1, D_MODEL)

    pool_p, dn_conv_p, dn_S_p, ml_C_p, ml_n_p, ml_m_p, sc_conv_p = [jnp.stack(z) for z in zip(*p_states)]
    pool_s, dn_conv_s, dn_S_s, ml_C_s, ml_n_s, ml_m_s, sc_conv_s = [jnp.stack(z) for z in zip(*s_states)]
    mem_k_p = jnp.stack(mem_k_list)
    mem_v_p = jnp.stack(mem_v_list)
    return (y_prompt, y_sample, pool_p, pool_s, dn_conv_p, dn_conv_s, dn_S_p, dn_S_s, ml_C_p, ml_C_s,
            ml_n_p, ml_n_s, ml_m_p, ml_m_s, sc_conv_p, sc_conv_s, mem_k_p, mem_v_p)
```

```python
import functools

import numpy as np
import jax
import jax.numpy as jnp
from jax import lax
from jax.experimental import pallas as pl
from jax.experimental.pallas import tpu as pltpu

F32 = jnp.float32
BF16 = jnp.bfloat16

EPS = 1e-6
D_MODEL = 2048
GROUP_W = 512
HEAD_DIM = 128
N_HEADS = GROUP_W // HEAD_DIM
CHUNK = 64
POOL_WINDOWS = (2, 4, 8, 16)
POOL_STATE = 15
DN_CONV = 4
SC_WIDTH = 3
MEM_LEN = 256
PAST_LEN = 16384
FFN_TN = 512
COL_A, COL_BQ, COL_BK, COL_BV, COL_BZ = 0, 1, 2, 3, 4
COL_CQ, COL_CK, COL_CV, COL_CO = 5, 6, 7, 8
COL_DB, COL_DC, COL_DH = 9, 10, 11
GATE_COL0 = 12 * GROUP_W
IN_W_PAD = GATE_COL0 + 128
LANE_BETA, LANE_A, LANE_I, LANE_F = 0, 4, 8, 12
TB = 256
VMEM_LIMIT = 56 * 2**20


def _cparams(sem, vmem=None):
    return pltpu.CompilerParams(dimension_semantics=sem, vmem_limit_bytes=vmem)


def _bdot(a, b):
    return jnp.dot(a.astype(BF16), b.astype(BF16), preferred_element_type=F32)


def _bdot_nt(a, b):
    return lax.dot_general(a.astype(BF16), b.astype(BF16), (((1,), (1,)), ((), ())),
                           preferred_element_type=F32)


def _sigmoid(x):
    return 1.0 / (1.0 + jnp.exp(-x))


def _silu(x):
    return x * _sigmoid(x)


def _softplus(x):
    return jnp.maximum(x, 0.0) + jnp.log(1.0 + jnp.exp(-jnp.abs(x)))


def _col(x, idx):
    return x[:, idx:idx + 1]


def _resident(shape):
    nd = len(shape)
    return pl.BlockSpec(shape, lambda *_: (0,) * nd, pipeline_mode=pl.Buffered(1))


def _norm_into(x_ref, g_ref, h_ref):
    tm = x_ref.shape[0]
    rs = min(tm, 128)

    def body(i, carry):
        r = pl.multiple_of(i * rs, rs)
        x = x_ref[pl.ds(r, rs), :]
        ms = jnp.mean(x * x, axis=-1, keepdims=True)
        h_ref[pl.ds(r, rs), :] = (x * lax.rsqrt(ms + EPS) * g_ref[...]).astype(BF16)
        return carry

    lax.fori_loop(0, tm // rs, body, 0)


def _norm_mm_kernel(x_ref, g_ref, w_ref, o_ref, h_ref):
    @pl.when(pl.program_id(1) == 0)
    def _():
        _norm_into(x_ref, g_ref, h_ref)

    o_ref[...] = jnp.dot(h_ref[...], w_ref[...], preferred_element_type=F32).astype(o_ref.dtype)


def _norm_swiglu_kernel(x_ref, g_ref, wg_ref, wu_ref, o_ref, h_ref):
    @pl.when(pl.program_id(1) == 0)
    def _():
        _norm_into(x_ref, g_ref, h_ref)

    h = h_ref[...]
    a = jnp.dot(h, wg_ref[...], preferred_element_type=F32)
    b = jnp.dot(h, wu_ref[...], preferred_element_type=F32)
    o_ref[...] = (_silu(a) * b).astype(o_ref.dtype)


def norm_matmul(x, g, ws, layer, *, tm, tn, out_dtype):
    M, K = x.shape
    n_steps = pl.cdiv(ws[0].shape[2], tn)
    N = n_steps * tn
    tm = min(tm, M)
    kern = _norm_swiglu_kernel if len(ws) == 2 else _norm_mm_kernel
    return pl.pallas_call(
        kern,
        grid=(M // tm, n_steps),
        in_specs=[pl.BlockSpec((tm, K), lambda i, j: (i, 0)),
                  pl.BlockSpec((1, K), lambda i, j: (0, 0))]
        + [pl.BlockSpec((None, K, tn), lambda i, j: (layer, 0, j)) for _ in ws],
        out_specs=pl.BlockSpec((tm, tn), lambda i, j: (i, j)),
        out_shape=jax.ShapeDtypeStruct((M, N), out_dtype),
        scratch_shapes=[pltpu.VMEM((tm, K), BF16)],
        compiler_params=_cparams(("parallel", "arbitrary"), VMEM_LIMIT),
    )(x, g.reshape(1, K), *ws)


def _mm_resnorm_kernel(*refs, n_a, scale):
    a_refs = refs[:n_a]
    w_ref, g_ref, res_ref, o_ref = refs[n_a:]
    parts = [r[...].astype(BF16) for r in a_refs]
    a = parts[0] if n_a == 1 else jnp.concatenate(parts, axis=1)
    y = jnp.dot(a, w_ref[...], preferred_element_type=F32)
    ms = jnp.mean(y * y, axis=-1, keepdims=True)
    o_ref[...] = res_ref[...] + scale * (y * lax.rsqrt(ms + EPS) * g_ref[...])


def matmul_resnorm(a_list, w, layer, g, res, *, scale, tm):
    M, N = res.shape
    tm = min(tm, M)
    K = w.shape[1]
    widths = [K] if len(a_list) == 1 else [a.shape[1] for a in a_list]
    assert sum(widths) == K
    w_spec = pl.BlockSpec((None, K, N), lambda i: (layer, 0, 0), pipeline_mode=pl.Buffered(1))
    return pl.pallas_call(
        functools.partial(_mm_resnorm_kernel, n_a=len(a_list), scale=scale),
        grid=(M // tm,),
        in_specs=[pl.BlockSpec((tm, wd), lambda i: (i, 0)) for wd in widths]
        + [w_spec, _resident((1, N)), pl.BlockSpec((tm, N), lambda i: (i, 0))],
        out_specs=pl.BlockSpec((tm, N), lambda i: (i, 0)),
        out_shape=jax.ShapeDtypeStruct((M, N), F32),
        compiler_params=_cparams(("parallel",), VMEM_LIMIT),
    )(*a_list, w, g.reshape(1, N), res)


def _chunk_cumsum(x):
    n = x.shape[0]
    row = lax.broadcasted_iota(jnp.int32, (n, 1), 0) % CHUNK
    s = 1
    while s < CHUNK:
        x = x + jnp.where(row >= s, pltpu.roll(x, s, 0), 0.0)
        s *= 2
    return x


def _chunk_last(x):
    n = x.shape[0]
    parts = [jnp.broadcast_to(x[c * CHUNK + CHUNK - 1:c * CHUNK + CHUNK, :], (CHUNK, x.shape[1]))
             for c in range(n // CHUNK)]
    return jnp.concatenate(parts, axis=0)


def _pad_rows(x, c, n_chunks):
    z = jnp.zeros_like(x)
    return jnp.concatenate([x if i == c else z for i in range(n_chunks)], axis=0)


def _blk_masks(n):
    r = lax.broadcasted_iota(jnp.int32, (n, n), 0)
    c = lax.broadcasted_iota(jnp.int32, (n, n), 1)
    same = (r // CHUNK) == (c // CHUNK)
    return same & (c <= r), same & (c < r), r == c


def _split(a):
    ah = a.astype(BF16)
    return ah, (a - ah.astype(F32)).astype(BF16)


def _split_dot(a, b):
    d = lambda x, y: jnp.dot(x, y, preferred_element_type=F32)
    return d(a[0], b[0]) + d(a[0], b[1]) + d(a[1], b[0])


def _unit_lower_inverses(a_list, eye):
    ns = [-a for a in a_list]
    ps = [eye + n for n in ns]
    n_sp = [_split(n) for n in ns]
    s = 2
    while s < CHUNK:
        ns = [_split_dot(x, x) for x in n_sp]
        n_sp = [_split(n) for n in ns]
        ps = [p + _split_dot(_split(p), x) for p, x in zip(ps, n_sp)]
        s *= 2
    return ps


def _pool_sconv_kernel(ua_ref, db_ref, dc_ref, dh_ref, pw_ref, ps_ref, sw_ref,
                       ya_ref, yd_ref, ptail_ref, stail_ref, hist_a, hist_d, *, tb):
    i = pl.program_id(1)

    @pl.when(i == 0)
    def _():
        hist_a[...] = jnp.zeros_like(hist_a)
        hist_d[...] = jnp.zeros_like(hist_d)

    u = ua_ref[...]
    ext = jnp.concatenate([hist_a[...], u], axis=0)
    a2 = ext + pltpu.roll(ext, 1, 0)
    a4 = a2[:, 128:] + pltpu.roll(a2[:, 128:], 2, 0)
    a8 = a4[:, 128:] + pltpu.roll(a4[:, 128:], 4, 0)
    a16 = a8[:, 128:] + pltpu.roll(a8[:, 128:], 8, 0)
    sums = (a2[16:, :128], a4[16:, :128], a8[16:, :128], a16[16:, :])
    pos = i * tb + lax.broadcasted_iota(jnp.int32, (tb, 1), 0)
    ys = []
    for g, w in enumerate(POOL_WINDOWS):
        cnt = jnp.minimum(pos + 1, w).astype(F32)
        d = sums[g] / cnt - u[:, g * 128:(g + 1) * 128]
        ys.append(_bdot(d, pw_ref[g]))
    ya_ref[...] = (jnp.concatenate(ys, axis=1) * ps_ref[...]).astype(ya_ref.dtype)
    hist_a[...] = u[tb - 16:, :]

    us = dc_ref[...] * dh_ref[...]
    ext = jnp.concatenate([hist_d[...], us], axis=0)
    sw = sw_ref[...]
    y = sw[0:1] * pltpu.roll(ext, 2, 0) + sw[1:2] * pltpu.roll(ext, 1, 0) + sw[2:3] * ext
    yd_ref[...] = (db_ref[...] * y[8:]).astype(yd_ref.dtype)
    hist_d[...] = us[tb - 8:, :]

    @pl.when(i == pl.num_programs(1) - 1)
    def _():
        ptail_ref[0] = u[tb - 16:, :]
        stail_ref[0] = us[tb - 8:, :]


def pool_sconv_prompt(P, B, T, pool_w, pool_scale, sc_w):
    nT = T // TB
    M = B * T
    col = lambda c: pl.BlockSpec((TB, GROUP_W), lambda b, i: (b * nT + i, c))
    return pl.pallas_call(
        functools.partial(_pool_sconv_kernel, tb=TB),
        grid=(B, nT),
        in_specs=[col(COL_A), col(COL_DB), col(COL_DC), col(COL_DH),
                  pl.BlockSpec((4, 128, 128), lambda b, i: (0, 0, 0)),
                  pl.BlockSpec((1, GROUP_W), lambda b, i: (0, 0)),
                  pl.BlockSpec((SC_WIDTH, GROUP_W), lambda b, i: (0, 0))],
        out_specs=[pl.BlockSpec((TB, GROUP_W), lambda b, i: (b * nT + i, 0)),
                   pl.BlockSpec((TB, GROUP_W), lambda b, i: (b * nT + i, 0)),
                   pl.BlockSpec((1, 16, GROUP_W), lambda b, i: (b, 0, 0)),
                   pl.BlockSpec((1, 8, GROUP_W), lambda b, i: (b, 0, 0))],
        out_shape=[jax.ShapeDtypeStruct((M, GROUP_W), BF16),
                   jax.ShapeDtypeStruct((M, GROUP_W), BF16),
                   jax.ShapeDtypeStruct((B, 16, GROUP_W), F32),
                   jax.ShapeDtypeStruct((B, 8, GROUP_W), F32)],
        scratch_shapes=[pltpu.VMEM((16, GROUP_W), F32), pltpu.VMEM((8, GROUP_W), F32)],
        compiler_params=_cparams(("parallel", "arbitrary"), VMEM_LIMIT),
    )(P, P, P, P, pool_w, pool_scale.reshape(1, GROUP_W), sc_w)


def _deltanet_kernel(q_ref, k_ref, v_ref, z_ref, gt_ref, cw_ref, alog_ref, dtb_ref, ng_ref,
                     y_ref, s_out_ref, ctail_ref, s_ref, hist_ref, *, tb):
    i = pl.program_id(1)
    nc = tb // CHUNK

    @pl.when(i == 0)
    def _():
        s_ref[...] = jnp.zeros_like(s_ref)
        hist_ref[...] = jnp.zeros_like(hist_ref)

    qkv = jnp.concatenate([q_ref[...], k_ref[...], v_ref[...]], axis=1)
    ext = jnp.concatenate([hist_ref[...], qkv], axis=0)
    cw = cw_ref[...]
    conv = (cw[0:1] * pltpu.roll(ext, 3, 0) + cw[1:2] * pltpu.roll(ext, 2, 0)
            + cw[2:3] * pltpu.roll(ext, 1, 0) + cw[3:4] * ext)[8:]
    act = _silu(conv)
    hist_ref[...] = qkv[tb - 8:, :]

    gates = gt_ref[...]
    beta_all = _sigmoid(gates)
    g_all = -jnp.exp(alog_ref[...]) * _softplus(gates + dtb_ref[...])
    gcum_all = _chunk_cumsum(g_all)
    glast_all = _chunk_last(gcum_all)
    gcum_t = gcum_all.T
    causal, strict, diag = _blk_masks(tb)
    eye = diag.astype(F32)

    heads = range(N_HEADS)
    hd = lambda base, h: slice(base + h * HEAD_DIM, base + (h + 1) * HEAD_DIM)
    qs, ks, k_ts, a_mats, qks, gcs, gls, betas = [], [], [], [], [], [], [], []
    for h in heads:
        qh = act[:, hd(0, h)]
        kh = act[:, hd(GROUP_W, h)]
        qh = qh * lax.rsqrt(jnp.sum(qh * qh, axis=-1, keepdims=True) + EPS) * HEAD_DIM ** -0.5
        kh = kh * lax.rsqrt(jnp.sum(kh * kh, axis=-1, keepdims=True) + EPS)
        beta = _col(beta_all, LANE_BETA + h)
        gc = _col(gcum_all, LANE_A + h)
        gr = gcum_t[LANE_A + h:LANE_A + h + 1, :]
        decay = jnp.where(causal, jnp.exp(jnp.where(causal, gc - gr, 0.0)), 0.0)
        k_t = kh.T.astype(BF16)
        a_mats.append(jnp.where(strict, _bdot(kh * beta, k_t) * decay, 0.0))
        qks.append(jnp.where(causal, _bdot(qh, k_t) * decay, 0.0).astype(BF16))
        qs.append(qh)
        ks.append(kh)
        k_ts.append(k_t)
        gcs.append(gc)
        gls.append(_col(glast_all, LANE_A + h))
        betas.append(beta)
    tinvs = _unit_lower_inverses(a_mats, eye)
    us, ws, q_decs, tails = [], [], [], []
    for h in heads:
        egc = jnp.exp(gcs[h])
        vb = act[:, hd(2 * GROUP_W, h)] * betas[h]
        uw = _bdot(tinvs[h], jnp.concatenate([vb, ks[h] * (betas[h] * egc)], axis=1))
        us.append(uw[:, :HEAD_DIM])
        ws.append(uw[:, HEAD_DIM:].astype(BF16))
        q_decs.append((qs[h] * egc).astype(BF16))
        tails.append(jnp.exp(gls[h] - gcs[h]))
    s_hs = [s_ref[h] for h in heads]
    outs = [[] for _ in heads]
    for c in range(nc):
        r = slice(c * CHUNK, (c + 1) * CHUNK)
        for h in heads:
            s_b = s_hs[h].astype(BF16)
            v_new = us[h][r] - jnp.dot(ws[h][r], s_b, preferred_element_type=F32)
            outs[h].append(jnp.dot(q_decs[h][r], s_b, preferred_element_type=F32)
                           + _bdot(qks[h][r], _pad_rows(v_new, c, nc)))
            s_dec = jnp.exp(jnp.broadcast_to(gls[h][c * CHUNK:c * CHUNK + 1], (HEAD_DIM, 1)))
            s_hs[h] = s_hs[h] * s_dec + _bdot(k_ts[h], _pad_rows(v_new * tails[h][r], c, nc))
    ys = []
    for h in heads:
        s_ref[h] = s_hs[h]
        o = jnp.concatenate(outs[h], axis=0)
        o = o * lax.rsqrt(jnp.mean(o * o, axis=-1, keepdims=True) + EPS) * ng_ref[...]
        ys.append(o * _silu(z_ref[:, hd(0, h)]))
    y_ref[...] = jnp.concatenate(ys, axis=1).astype(y_ref.dtype)

    @pl.when(i == pl.num_programs(1) - 1)
    def _():
        s_out_ref[0] = s_ref[...]
        ctail_ref[0] = qkv[tb - 8:, :]


def deltanet_prompt(P, B, T, conv_w, a_log, dt_bias, norm_g):
    nT = T // TB
    M = B * T
    col = lambda c: pl.BlockSpec((TB, GROUP_W), lambda b, i: (b * nT + i, c))
    lane_row = lambda v, lane: jnp.zeros((1, 128), F32).at[0, lane:lane + N_HEADS].set(v)
    const = lambda shape: pl.BlockSpec(shape, lambda b, i: (0,) * len(shape))
    return pl.pallas_call(
        functools.partial(_deltanet_kernel, tb=TB),
        grid=(B, nT),
        in_specs=[col(COL_BQ), col(COL_BK), col(COL_BV), col(COL_BZ),
                  pl.BlockSpec((TB, 128), lambda b, i: (b * nT + i, GATE_COL0 // 128)),
                  const((DN_CONV, 3 * GROUP_W)), const((1, 128)), const((1, 128)), const((1, HEAD_DIM))],
        out_specs=[pl.BlockSpec((TB, GROUP_W), lambda b, i: (b * nT + i, 0)),
                   pl.BlockSpec((1, N_HEADS, HEAD_DIM, HEAD_DIM), lambda b, i: (b, 0, 0, 0)),
                   pl.BlockSpec((1, 8, 3 * GROUP_W), lambda b, i: (b, 0, 0))],
        out_shape=[jax.ShapeDtypeStruct((M, GROUP_W), BF16),
                   jax.ShapeDtypeStruct((B, N_HEADS, HEAD_DIM, HEAD_DIM), F32),
                   jax.ShapeDtypeStruct((B, 8, 3 * GROUP_W), F32)],
        scratch_shapes=[pltpu.VMEM((N_HEADS, HEAD_DIM, HEAD_DIM), F32),
                        pltpu.VMEM((8, 3 * GROUP_W), F32)],
        compiler_params=_cparams(("parallel", "arbitrary"), VMEM_LIMIT),
    )(P, P, P, P, P, conv_w, lane_row(a_log, LANE_A), lane_row(dt_bias, LANE_A),
      norm_g.reshape(1, HEAD_DIM))


def _mlstm_kernel(q_ref, k_ref, v_ref, og_ref, gt_ref, bi_ref, bf_ref, ng_ref,
                  y_ref, c_out_ref, nm_out_ref, c_ref, nm_ref, *, tb):
    i = pl.program_id(1)
    nc = tb // CHUNK

    @pl.when(i == 0)
    def _():
        c_ref[...] = jnp.zeros_like(c_ref)
        nm_ref[...] = jnp.zeros_like(nm_ref)

    gates = gt_ref[...]
    li_all = gates + bi_ref[...]
    lf_all = -_softplus(-(gates + bf_ref[...]))
    f_all = _chunk_cumsum(lf_all)
    flast_all = _chunk_last(f_all)
    f_t = f_all.T
    li_t = li_all.T
    causal, _, _ = _blk_masks(tb)

    heads = range(N_HEADS)
    hd = lambda h: slice(h * HEAD_DIM, (h + 1) * HEAD_DIM)
    qs, kss, k_ts, evs, eks, intras, den_intras, w_inters, inv_floor, decs = ([] for _ in range(10))
    for h in heads:
        qh = q_ref[:, hd(h)]
        kh = k_ref[:, hd(h)] * HEAD_DIM ** -0.5
        vh = v_ref[:, hd(h)]
        fc = _col(f_all, LANE_F + h)
        fl = _col(flast_all, LANE_F + h)
        a_w = fl - fc + _col(li_all, LANE_I + h)
        fr = f_t[LANE_F + h:LANE_F + h + 1, :]
        lir = li_t[LANE_I + h:LANE_I + h + 1, :]
        dm = jnp.where(causal, fc - fr + lir, -jnp.inf)
        mx = jnp.max(dm, axis=1, keepdims=True)
        m_h = nm_ref[N_HEADS + h:N_HEADS + h + 1, 0:1]
        m_prev, m_next, dec_h = [], [], []
        for c in range(nc):
            r = slice(c * CHUNK, (c + 1) * CHUNK)
            fl_c = fl[c * CHUNK:c * CHUNK + 1]
            m_new = jnp.maximum(m_h + fl_c, jnp.max(a_w[r], axis=0, keepdims=True))
            dec_h.append(jnp.exp(m_h + fl_c - m_new))
            m_prev.append(jnp.broadcast_to(m_h, (CHUNK, 1)))
            m_next.append(jnp.broadcast_to(m_new, (CHUNK, 1)))
            m_h = m_new
        nm_ref[N_HEADS + h:N_HEADS + h + 1, :] = jnp.broadcast_to(m_h, (1, HEAD_DIM))
        b = fc + jnp.concatenate(m_prev, axis=0)
        m_t = jnp.maximum(b, mx)
        k_t = kh.T.astype(BF16)
        w_intra = jnp.exp(dm - m_t) * _bdot(qh, k_t)
        e = jnp.exp(a_w - jnp.concatenate(m_next, axis=0))
        qs.append(qh)
        k_ts.append(k_t)
        evs.append(e * vh)
        eks.append(e * kh)
        intras.append(_bdot(w_intra, vh))
        den_intras.append(jnp.sum(w_intra, axis=-1, keepdims=True))
        w_inters.append(jnp.exp(b - m_t))
        inv_floor.append(jnp.exp(-m_t))
        decs.append(dec_h)
    c_hs = [c_ref[h] for h in heads]
    n_hs = [nm_ref[h:h + 1, :] for h in heads]
    outs = [[] for _ in heads]
    for c in range(nc):
        r = slice(c * CHUNK, (c + 1) * CHUNK)
        for h in heads:
            q_c = qs[h][r]
            num = w_inters[h][r] * _bdot(q_c, c_hs[h]) + intras[h][r]
            den = w_inters[h][r] * jnp.sum(q_c * n_hs[h], axis=-1, keepdims=True) + den_intras[h][r]
            outs[h].append(num / jnp.maximum(jnp.abs(den), inv_floor[h][r]))
            c_hs[h] = c_hs[h] * decs[h][c] + _bdot(k_ts[h], _pad_rows(evs[h][r], c, nc))
            n_hs[h] = n_hs[h] * decs[h][c] + jnp.sum(eks[h][r], axis=0, keepdims=True)
    ys = []
    for h in heads:
        c_ref[h] = c_hs[h]
        nm_ref[h:h + 1, :] = n_hs[h]
        hh = jnp.concatenate(outs[h], axis=0)
        hh = _sigmoid(og_ref[:, hd(h)]) * hh
        hh = hh * lax.rsqrt(jnp.mean(hh * hh, axis=-1, keepdims=True) + EPS) * ng_ref[h:h + 1, :]
        ys.append(hh)
    y_ref[...] = jnp.concatenate(ys, axis=1).astype(y_ref.dtype)

    @pl.when(i == pl.num_programs(1) - 1)
    def _():
        c_out_ref[0] = c_ref[...]
        nm_out_ref[0] = nm_ref[...]


def mlstm_prompt(P, B, T, b_i, b_f, norm_g):
    nT = T // TB
    M = B * T
    col = lambda c: pl.BlockSpec((TB, GROUP_W), lambda b, i: (b * nT + i, c))
    lane_row = lambda v, lane: jnp.zeros((1, 128), F32).at[0, lane:lane + N_HEADS].set(v)
    const = lambda shape: pl.BlockSpec(shape, lambda b, i: (0,) * len(shape))
    return pl.pallas_call(
        functools.partial(_mlstm_kernel, tb=TB),
        grid=(B, nT),
        in_specs=[col(COL_CQ), col(COL_CK), col(COL_CV), col(COL_CO),
                  pl.BlockSpec((TB, 128), lambda b, i: (b * nT + i, GATE_COL0 // 128)),
                  const((1, 128)), const((1, 128)), const((N_HEADS, HEAD_DIM))],
        out_specs=[pl.BlockSpec((TB, GROUP_W), lambda b, i: (b * nT + i, 0)),
                   pl.BlockSpec((1, N_HEADS, HEAD_DIM, HEAD_DIM), lambda b, i: (b, 0, 0, 0)),
                   pl.BlockSpec((1, 2 * N_HEADS, HEAD_DIM), lambda b, i: (b, 0, 0))],
        out_shape=[jax.ShapeDtypeStruct((M, GROUP_W), BF16),
                   jax.ShapeDtypeStruct((B, N_HEADS, HEAD_DIM, HEAD_DIM), F32),
                   jax.ShapeDtypeStruct((B, 2 * N_HEADS, HEAD_DIM), F32)],
        scratch_shapes=[pltpu.VMEM((N_HEADS, HEAD_DIM, HEAD_DIM), F32),
                        pltpu.VMEM((2 * N_HEADS, HEAD_DIM), F32)],
        compiler_params=_cparams(("parallel", "arbitrary"), VMEM_LIMIT),
    )(P, P, P, P, P, lane_row(b_i, LANE_I), lane_row(b_f, LANE_F), norm_g)


def _xattn_prompt_kernel(x_ref, gpre_ref, wq_ref, kv_ref, wo_ref, gpost_ref, o_ref, h_ref):
    _norm_into(x_ref, gpre_ref, h_ref)
    q = jnp.dot(h_ref[...], wq_ref[...], preferred_element_type=F32)
    outs = []
    for h in range(N_HEADS):
        sl = slice(h * HEAD_DIM, (h + 1) * HEAD_DIM)
        k_h = kv_ref[:, sl]
        v_h = kv_ref[:, GROUP_W + h * HEAD_DIM:GROUP_W + (h + 1) * HEAD_DIM]
        s = _bdot_nt(q[:, sl], k_h) * HEAD_DIM ** -0.5
        p = jnp.exp(s - jnp.max(s, axis=-1, keepdims=True))
        p = p / jnp.sum(p, axis=-1, keepdims=True)
        outs.append(_bdot(p, v_h))
    o = jnp.concatenate(outs, axis=1).astype(BF16)
    y = jnp.dot(o, wo_ref[...], preferred_element_type=F32)
    ms = jnp.mean(y * y, axis=-1, keepdims=True)
    o_ref[...] = x_ref[...] + y * lax.rsqrt(ms + EPS) * gpost_ref[...]


def xattn_prompt(x, B, T, g_pre, wq, kv, wo, layer, g_post, *, tq=512):
    nq = T // tq
    M = B * T
    return pl.pallas_call(
        _xattn_prompt_kernel,
        grid=(B, nq),
        in_specs=[pl.BlockSpec((tq, D_MODEL), lambda b, i: (b * nq + i, 0)),
                  pl.BlockSpec((1, D_MODEL), lambda b, i: (0, 0)),
                  pl.BlockSpec((None, D_MODEL, GROUP_W), lambda b, i: (layer, 0, 0)),
                  pl.BlockSpec((MEM_LEN, 2 * GROUP_W), lambda b, i: (b, 0)),
                  pl.BlockSpec((None, GROUP_W, D_MODEL), lambda b, i: (layer, 0, 0)),
                  pl.BlockSpec((1, D_MODEL), lambda b, i: (0, 0))],
        out_specs=pl.BlockSpec((tq, D_MODEL), lambda b, i: (b * nq + i, 0)),
        out_shape=jax.ShapeDtypeStruct((M, D_MODEL), F32),
        scratch_shapes=[pltpu.VMEM((tq, D_MODEL), BF16)],
        compiler_params=_cparams(("parallel", "parallel"), VMEM_LIMIT),
    )(x, g_pre.reshape(1, D_MODEL), wq, kv, wo, g_post.reshape(1, D_MODEL))


SC_BETA, SC_EG, SC_QKB, SC_LI, SC_LF, SC_QKC = 0, 4, 8, 12, 16, 20


def _sample_pre_kernel(p_ref, pool_ref, dnc_ref, scc_ref, pw_ref, ps_ref, cw_ref, alog_ref, dtb_ref,
                       bi_ref, bf_ref, sw_ref,
                       ya_ref, yd_ref, pool_out_ref, dnc_out_ref, scc_out_ref,
                       rows_b_ref, rows_c_ref, scal_ref, cols_ref, *, start_pos):
    nb = p_ref.shape[0]
    blk = lambda c: p_ref[:, c * GROUP_W:(c + 1) * GROUP_W]

    u = blk(COL_A)
    ys = []
    for g, w in enumerate(POOL_WINDOWS):
        lanes = slice(g * 128, (g + 1) * 128)
        tot = u[:, lanes]
        for r in range(POOL_STATE + 1 - w, POOL_STATE):
            tot = tot + pool_ref[:, r * GROUP_W + g * 128:r * GROUP_W + (g + 1) * 128]
        d = tot / float(min(start_pos + 1, w)) - u[:, lanes]
        ys.append(_bdot(d, pw_ref[g]))
    ya_ref[...] = (jnp.concatenate(ys, axis=1) * ps_ref[...]).astype(ya_ref.dtype)
    pool_out_ref[:, :(POOL_STATE - 1) * GROUP_W] = pool_ref[:, GROUP_W:]
    pool_out_ref[:, (POOL_STATE - 1) * GROUP_W:] = u

    us = blk(COL_DC) * blk(COL_DH)
    sw = sw_ref[...]
    y = sw[0:1] * scc_ref[:, :GROUP_W] + sw[1:2] * scc_ref[:, GROUP_W:] + sw[2:3] * us
    yd_ref[...] = (blk(COL_DB) * y).astype(yd_ref.dtype)
    scc_out_ref[:, :GROUP_W] = scc_ref[:, GROUP_W:]
    scc_out_ref[:, GROUP_W:] = us

    qkv = p_ref[:, COL_BQ * GROUP_W:(COL_BV + 1) * GROUP_W]
    cw = cw_ref[...]
    W3 = 3 * GROUP_W
    conv = cw[DN_CONV - 1:DN_CONV] * qkv
    for j in range(DN_CONV - 1):
        conv = conv + cw[j:j + 1] * dnc_ref[:, j * W3:(j + 1) * W3]
    act = _silu(conv)
    dnc_out_ref[:, :(DN_CONV - 2) * W3] = dnc_ref[:, W3:]
    dnc_out_ref[:, (DN_CONV - 2) * W3:] = qkv

    gates = p_ref[:, GATE_COL0:GATE_COL0 + 128]
    beta_all = _sigmoid(gates)
    eg_all = jnp.exp(-jnp.exp(alog_ref[...]) * _softplus(gates + dtb_ref[...]))
    li_all = gates + bi_ref[...]
    lf_all = -_softplus(-(gates + bf_ref[...]))

    lane = lax.broadcasted_iota(jnp.int32, (nb, 128), 1)
    scal = jnp.zeros((nb, 128), F32)

    def put(tab, lane_idx, colv):
        return jnp.where(lane == lane_idx, colv, tab)

    qs, ks = [], []
    for h in range(N_HEADS):
        sl = slice(h * HEAD_DIM, (h + 1) * HEAD_DIM)
        qh = act[:, sl]
        kh = act[:, GROUP_W + h * HEAD_DIM:GROUP_W + (h + 1) * HEAD_DIM]
        qh = qh * lax.rsqrt(jnp.sum(qh * qh, axis=-1, keepdims=True) + EPS) * HEAD_DIM ** -0.5
        kh = kh * lax.rsqrt(jnp.sum(kh * kh, axis=-1, keepdims=True) + EPS)
        qs.append(qh)
        ks.append(kh)
        scal = put(scal, SC_BETA + h, _col(beta_all, LANE_BETA + h))
        scal = put(scal, SC_EG + h, _col(eg_all, LANE_A + h))
        scal = put(scal, SC_QKB + h, jnp.sum(qh * kh, axis=-1, keepdims=True))
        cols_ref[(0 * N_HEADS + h) * HEAD_DIM:(0 * N_HEADS + h + 1) * HEAD_DIM, :] = qh.T.astype(BF16)
        cols_ref[(1 * N_HEADS + h) * HEAD_DIM:(1 * N_HEADS + h + 1) * HEAD_DIM, :] = kh.T.astype(BF16)
    rows_b_ref[...] = jnp.concatenate(qs + ks + [act[:, 2 * GROUP_W:]], axis=1)

    qc = blk(COL_CQ)
    kc = blk(COL_CK) * HEAD_DIM ** -0.5
    for h in range(N_HEADS):
        sl = slice(h * HEAD_DIM, (h + 1) * HEAD_DIM)
        scal = put(scal, SC_LI + h, _col(li_all, LANE_I + h))
        scal = put(scal, SC_LF + h, _col(lf_all, LANE_F + h))
        scal = put(scal, SC_QKC + h, jnp.sum(qc[:, sl] * kc[:, sl], axis=-1, keepdims=True))
        cols_ref[(2 * N_HEADS + h) * HEAD_DIM:(2 * N_HEADS + h + 1) * HEAD_DIM, :] = qc[:, sl].T.astype(BF16)
        cols_ref[(3 * N_HEADS + h) * HEAD_DIM:(3 * N_HEADS + h + 1) * HEAD_DIM, :] = kc[:, sl].T.astype(BF16)
    rows_c_ref[...] = jnp.concatenate([qc, kc, blk(COL_CV)], axis=1)
    scal_ref[...] = scal


def sample_pre(P, pool_st, dnc_st, scc_st, pool_w, pool_scale, conv_w, a_log, dt_bias, b_i, b_f, sc_w,
               start_pos):
    nb = P.shape[0]
    lane_row = lambda v, lane: jnp.zeros((1, 128), F32).at[0, lane:lane + N_HEADS].set(v)
    W3 = 3 * GROUP_W
    out_shape = [jax.ShapeDtypeStruct((nb, GROUP_W), BF16),
                 jax.ShapeDtypeStruct((nb, GROUP_W), BF16),
                 jax.ShapeDtypeStruct(pool_st.shape, F32),
                 jax.ShapeDtypeStruct(dnc_st.shape, F32),
                 jax.ShapeDtypeStruct(scc_st.shape, F32),
                 jax.ShapeDtypeStruct((nb, W3), F32),
                 jax.ShapeDtypeStruct((nb, W3), F32),
                 jax.ShapeDtypeStruct((nb, 128), F32),
                 jax.ShapeDtypeStruct((4 * N_HEADS * HEAD_DIM, nb), BF16)]
    return pl.pallas_call(
        functools.partial(_sample_pre_kernel, start_pos=start_pos),
        out_shape=out_shape,
        compiler_params=pltpu.CompilerParams(vmem_limit_bytes=VMEM_LIMIT),
    )(P, pool_st, dnc_st, scc_st, pool_w, pool_scale.reshape(1, GROUP_W), conv_w,
      lane_row(a_log, LANE_A), lane_row(dt_bias, LANE_A), lane_row(b_i, LANE_I), lane_row(b_f, LANE_F), sc_w)


def _sample_rec_kernel(cols_ref, rows_b_ref, rows_c_ref, scal_ref, z_ref, og_ref, n_ref, m_ref,
                       dng_ref, mlg_ref, s_ref, c_ref, s_acc_ref, c_acc_ref,
                       yb_ref, yc_ref, s_out_ref, c_out_ref, n_out_ref, m_out_ref,
                       ob_ref, hc_ref, *, tb):
    del s_acc_ref, c_acc_ref
    i = pl.program_id(0)
    nb = cols_ref.shape[1]
    row_id = lax.broadcasted_iota(jnp.int32, (nb, 128), 0)
    lane_id = lax.broadcasted_iota(jnp.int32, (1, 128), 1)

    def body(j, carry):
        b = i * tb + j
        onehot = (row_id == b).astype(BF16)
        cols = jnp.dot(cols_ref[...], onehot, preferred_element_type=F32)
        scal = scal_ref[pl.ds(b, 1), :]
        sc = lambda idx: _col(scal, idx)
        m_row = m_ref[pl.ds(b, 1), :]
        rb = rows_b_ref[pl.ds(b, 1), :]
        rc = rows_c_ref[pl.ds(b, 1), :]
        n_all = n_ref[pl.ds(b, 1), :]
        m_new_row = jnp.zeros((1, 128), F32)
        o_rows, h_rows, n_rows = [], [], []
        for h in range(N_HEADS):
            sl = slice(h * HEAD_DIM, (h + 1) * HEAD_DIM)
            colblk = lambda v: cols[(v * N_HEADS + h) * HEAD_DIM:(v * N_HEADS + h + 1) * HEAD_DIM, :]
            s = s_ref[j, h]
            ks = jnp.sum(colblk(1) * s, axis=0, keepdims=True)
            qs = jnp.sum(colblk(0) * s, axis=0, keepdims=True)
            beta, eg, qk = sc(SC_BETA + h), sc(SC_EG + h), sc(SC_QKB + h)
            v_row = rb[:, 2 * GROUP_W + h * HEAD_DIM:2 * GROUP_W + (h + 1) * HEAD_DIM]
            v_new = beta * v_row - (beta * eg) * ks
            o_rows.append(eg * qs + qk * v_new)
            s_out_ref[j, h] = s * eg + colblk(1) * v_new
            cm = c_ref[j, h]
            qc = jnp.sum(colblk(2) * cm, axis=0, keepdims=True)
            q_row = rc[:, sl]
            k_row = rc[:, GROUP_W + h * HEAD_DIM:GROUP_W + (h + 1) * HEAD_DIM]
            vc_row = rc[:, 2 * GROUP_W + h * HEAD_DIM:2 * GROUP_W + (h + 1) * HEAD_DIM]
            n_row = n_all[:, sl]
            li, lf, qkc = sc(SC_LI + h), sc(SC_LF + h), sc(SC_QKC + h)
            m_old = _col(m_row, h)
            bb = lf + m_old
            m_t = jnp.maximum(bb, li)
            w_intra = jnp.exp(li - m_t) * qkc
            w_inter = jnp.exp(bb - m_t)
            num = w_inter * qc + w_intra * vc_row
            den = w_inter * jnp.sum(q_row * n_row, axis=-1, keepdims=True) + w_intra
            h_rows.append(num / jnp.maximum(jnp.abs(den), jnp.exp(-m_t)))
            m_new = jnp.maximum(m_old + lf, li)
            dec = jnp.exp(m_old + lf - m_new)
            e = jnp.exp(li - m_new)
            c_out_ref[j, h] = cm * dec + colblk(3) * (e * vc_row)
            n_rows.append(n_row * dec + e * k_row)
            m_new_row = jnp.where(lane_id == h, m_new, m_new_row)
        ob_ref[pl.ds(b, 1), :] = jnp.concatenate(o_rows, axis=1)
        hc_ref[pl.ds(b, 1), :] = jnp.concatenate(h_rows, axis=1)
        n_out_ref[pl.ds(b, 1), :] = jnp.concatenate(n_rows, axis=1)
        m_out_ref[pl.ds(b, 1), :] = m_new_row
        return carry

    lax.fori_loop(0, tb, body, 0)

    @pl.when(i == pl.num_programs(0) - 1)
    def _():
        ys_b, ys_c = [], []
        for h in range(N_HEADS):
            sl = slice(h * HEAD_DIM, (h + 1) * HEAD_DIM)
            o = ob_ref[:, sl]
            o = o * lax.rsqrt(jnp.mean(o * o, axis=-1, keepdims=True) + EPS) * dng_ref[...]
            ys_b.append(o * _silu(z_ref[:, sl]))
            hh = _sigmoid(og_ref[:, sl]) * hc_ref[:, sl]
            hh = hh * lax.rsqrt(jnp.mean(hh * hh, axis=-1, keepdims=True) + EPS) * mlg_ref[h:h + 1, :]
            ys_c.append(hh)
        yb_ref[...] = jnp.concatenate(ys_b, axis=1).astype(yb_ref.dtype)
        yc_ref[...] = jnp.concatenate(ys_c, axis=1).astype(yc_ref.dtype)


def sample_rec(cols, rows_b, rows_c, scal, P, n_st, m_st, dn_norm_g, ml_norm_g, s_all, c_all, s_acc, c_acc,
               layer, *, tb=8):
    nb = P.shape[0]
    W3 = 3 * GROUP_W
    full = lambda shape: pl.BlockSpec(shape, lambda i: (0,) * len(shape))
    state = pl.BlockSpec((None, tb, N_HEADS, HEAD_DIM, HEAD_DIM), lambda i: (layer, i, 0, 0, 0))
    untouched = pl.BlockSpec(memory_space=pl.ANY)
    return pl.pallas_call(
        functools.partial(_sample_rec_kernel, tb=tb),
        grid=(nb // tb,),
        in_specs=[full(cols.shape), full((nb, W3)), full((nb, W3)), full((nb, 128)),
                  pl.BlockSpec((nb, GROUP_W), lambda i: (0, COL_BZ)),
                  pl.BlockSpec((nb, GROUP_W), lambda i: (0, COL_CO)),
                  full((nb, GROUP_W)), full((nb, 128)), full((1, HEAD_DIM)), full((N_HEADS, HEAD_DIM)),
                  state, state, untouched, untouched],
        out_specs=[full((nb, GROUP_W)), full((nb, GROUP_W)), state, state,
                   full((nb, GROUP_W)), full((nb, 128))],
        out_shape=[jax.ShapeDtypeStruct((nb, GROUP_W), BF16),
                   jax.ShapeDtypeStruct((nb, GROUP_W), BF16),
                   jax.ShapeDtypeStruct(s_all.shape, F32),
                   jax.ShapeDtypeStruct(c_all.shape, F32),
                   jax.ShapeDtypeStruct((nb, GROUP_W), F32),
                   jax.ShapeDtypeStruct((nb, 128), F32)],
        input_output_aliases={12: 2, 13: 3},
        scratch_shapes=[pltpu.VMEM((nb, GROUP_W), F32), pltpu.VMEM((nb, GROUP_W), F32)],
        compiler_params=_cparams(("arbitrary",), VMEM_LIMIT),
    )(cols, rows_b, rows_c, scal, P, P, n_st, m_st, dn_norm_g.reshape(1, HEAD_DIM), ml_norm_g, s_all, c_all,
      s_acc, c_acc)


def _sample_xattn_kernel(q_ref, k_ref, v_ref, o_ref, *, tb):
    i = pl.program_id(0)
    n_rows = k_ref.shape[1]
    row = lax.broadcasted_iota(jnp.int32, (2 * N_HEADS, n_rows), 0)
    lane = lax.broadcasted_iota(jnp.int32, (2 * N_HEADS, n_rows), 1)
    valid = (lane % N_HEADS) == (row % N_HEADS)

    def body(j, carry):
        b = i * tb + j
        q_row = q_ref[pl.ds(b, 1), :]
        heads = [q_row[:, h * HEAD_DIM:(h + 1) * HEAD_DIM] for h in range(N_HEADS)]
        q8 = jnp.concatenate(heads + heads, axis=0)
        s = _bdot_nt(q8, k_ref[j]) * HEAD_DIM ** -0.5
        s = jnp.where(valid, s, -jnp.inf)
        p = jnp.exp(s - jnp.max(s, axis=-1, keepdims=True))
        p = p / jnp.sum(p, axis=-1, keepdims=True)
        o8 = _bdot(p, v_ref[j])
        o_ref[pl.ds(b, 1), :] = jnp.concatenate([o8[h:h + 1, :] for h in range(N_HEADS)], axis=1)
        return carry

    lax.fori_loop(0, tb, body, 0)


def sample_xattn(q, k_all, v_all, layer, *, tb=8):
    nb = q.shape[0]
    n_rows = k_all.shape[2]
    kv = pl.BlockSpec((None, tb, n_rows, HEAD_DIM), lambda i: (layer, i, 0, 0))
    return pl.pallas_call(
        functools.partial(_sample_xattn_kernel, tb=tb),
        grid=(nb // tb,),
        in_specs=[pl.BlockSpec((nb, GROUP_W), lambda i: (0, 0)), kv, kv],
        out_specs=pl.BlockSpec((nb, GROUP_W), lambda i: (0, 0)),
        out_shape=jax.ShapeDtypeStruct((nb, GROUP_W), F32),
        compiler_params=_cparams(("arbitrary",), VMEM_LIMIT),
    )(q, k_all, v_all)


def _prep_w_in(w_in):
    b0 = 5 * GROUP_W
    c0 = b0 + 2 * N_HEADS
    c1 = c0 + 4 * GROUP_W
    d0 = c1 + 2 * N_HEADS
    d1 = d0 + 3 * GROUP_W
    cut = lambda a, b: w_in[..., a:b].astype(BF16)
    pad = jnp.zeros(w_in.shape[:-1] + (128 - 4 * N_HEADS,), BF16)
    return jnp.concatenate([cut(0, b0), cut(c0, c1), cut(d0, d1), cut(b0, c0), cut(c1, d0), pad], axis=-1)


def _ffn(x, g_pre, g_post, wg, wu, wd, layer, *, tm_up, tm_down):
    act = norm_matmul(x, g_pre, [wg, wu], layer, tm=tm_up, tn=FFN_TN, out_dtype=BF16)
    return matmul_resnorm([act], wd, layer, g_post, x, scale=0.5, tm=tm_down)


def _prompt_layer(x, mem2d, B, T, lw, layer):
    g = lw['norm_g']
    x = _ffn(x, g[0], g[1], lw['ffn1_wg'], lw['ffn1_wu'], lw['ffn1_wd'], layer, tm_up=1024, tm_down=256)
    P = norm_matmul(x, g[2], [lw['w_in']], layer, tm=1024, tn=896, out_dtype=F32)
    ya, yd, pool_tail, sc_tail = pool_sconv_prompt(P, B, T, lw['pool_w'], lw['pool_scale'], lw['sc_conv_w'])
    yb, dn_s, dn_tail = deltanet_prompt(P, B, T, lw['dn_conv_w'], lw['dn_A_log'], lw['dn_dt_bias'],
                                        lw['dn_norm_g'])
    yc, ml_c, ml_nm = mlstm_prompt(P, B, T, lw['ml_b_i'], lw['ml_b_f'], lw['ml_norm_g'])
    x = matmul_resnorm([ya, yb, yc, yd], lw['w_out'], layer, g[3], x, scale=1.0, tm=512)
    kv = norm_matmul(mem2d, g[8], [lw['x_wkv']], layer, tm=1024, tn=512, out_dtype=F32)
    x = xattn_prompt(x, B, T, g[4], lw['x_wq'], kv, lw['x_wo'], layer, g[5])
    x = _ffn(x, g[6], g[7], lw['ffn2_wg'], lw['ffn2_wu'], lw['ffn2_wd'], layer, tm_up=1024, tm_down=256)
    states = (pool_tail[:, 16 - POOL_STATE:], dn_tail[:, 8 - (DN_CONV - 1):], dn_s, ml_c,
              ml_nm[:, :N_HEADS], ml_nm[:, N_HEADS:, 0], sc_tail[:, 8 - (SC_WIDTH - 1):])
    mem_k = kv[:, :GROUP_W].reshape(B, MEM_LEN, N_HEADS, HEAD_DIM)
    mem_v = kv[:, GROUP_W:].reshape(B, MEM_LEN, N_HEADS, HEAD_DIM)
    return x, states, mem_k, mem_v


def _sample_layer(x, st, big, acc, layer, lw, start_pos):
    pool_st, dnc_st, ml_n, ml_m, scc_st = st
    s_all, c_all, k_all, v_all = big
    s_acc, c_acc = acc
    nb = x.shape[0]
    g = lw['norm_g']
    x = _ffn(x, g[0], g[1], lw['ffn1_wg'], lw['ffn1_wu'], lw['ffn1_wd'], layer, tm_up=128, tm_down=128)
    P = norm_matmul(x, g[2], [lw['w_in']], layer, tm=128, tn=896, out_dtype=F32)
    (ya, yd, pool_new, dnc_new, scc_new, rows_b, rows_c, scal, cols) = sample_pre(
        P, pool_st.reshape(nb, -1), dnc_st.reshape(nb, -1), scc_st.reshape(nb, -1),
        lw['pool_w'], lw['pool_scale'], lw['dn_conv_w'], lw['dn_A_log'], lw['dn_dt_bias'],
        lw['ml_b_i'], lw['ml_b_f'], lw['sc_conv_w'], start_pos)
    m_pad = jnp.pad(ml_m, ((0, 0), (0, 128 - N_HEADS)))
    yb, yc, s_acc, c_acc, n_new, m_new = sample_rec(
        cols, rows_b, rows_c, scal, P, ml_n.reshape(nb, GROUP_W), m_pad,
        lw['dn_norm_g'], lw['ml_norm_g'], s_all, c_all, s_acc, c_acc, layer)
    x = matmul_resnorm([ya, yb, yc, yd], lw['w_out'], layer, g[3], x, scale=1.0, tm=128)
    q = norm_matmul(x, g[4], [lw['x_wq']], layer, tm=128, tn=GROUP_W, out_dtype=F32)
    o = sample_xattn(q, k_all, v_all, layer)
    x = matmul_resnorm([o], lw['x_wo'], layer, g[5], x, scale=1.0, tm=128)
    x = _ffn(x, g[6], g[7], lw['ffn2_wg'], lw['ffn2_wu'], lw['ffn2_wd'], layer, tm_up=128, tm_down=128)
    states = (pool_new.reshape(pool_st.shape), dnc_new.reshape(dnc_st.shape),
              n_new.reshape(ml_n.shape), m_new[:, :N_HEADS], scc_new.reshape(scc_st.shape))
    return x, states, (s_acc, c_acc)


def kernel(x_prompt, x_sample, mem_prompt, state_pool, state_dn_conv, state_dn_S, state_ml_C, state_ml_n,
           state_ml_m, state_sc_conv, cache_mem_k, cache_mem_v, norm_g, w_in, w_out, pool_w, pool_scale,
           dn_conv_w, dn_A_log, dn_dt_bias, dn_norm_g, ml_b_i, ml_b_f, ml_norm_g, sc_conv_w,
           x_wq, x_wk, x_wv, x_wo, ffn1_wg, ffn1_wu, ffn1_wd, ffn2_wg, ffn2_wu, ffn2_wd):
    depth = norm_g.shape[0]
    B, T, _ = x_prompt.shape
    nb, t_dec, _ = x_sample.shape
    assert t_dec == 1
    start_pos = PAST_LEN

    stacked = dict(w_in=_prep_w_in(w_in), w_out=w_out.astype(BF16), x_wq=x_wq.astype(BF16),
                   x_wkv=jnp.concatenate([x_wk.astype(BF16), x_wv.astype(BF16)], axis=-1),
                   x_wo=x_wo.astype(BF16),
                   ffn1_wg=ffn1_wg.astype(BF16), ffn1_wu=ffn1_wu.astype(BF16), ffn1_wd=ffn1_wd.astype(BF16),
                   ffn2_wg=ffn2_wg.astype(BF16), ffn2_wu=ffn2_wu.astype(BF16), ffn2_wd=ffn2_wd.astype(BF16))

    def layer_weights(l):
        return dict(stacked, norm_g=norm_g[l], pool_w=pool_w[l],
                    pool_scale=pool_scale[l], dn_conv_w=dn_conv_w[l], dn_A_log=dn_A_log[l],
                    dn_dt_bias=dn_dt_bias[l], dn_norm_g=dn_norm_g[l], ml_b_i=ml_b_i[l], ml_b_f=ml_b_f[l],
                    ml_norm_g=ml_norm_g[l], sc_conv_w=sc_conv_w[l])

    mem2d = mem_prompt.reshape(B * MEM_LEN, D_MODEL)
    h = x_prompt.reshape(B * T, D_MODEL)
    p_states, mem_k_list, mem_v_list = [], [], []
    for l in range(depth):
        h, ns, mk, mv = _prompt_layer(h, mem2d, B, T, layer_weights(l), l)
        p_states.append(ns)
        mem_k_list.append(mk)
        mem_v_list.append(mv)
    y_prompt = h.reshape(B, T, D_MODEL)

    s_inputs = (state_pool, state_dn_conv, state_ml_n, state_ml_m, state_sc_conv)
    big = (state_dn_S, state_ml_C,
           cache_mem_k.reshape(depth, nb, MEM_LEN * N_HEADS, HEAD_DIM),
           cache_mem_v.reshape(depth, nb, MEM_LEN * N_HEADS, HEAD_DIM))
    h = x_sample.reshape(nb, D_MODEL)
    s_states = []
    acc = (lax.empty(state_dn_S.shape, F32), lax.empty(state_ml_C.shape, F32))
    for l in range(depth):
        st = tuple(s[l] for s in s_inputs)
        h, ns, acc = _sample_layer(h, st, big, acc, l, layer_weights(l), start_pos)
        s_states.append(ns)
    y_sample = h.reshape(nb, 1, D_MODEL)
    dn_S_s, ml_C_s = acc

    pool_p, dn_conv_p, dn_S_p, ml_C_p, ml_n_p, ml_m_p, sc_conv_p = [jnp.stack(z) for z in zip(*p_states)]
    pool_s, dn_conv_s, ml_n_s, ml_m_s, sc_conv_s = [jnp.stack(z) for z in zip(*s_states)]
    mem_k_p = jnp.stack(mem_k_list)
    mem_v_p = jnp.stack(mem_v_list)
    return (y_prompt, y_sample, pool_p, pool_s, dn_conv_p, dn_conv_s, dn_S_p, dn_S_s, ml_C_p, ml_C_s,
            ml_n_p, ml_n_s, ml_m_p, ml_m_s, sc_conv_p, sc_conv_s, mem_k_p, mem_v_p)
```

```python
import functools

import numpy as np
import jax
import jax.numpy as jnp
from jax import lax
from jax.experimental import pallas as pl
from jax.experimental.pallas import tpu as pltpu

F32 = jnp.float32
BF16 = jnp.bfloat16

EPS = 1e-6
D_MODEL = 2048
GROUP_W = 512
HEAD_DIM = 128
N_HEADS = GROUP_W // HEAD_DIM
CHUNK = 64
POOL_WINDOWS = (2, 4, 8, 16)
POOL_STATE = 15
DN_CONV = 4
SC_WIDTH = 3
MEM_LEN = 256
PAST_LEN = 16384
FFN_TN = 512
COL_A, COL_BQ, COL_BK, COL_BV, COL_BZ = 0, 1, 2, 3, 4
COL_CQ, COL_CK, COL_CV, COL_CO = 5, 6, 7, 8
COL_DB, COL_DC, COL_DH = 9, 10, 11
GATE_COL0 = 12 * GROUP_W
IN_W_PAD = GATE_COL0 + 128
IN_TN = 1280
LANE_BETA, LANE_A, LANE_I, LANE_F = 0, 4, 8, 12
TB = 256
SUB = 128
VMEM_LIMIT = 56 * 2**20


def _cparams(sem, vmem=None):
    return pltpu.CompilerParams(dimension_semantics=sem, vmem_limit_bytes=vmem)


def _bdot(a, b):
    return jnp.dot(a.astype(BF16), b.astype(BF16), preferred_element_type=F32)


def _bdot_nt(a, b):
    return lax.dot_general(a.astype(BF16), b.astype(BF16), (((1,), (1,)), ((), ())),
                           preferred_element_type=F32)


def _sigmoid(x):
    return 1.0 / (1.0 + jnp.exp(-x))


def _silu(x):
    return x * _sigmoid(x)


def _softplus(x):
    return jnp.maximum(x, 0.0) + jnp.log(1.0 + jnp.exp(-jnp.abs(x)))


def _col(x, idx):
    return x[:, idx:idx + 1]


def _resident(shape):
    nd = len(shape)
    return pl.BlockSpec(shape, lambda *_: (0,) * nd, pipeline_mode=pl.Buffered(1))


def _norm_into(x_ref, g_ref, h_ref):
    tm = x_ref.shape[0]
    rs = min(tm, 128)

    def body(i, carry):
        r = pl.multiple_of(i * rs, rs)
        x = x_ref[pl.ds(r, rs), :]
        ms = jnp.mean(x * x, axis=-1, keepdims=True)
        h_ref[pl.ds(r, rs), :] = (x * lax.rsqrt(ms + EPS) * g_ref[...]).astype(BF16)
        return carry

    lax.fori_loop(0, tm // rs, body, 0)


def _norm_mm_kernel(x_ref, g_ref, w_ref, o_ref, h_ref):
    @pl.when(pl.program_id(1) == 0)
    def _():
        _norm_into(x_ref, g_ref, h_ref)

    o_ref[...] = jnp.dot(h_ref[...], w_ref[...], preferred_element_type=F32).astype(o_ref.dtype)


def _norm_swiglu_kernel(x_ref, g_ref, wg_ref, wu_ref, o_ref, h_ref):
    @pl.when(pl.program_id(1) == 0)
    def _():
        _norm_into(x_ref, g_ref, h_ref)

    h = h_ref[...]
    a = jnp.dot(h, wg_ref[...], preferred_element_type=F32)
    b = jnp.dot(h, wu_ref[...], preferred_element_type=F32)
    o_ref[...] = (_silu(a) * b).astype(o_ref.dtype)


def norm_matmul(x, g, ws, layer, *, tm, tn, out_dtype):
    M, K = x.shape
    n_steps = pl.cdiv(ws[0].shape[2], tn)
    N = n_steps * tn
    tm = min(tm, M)
    kern = _norm_swiglu_kernel if len(ws) == 2 else _norm_mm_kernel
    return pl.pallas_call(
        kern,
        grid=(M // tm, n_steps),
        in_specs=[pl.BlockSpec((tm, K), lambda i, j: (i, 0)),
                  pl.BlockSpec((1, K), lambda i, j: (0, 0))]
        + [pl.BlockSpec((None, K, tn), lambda i, j: (layer, 0, j)) for _ in ws],
        out_specs=pl.BlockSpec((tm, tn), lambda i, j: (i, j)),
        out_shape=jax.ShapeDtypeStruct((M, N), out_dtype),
        scratch_shapes=[pltpu.VMEM((tm, K), BF16)],
        compiler_params=_cparams(("parallel", "arbitrary"), VMEM_LIMIT),
    )(x, g.reshape(1, K), *ws)


def _mm_resnorm_kernel(*refs, n_a, scale):
    a_refs = refs[:n_a]
    w_ref, g_ref, res_ref, o_ref = refs[n_a:]
    parts = [r[...].astype(BF16) for r in a_refs]
    a = parts[0] if n_a == 1 else jnp.concatenate(parts, axis=1)
    y = jnp.dot(a, w_ref[...], preferred_element_type=F32)
    ms = jnp.mean(y * y, axis=-1, keepdims=True)
    o_ref[...] = res_ref[...] + scale * (y * lax.rsqrt(ms + EPS) * g_ref[...])


def matmul_resnorm(a_list, w, layer, g, res, *, scale, tm):
    M, N = res.shape
    tm = min(tm, M)
    K = w.shape[1]
    widths = [K] if len(a_list) == 1 else [a.shape[1] for a in a_list]
    assert sum(widths) == K
    w_spec = pl.BlockSpec((None, K, N), lambda i: (layer, 0, 0), pipeline_mode=pl.Buffered(1))
    return pl.pallas_call(
        functools.partial(_mm_resnorm_kernel, n_a=len(a_list), scale=scale),
        grid=(M // tm,),
        in_specs=[pl.BlockSpec((tm, wd), lambda i: (i, 0)) for wd in widths]
        + [w_spec, _resident((1, N)), pl.BlockSpec((tm, N), lambda i: (i, 0))],
        out_specs=pl.BlockSpec((tm, N), lambda i: (i, 0)),
        out_shape=jax.ShapeDtypeStruct((M, N), F32),
        compiler_params=_cparams(("parallel",), VMEM_LIMIT),
    )(*a_list, w, g.reshape(1, N), res)


def _chunk_cumsum(x):
    n = x.shape[0]
    row = lax.broadcasted_iota(jnp.int32, (n, 1), 0) % CHUNK
    s = 1
    while s < CHUNK:
        x = x + jnp.where(row >= s, pltpu.roll(x, s, 0), 0.0)
        s *= 2
    return x


def _chunk_last(x):
    n = x.shape[0]
    parts = [jnp.broadcast_to(x[c * CHUNK + CHUNK - 1:c * CHUNK + CHUNK, :], (CHUNK, x.shape[1]))
             for c in range(n // CHUNK)]
    return jnp.concatenate(parts, axis=0)


def _pad_rows(x, c, n_chunks):
    z = jnp.zeros_like(x)
    return jnp.concatenate([x if i == c else z for i in range(n_chunks)], axis=0)


def _blk_masks(n):
    r = lax.broadcasted_iota(jnp.int32, (n, n), 0)
    c = lax.broadcasted_iota(jnp.int32, (n, n), 1)
    same = (r // CHUNK) == (c // CHUNK)
    return same & (c <= r), same & (c < r), r == c


def _split(a):
    ah = a.astype(BF16)
    return ah, (a - ah.astype(F32)).astype(BF16)


def _split_dot(a, b):
    d = lambda x, y: jnp.dot(x, y, preferred_element_type=F32)
    return d(a[0], b[0]) + d(a[0], b[1]) + d(a[1], b[0])


def _unit_lower_inverses(a_list, n):
    r = lax.broadcasted_iota(jnp.int32, (n, n), 0)
    c = lax.broadcasted_iota(jnp.int32, (n, n), 1)
    pair = lambda s: ((r // (2 * s)) == (c // (2 * s))) & ((r // s) != (c // s))
    eye = (r == c).astype(F32)
    xs = [eye - jnp.where(pair(1), a, 0.0) for a in a_list]
    s = 2
    while s < CHUNK:
        m = pair(s)
        x_sp = [_split(x) for x in xs]
        ts = [_split_dot(_split(jnp.where(m, a, 0.0)), x) for a, x in zip(a_list, x_sp)]
        xs = [x - _split_dot(xp, _split(t)) for x, xp, t in zip(xs, x_sp, ts)]
        s *= 2
    return xs


def _pool_sconv_kernel(ua_ref, db_ref, dc_ref, dh_ref, pw_ref, ps_ref, sw_ref,
                       ya_ref, yd_ref, ptail_ref, stail_ref, hist_a, hist_d, *, tb):
    i = pl.program_id(1)

    @pl.when(i == 0)
    def _():
        hist_a[...] = jnp.zeros_like(hist_a)
        hist_d[...] = jnp.zeros_like(hist_d)

    u = ua_ref[...]
    ext = jnp.concatenate([hist_a[...], u], axis=0)
    a2 = ext + pltpu.roll(ext, 1, 0)
    a4 = a2[:, 128:] + pltpu.roll(a2[:, 128:], 2, 0)
    a8 = a4[:, 128:] + pltpu.roll(a4[:, 128:], 4, 0)
    a16 = a8[:, 128:] + pltpu.roll(a8[:, 128:], 8, 0)
    sums = (a2[16:, :128], a4[16:, :128], a8[16:, :128], a16[16:, :])
    pos = i * tb + lax.broadcasted_iota(jnp.int32, (tb, 1), 0)
    ys = []
    for g, w in enumerate(POOL_WINDOWS):
        cnt = jnp.minimum(pos + 1, w).astype(F32)
        d = sums[g] / cnt - u[:, g * 128:(g + 1) * 128]
        ys.append(_bdot(d, pw_ref[g]))
    ya_ref[...] = (jnp.concatenate(ys, axis=1) * ps_ref[...]).astype(ya_ref.dtype)
    hist_a[...] = u[tb - 16:, :]

    us = dc_ref[...] * dh_ref[...]
    ext = jnp.concatenate([hist_d[...], us], axis=0)
    sw = sw_ref[...]
    y = sw[0:1] * pltpu.roll(ext, 2, 0) + sw[1:2] * pltpu.roll(ext, 1, 0) + sw[2:3] * ext
    yd_ref[...] = (db_ref[...] * y[8:]).astype(yd_ref.dtype)
    hist_d[...] = us[tb - 8:, :]

    @pl.when(i == pl.num_programs(1) - 1)
    def _():
        ptail_ref[0] = u[tb - 16:, :]
        stail_ref[0] = us[tb - 8:, :]


def pool_sconv_prompt(P, B, T, pool_w, pool_scale, sc_w):
    nT = T // TB
    M = B * T
    col = lambda c: pl.BlockSpec((TB, GROUP_W), lambda b, i: (b * nT + i, c))
    return pl.pallas_call(
        functools.partial(_pool_sconv_kernel, tb=TB),
        grid=(B, nT),
        in_specs=[col(COL_A), col(COL_DB), col(COL_DC), col(COL_DH),
                  pl.BlockSpec((4, 128, 128), lambda b, i: (0, 0, 0)),
                  pl.BlockSpec((1, GROUP_W), lambda b, i: (0, 0)),
                  pl.BlockSpec((SC_WIDTH, GROUP_W), lambda b, i: (0, 0))],
        out_specs=[pl.BlockSpec((TB, GROUP_W), lambda b, i: (b * nT + i, 0)),
                   pl.BlockSpec((TB, GROUP_W), lambda b, i: (b * nT + i, 0)),
                   pl.BlockSpec((1, 16, GROUP_W), lambda b, i: (b, 0, 0)),
                   pl.BlockSpec((1, 8, GROUP_W), lambda b, i: (b, 0, 0))],
        out_shape=[jax.ShapeDtypeStruct((M, GROUP_W), BF16),
                   jax.ShapeDtypeStruct((M, GROUP_W), BF16),
                   jax.ShapeDtypeStruct((B, 16, GROUP_W), F32),
                   jax.ShapeDtypeStruct((B, 8, GROUP_W), F32)],
        scratch_shapes=[pltpu.VMEM((16, GROUP_W), F32), pltpu.VMEM((8, GROUP_W), F32)],
        compiler_params=_cparams(("parallel", "arbitrary"), VMEM_LIMIT),
    )(P, P, P, P, pool_w, pool_scale.reshape(1, GROUP_W), sc_w)


def _deltanet_kernel(q_ref, k_ref, v_ref, z_ref, gt_ref, cw_ref, alog_ref, dtb_ref, ng_ref,
                     y_ref, s_out_ref, ctail_ref, s_ref, hist_ref, *, tb):
    i = pl.program_id(1)
    nc = tb // CHUNK

    @pl.when(i == 0)
    def _():
        s_ref[...] = jnp.zeros_like(s_ref)
        hist_ref[...] = jnp.zeros_like(hist_ref)

    qkv = jnp.concatenate([q_ref[...], k_ref[...], v_ref[...]], axis=1)
    ext = jnp.concatenate([hist_ref[...], qkv], axis=0)
    cw = cw_ref[...]
    conv = (cw[0:1] * pltpu.roll(ext, 3, 0) + cw[1:2] * pltpu.roll(ext, 2, 0)
            + cw[2:3] * pltpu.roll(ext, 1, 0) + cw[3:4] * ext)[8:]
    act = _silu(conv)
    hist_ref[...] = qkv[tb - 8:, :]

    gates = gt_ref[...]
    beta_all = _sigmoid(gates)
    g_all = -jnp.exp(alog_ref[...]) * _softplus(gates + dtb_ref[...])
    gcum_all = _chunk_cumsum(g_all)
    glast_all = _chunk_last(gcum_all)
    gcum_t = gcum_all.T
    causal, strict, _ = _blk_masks(SUB)

    heads = range(N_HEADS)
    subs = range(tb // SUB)
    cps = SUB // CHUNK
    hd = lambda base, h: slice(base + h * HEAD_DIM, base + (h + 1) * HEAD_DIM)
    qs, ks, gcs, gls, betas = [], [], [], [], []
    for h in heads:
        qh = act[:, hd(0, h)]
        kh = act[:, hd(GROUP_W, h)]
        qs.append(qh * lax.rsqrt(jnp.sum(qh * qh, axis=-1, keepdims=True) + EPS) * HEAD_DIM ** -0.5)
        ks.append(kh * lax.rsqrt(jnp.sum(kh * kh, axis=-1, keepdims=True) + EPS))
        betas.append(_col(beta_all, LANE_BETA + h))
        gcs.append(_col(gcum_all, LANE_A + h))
        gls.append(_col(glast_all, LANE_A + h))
    k_ts, a_mats, qks = {}, [], {}
    for st in subs:
        R = slice(st * SUB, (st + 1) * SUB)
        for h in heads:
            gr = gcum_t[LANE_A + h:LANE_A + h + 1, R]
            decay = jnp.where(causal, jnp.exp(jnp.where(causal, gcs[h][R] - gr, 0.0)), 0.0)
            k_t = ks[h][R].T.astype(BF16)
            a_mats.append(jnp.where(strict, _bdot(ks[h][R] * betas[h][R], k_t) * decay, 0.0))
            qks[st, h] = jnp.where(causal, _bdot(qs[h][R], k_t) * decay, 0.0).astype(BF16)
            k_ts[st, h] = k_t
    tinvs = _unit_lower_inverses(a_mats, SUB)
    us, ws = {}, {}
    for st in subs:
        R = slice(st * SUB, (st + 1) * SUB)
        for h in heads:
            vb = act[R, hd(2 * GROUP_W, h)] * betas[h][R]
            kbe = ks[h][R] * (betas[h][R] * jnp.exp(gcs[h][R]))
            uw = _bdot(tinvs[st * N_HEADS + h], jnp.concatenate([vb, kbe], axis=1))
            us[st, h] = uw[:, :HEAD_DIM]
            ws[st, h] = uw[:, HEAD_DIM:].astype(BF16)
    q_decs = [(qs[h] * jnp.exp(gcs[h])).astype(BF16) for h in heads]
    tails = [jnp.exp(gls[h] - gcs[h]) for h in heads]
    s_hs = [s_ref[h] for h in heads]
    outs = [[] for _ in heads]
    for c in range(nc):
        r = slice(c * CHUNK, (c + 1) * CHUNK)
        st, cl = divmod(c, cps)
        rl = slice(cl * CHUNK, (cl + 1) * CHUNK)
        for h in heads:
            s_b = s_hs[h].astype(BF16)
            v_new = us[st, h][rl] - jnp.dot(ws[st, h][rl], s_b, preferred_element_type=F32)
            outs[h].append(jnp.dot(q_decs[h][r], s_b, preferred_element_type=F32)
                           + _bdot(qks[st, h][rl], _pad_rows(v_new, cl, cps)))
            s_dec = jnp.exp(jnp.broadcast_to(gls[h][c * CHUNK:c * CHUNK + 1], (HEAD_DIM, 1)))
            s_hs[h] = s_hs[h] * s_dec + _bdot(k_ts[st, h], _pad_rows(v_new * tails[h][r], cl, cps))
    ys = []
    for h in heads:
        s_ref[h] = s_hs[h]
        o = jnp.concatenate(outs[h], axis=0)
        o = o * lax.rsqrt(jnp.mean(o * o, axis=-1, keepdims=True) + EPS) * ng_ref[...]
        ys.append(o * _silu(z_ref[:, hd(0, h)]))
    y_ref[...] = jnp.concatenate(ys, axis=1).astype(y_ref.dtype)

    @pl.when(i == pl.num_programs(1) - 1)
    def _():
        s_out_ref[0] = s_ref[...]
        ctail_ref[0] = qkv[tb - 8:, :]


def deltanet_prompt(P, B, T, conv_w, a_log, dt_bias, norm_g):
    nT = T // TB
    M = B * T
    col = lambda c: pl.BlockSpec((TB, GROUP_W), lambda b, i: (b * nT + i, c))
    lane_row = lambda v, lane: jnp.zeros((1, 128), F32).at[0, lane:lane + N_HEADS].set(v)
    const = lambda shape: pl.BlockSpec(shape, lambda b, i: (0,) * len(shape))
    return pl.pallas_call(
        functools.partial(_deltanet_kernel, tb=TB),
        grid=(B, nT),
        in_specs=[col(COL_BQ), col(COL_BK), col(COL_BV), col(COL_BZ),
                  pl.BlockSpec((TB, 128), lambda b, i: (b * nT + i, GATE_COL0 // 128)),
                  const((DN_CONV, 3 * GROUP_W)), const((1, 128)), const((1, 128)), const((1, HEAD_DIM))],
        out_specs=[pl.BlockSpec((TB, GROUP_W), lambda b, i: (b * nT + i, 0)),
                   pl.BlockSpec((1, N_HEADS, HEAD_DIM, HEAD_DIM), lambda b, i: (b, 0, 0, 0)),
                   pl.BlockSpec((1, 8, 3 * GROUP_W), lambda b, i: (b, 0, 0))],
        out_shape=[jax.ShapeDtypeStruct((M, GROUP_W), BF16),
                   jax.ShapeDtypeStruct((B, N_HEADS, HEAD_DIM, HEAD_DIM), F32),
                   jax.ShapeDtypeStruct((B, 8, 3 * GROUP_W), F32)],
        scratch_shapes=[pltpu.VMEM((N_HEADS, HEAD_DIM, HEAD_DIM), F32),
                        pltpu.VMEM((8, 3 * GROUP_W), F32)],
        compiler_params=_cparams(("parallel", "arbitrary"), VMEM_LIMIT),
    )(P, P, P, P, P, conv_w, lane_row(a_log, LANE_A), lane_row(dt_bias, LANE_A),
      norm_g.reshape(1, HEAD_DIM))


def _mlstm_kernel(q_ref, k_ref, v_ref, og_ref, gt_ref, bi_ref, bf_ref, ng_ref,
                  y_ref, c_out_ref, nm_out_ref, c_ref, nm_ref, *, tb):
    i = pl.program_id(1)
    nc = tb // CHUNK

    @pl.when(i == 0)
    def _():
        c_ref[...] = jnp.zeros_like(c_ref)
        nm_ref[...] = jnp.zeros_like(nm_ref)

    gates = gt_ref[...]
    li_all = gates + bi_ref[...]
    lf_all = -_softplus(-(gates + bf_ref[...]))
    f_all = _chunk_cumsum(lf_all)
    flast_all = _chunk_last(f_all)
    f_t = f_all.T
    li_t = li_all.T
    causal, _, _ = _blk_masks(tb)

    heads = range(N_HEADS)
    hd = lambda h: slice(h * HEAD_DIM, (h + 1) * HEAD_DIM)
    qs, kss, k_ts, evs, eks, intras, den_intras, w_inters, inv_floor, decs = ([] for _ in range(10))
    for h in heads:
        qh = q_ref[:, hd(h)]
        kh = k_ref[:, hd(h)] * HEAD_DIM ** -0.5
        vh = v_ref[:, hd(h)]
        fc = _col(f_all, LANE_F + h)
        fl = _col(flast_all, LANE_F + h)
        a_w = fl - fc + _col(li_all, LANE_I + h)
        fr = f_t[LANE_F + h:LANE_F + h + 1, :]
        lir = li_t[LANE_I + h:LANE_I + h + 1, :]
        dm = jnp.where(causal, fc - fr + lir, -jnp.inf)
        mx = jnp.max(dm, axis=1, keepdims=True)
        m_h = nm_ref[N_HEADS + h:N_HEADS + h + 1, 0:1]
        m_prev, m_next, dec_h = [], [], []
        for c in range(nc):
            r = slice(c * CHUNK, (c + 1) * CHUNK)
            fl_c = fl[c * CHUNK:c * CHUNK + 1]
            m_new = jnp.maximum(m_h + fl_c, jnp.max(a_w[r], axis=0, keepdims=True))
            dec_h.append(jnp.exp(m_h + fl_c - m_new))
            m_prev.append(jnp.broadcast_to(m_h, (CHUNK, 1)))
            m_next.append(jnp.broadcast_to(m_new, (CHUNK, 1)))
            m_h = m_new
        nm_ref[N_HEADS + h:N_HEADS + h + 1, :] = jnp.broadcast_to(m_h, (1, HEAD_DIM))
        b = fc + jnp.concatenate(m_prev, axis=0)
        m_t = jnp.maximum(b, mx)
        k_t = kh.T.astype(BF16)
        w_intra = jnp.exp(dm - m_t) * _bdot(qh, k_t)
        e = jnp.exp(a_w - jnp.concatenate(m_next, axis=0))
        qs.append(qh)
        k_ts.append(k_t)
        evs.append(e * vh)
        eks.append(e * kh)
        intras.append(_bdot(w_intra, vh))
        den_intras.append(jnp.sum(w_intra, axis=-1, keepdims=True))
        w_inters.append(jnp.exp(b - m_t))
        inv_floor.append(jnp.exp(-m_t))
        decs.append(dec_h)
    c_hs = [c_ref[h] for h in heads]
    n_hs = [nm_ref[h:h + 1, :] for h in heads]
    outs = [[] for _ in heads]
    for c in range(nc):
        r = slice(c * CHUNK, (c + 1) * CHUNK)
        for h in heads:
            q_c = qs[h][r]
            num = w_inters[h][r] * _bdot(q_c, c_hs[h]) + intras[h][r]
            den = w_inters[h][r] * jnp.sum(q_c * n_hs[h], axis=-1, keepdims=True) + den_intras[h][r]
            outs[h].append(num / jnp.maximum(jnp.abs(den), inv_floor[h][r]))
            c_hs[h] = c_hs[h] * decs[h][c] + _bdot(k_ts[h], _pad_rows(evs[h][r], c, nc))
            n_hs[h] = n_hs[h] * decs[h][c] + jnp.sum(eks[h][r], axis=0, keepdims=True)
    ys = []
    for h in heads:
        c_ref[h] = c_hs[h]
        nm_ref[h:h + 1, :] = n_hs[h]
        hh = jnp.concatenate(outs[h], axis=0)
        hh = _sigmoid(og_ref[:, hd(h)]) * hh
        hh = hh * lax.rsqrt(jnp.mean(hh * hh, axis=-1, keepdims=True) + EPS) * ng_ref[h:h + 1, :]
        ys.append(hh)
    y_ref[...] = jnp.concatenate(ys, axis=1).astype(y_ref.dtype)

    @pl.when(i == pl.num_programs(1) - 1)
    def _():
        c_out_ref[0] = c_ref[...]
        nm_out_ref[0] = nm_ref[...]


def mlstm_prompt(P, B, T, b_i, b_f, norm_g):
    nT = T // TB
    M = B * T
    col = lambda c: pl.BlockSpec((TB, GROUP_W), lambda b, i: (b * nT + i, c))
    lane_row = lambda v, lane: jnp.zeros((1, 128), F32).at[0, lane:lane + N_HEADS].set(v)
    const = lambda shape: pl.BlockSpec(shape, lambda b, i: (0,) * len(shape))
    return pl.pallas_call(
        functools.partial(_mlstm_kernel, tb=TB),
        grid=(B, nT),
        in_specs=[col(COL_CQ), col(COL_CK), col(COL_CV), col(COL_CO),
                  pl.BlockSpec((TB, 128), lambda b, i: (b * nT + i, GATE_COL0 // 128)),
                  const((1, 128)), const((1, 128)), const((N_HEADS, HEAD_DIM))],
        out_specs=[pl.BlockSpec((TB, GROUP_W), lambda b, i: (b * nT + i, 0)),
                   pl.BlockSpec((1, N_HEADS, HEAD_DIM, HEAD_DIM), lambda b, i: (b, 0, 0, 0)),
                   pl.BlockSpec((1, 2 * N_HEADS, HEAD_DIM), lambda b, i: (b, 0, 0))],
        out_shape=[jax.ShapeDtypeStruct((M, GROUP_W), BF16),
                   jax.ShapeDtypeStruct((B, N_HEADS, HEAD_DIM, HEAD_DIM), F32),
                   jax.ShapeDtypeStruct((B, 2 * N_HEADS, HEAD_DIM), F32)],
        scratch_shapes=[pltpu.VMEM((N_HEADS, HEAD_DIM, HEAD_DIM), F32),
                        pltpu.VMEM((2 * N_HEADS, HEAD_DIM), F32)],
        compiler_params=_cparams(("parallel", "arbitrary"), VMEM_LIMIT),
    )(P, P, P, P, P, lane_row(b_i, LANE_I), lane_row(b_f, LANE_F), norm_g)


def _xattn_prompt_kernel(x_ref, gpre_ref, wq_ref, kv_ref, wo_ref, gpost_ref, o_ref, h_ref):
    _norm_into(x_ref, gpre_ref, h_ref)
    q = jnp.dot(h_ref[...], wq_ref[...], preferred_element_type=F32)
    outs = []
    for h in range(N_HEADS):
        sl = slice(h * HEAD_DIM, (h + 1) * HEAD_DIM)
        k_h = kv_ref[:, sl]
        v_h = kv_ref[:, GROUP_W + h * HEAD_DIM:GROUP_W + (h + 1) * HEAD_DIM]
        s = _bdot_nt(q[:, sl], k_h) * HEAD_DIM ** -0.5
        p = jnp.exp(s - jnp.max(s, axis=-1, keepdims=True))
        p = p / jnp.sum(p, axis=-1, keepdims=True)
        outs.append(_bdot(p, v_h))
    o = jnp.concatenate(outs, axis=1).astype(BF16)
    y = jnp.dot(o, wo_ref[...], preferred_element_type=F32)
    ms = jnp.mean(y * y, axis=-1, keepdims=True)
    o_ref[...] = x_ref[...] + y * lax.rsqrt(ms + EPS) * gpost_ref[...]


def xattn_prompt(x, B, T, g_pre, wq, kv, wo, layer, g_post, *, tq=512):
    nq = T // tq
    M = B * T
    return pl.pallas_call(
        _xattn_prompt_kernel,
        grid=(B, nq),
        in_specs=[pl.BlockSpec((tq, D_MODEL), lambda b, i: (b * nq + i, 0)),
                  pl.BlockSpec((1, D_MODEL), lambda b, i: (0, 0)),
                  pl.BlockSpec((None, D_MODEL, GROUP_W), lambda b, i: (layer, 0, 0)),
                  pl.BlockSpec((MEM_LEN, 2 * GROUP_W), lambda b, i: (b, 0)),
                  pl.BlockSpec((None, GROUP_W, D_MODEL), lambda b, i: (layer, 0, 0)),
                  pl.BlockSpec((1, D_MODEL), lambda b, i: (0, 0))],
        out_specs=pl.BlockSpec((tq, D_MODEL), lambda b, i: (b * nq + i, 0)),
        out_shape=jax.ShapeDtypeStruct((M, D_MODEL), F32),
        scratch_shapes=[pltpu.VMEM((tq, D_MODEL), BF16)],
        compiler_params=_cparams(("parallel", "parallel"), VMEM_LIMIT),
    )(x, g_pre.reshape(1, D_MODEL), wq, kv, wo, g_post.reshape(1, D_MODEL))


SC_BETA, SC_EG, SC_QKB, SC_LI, SC_LF, SC_QKC = 0, 4, 8, 12, 16, 20


def _sample_pre_kernel(p_ref, pool_ref, dnc_ref, scc_ref, pw_ref, ps_ref, cw_ref, alog_ref, dtb_ref,
                       bi_ref, bf_ref, sw_ref,
                       ya_ref, yd_ref, pool_out_ref, dnc_out_ref, scc_out_ref,
                       rows_b_ref, rows_c_ref, scal_ref, cols_ref, *, start_pos):
    nb = p_ref.shape[0]
    blk = lambda c: p_ref[:, c * GROUP_W:(c + 1) * GROUP_W]

    u = blk(COL_A)
    ys = []
    for g, w in enumerate(POOL_WINDOWS):
        lanes = slice(g * 128, (g + 1) * 128)
        tot = u[:, lanes]
        for r in range(POOL_STATE + 1 - w, POOL_STATE):
            tot = tot + pool_ref[:, r * GROUP_W + g * 128:r * GROUP_W + (g + 1) * 128]
        d = tot / float(min(start_pos + 1, w)) - u[:, lanes]
        ys.append(_bdot(d, pw_ref[g]))
    ya_ref[...] = (jnp.concatenate(ys, axis=1) * ps_ref[...]).astype(ya_ref.dtype)
    pool_out_ref[:, :(POOL_STATE - 1) * GROUP_W] = pool_ref[:, GROUP_W:]
    pool_out_ref[:, (POOL_STATE - 1) * GROUP_W:] = u

    us = blk(COL_DC) * blk(COL_DH)
    sw = sw_ref[...]
    y = sw[0:1] * scc_ref[:, :GROUP_W] + sw[1:2] * scc_ref[:, GROUP_W:] + sw[2:3] * us
    yd_ref[...] = (blk(COL_DB) * y).astype(yd_ref.dtype)
    scc_out_ref[:, :GROUP_W] = scc_ref[:, GROUP_W:]
    scc_out_ref[:, GROUP_W:] = us

    qkv = p_ref[:, COL_BQ * GROUP_W:(COL_BV + 1) * GROUP_W]
    cw = cw_ref[...]
    W3 = 3 * GROUP_W
    conv = cw[DN_CONV - 1:DN_CONV] * qkv
    for j in range(DN_CONV - 1):
        conv = conv + cw[j:j + 1] * dnc_ref[:, j * W3:(j + 1) * W3]
    act = _silu(conv)
    dnc_out_ref[:, :(DN_CONV - 2) * W3] = dnc_ref[:, W3:]
    dnc_out_ref[:, (DN_CONV - 2) * W3:] = qkv

    gates = p_ref[:, GATE_COL0:GATE_COL0 + 128]
    beta_all = _sigmoid(gates)
    eg_all = jnp.exp(-jnp.exp(alog_ref[...]) * _softplus(gates + dtb_ref[...]))
    li_all = gates + bi_ref[...]
    lf_all = -_softplus(-(gates + bf_ref[...]))

    lane = lax.broadcasted_iota(jnp.int32, (nb, 128), 1)
    scal = jnp.zeros((nb, 128), F32)

    def put(tab, lane_idx, colv):
        return jnp.where(lane == lane_idx, colv, tab)

    qs, ks = [], []
    for h in range(N_HEADS):
        sl = slice(h * HEAD_DIM, (h + 1) * HEAD_DIM)
        qh = act[:, sl]
        kh = act[:, GROUP_W + h * HEAD_DIM:GROUP_W + (h + 1) * HEAD_DIM]
        qh = qh * lax.rsqrt(jnp.sum(qh * qh, axis=-1, keepdims=True) + EPS) * HEAD_DIM ** -0.5
        kh = kh * lax.rsqrt(jnp.sum(kh * kh, axis=-1, keepdims=True) + EPS)
        qs.append(qh)
        ks.append(kh)
        scal = put(scal, SC_BETA + h, _col(beta_all, LANE_BETA + h))
        scal = put(scal, SC_EG + h, _col(eg_all, LANE_A + h))
        scal = put(scal, SC_QKB + h, jnp.sum(qh * kh, axis=-1, keepdims=True))
        cols_ref[(0 * N_HEADS + h) * HEAD_DIM:(0 * N_HEADS + h + 1) * HEAD_DIM, :] = qh.T.astype(BF16)
        cols_ref[(1 * N_HEADS + h) * HEAD_DIM:(1 * N_HEADS + h + 1) * HEAD_DIM, :] = kh.T.astype(BF16)
    rows_b_ref[...] = jnp.concatenate(qs + ks + [act[:, 2 * GROUP_W:]], axis=1)

    qc = blk(COL_CQ)
    kc = blk(COL_CK) * HEAD_DIM ** -0.5
    for h in range(N_HEADS):
        sl = slice(h * HEAD_DIM, (h + 1) * HEAD_DIM)
        scal = put(scal, SC_LI + h, _col(li_all, LANE_I + h))
        scal = put(scal, SC_LF + h, _col(lf_all, LANE_F + h))
        scal = put(scal, SC_QKC + h, jnp.sum(qc[:, sl] * kc[:, sl], axis=-1, keepdims=True))
        cols_ref[(2 * N_HEADS + h) * HEAD_DIM:(2 * N_HEADS + h + 1) * HEAD_DIM, :] = qc[:, sl].T.astype(BF16)
        cols_ref[(3 * N_HEADS + h) * HEAD_DIM:(3 * N_HEADS + h + 1) * HEAD_DIM, :] = kc[:, sl].T.astype(BF16)
    rows_c_ref[...] = jnp.concatenate([qc, kc, blk(COL_CV)], axis=1)
    scal_ref[...] = scal


def sample_pre(P, pool_st, dnc_st, scc_st, pool_w, pool_scale, conv_w, a_log, dt_bias, b_i, b_f, sc_w,
               start_pos):
    nb = P.shape[0]
    lane_row = lambda v, lane: jnp.zeros((1, 128), F32).at[0, lane:lane + N_HEADS].set(v)
    W3 = 3 * GROUP_W
    out_shape = [jax.ShapeDtypeStruct((nb, GROUP_W), BF16),
                 jax.ShapeDtypeStruct((nb, GROUP_W), BF16),
                 jax.ShapeDtypeStruct(pool_st.shape, F32),
                 jax.ShapeDtypeStruct(dnc_st.shape, F32),
                 jax.ShapeDtypeStruct(scc_st.shape, F32),
                 jax.ShapeDtypeStruct((nb, W3), F32),
                 jax.ShapeDtypeStruct((nb, W3), F32),
                 jax.ShapeDtypeStruct((nb, 128), F32),
                 jax.ShapeDtypeStruct((4 * N_HEADS * HEAD_DIM, nb), BF16)]
    return pl.pallas_call(
        functools.partial(_sample_pre_kernel, start_pos=start_pos),
        out_shape=out_shape,
        compiler_params=pltpu.CompilerParams(vmem_limit_bytes=VMEM_LIMIT),
    )(P, pool_st, dnc_st, scc_st, pool_w, pool_scale.reshape(1, GROUP_W), conv_w,
      lane_row(a_log, LANE_A), lane_row(dt_bias, LANE_A), lane_row(b_i, LANE_I), lane_row(b_f, LANE_F), sc_w)


def _sample_rec_kernel(cols_ref, rows_b_ref, rows_c_ref, scal_ref, z_ref, og_ref, n_ref, m_ref,
                       dng_ref, mlg_ref, s_ref, c_ref, s_acc_ref, c_acc_ref,
                       yb_ref, yc_ref, s_out_ref, c_out_ref, n_out_ref, m_out_ref,
                       ob_ref, hc_ref, *, tb):
    del s_acc_ref, c_acc_ref
    i = pl.program_id(0)
    nb = cols_ref.shape[1]
    row_id = lax.broadcasted_iota(jnp.int32, (nb, 128), 0)
    lane_id = lax.broadcasted_iota(jnp.int32, (1, 128), 1)

    def body(j, carry):
        b = i * tb + j
        onehot = (row_id == b).astype(BF16)
        cols = jnp.dot(cols_ref[...], onehot, preferred_element_type=F32)
        scal = scal_ref[pl.ds(b, 1), :]
        sc = lambda idx: _col(scal, idx)
        m_row = m_ref[pl.ds(b, 1), :]
        rb = rows_b_ref[pl.ds(b, 1), :]
        rc = rows_c_ref[pl.ds(b, 1), :]
        n_all = n_ref[pl.ds(b, 1), :]
        m_new_row = jnp.zeros((1, 128), F32)
        o_rows, h_rows, n_rows = [], [], []
        for h in range(N_HEADS):
            sl = slice(h * HEAD_DIM, (h + 1) * HEAD_DIM)
            colblk = lambda v: cols[(v * N_HEADS + h) * HEAD_DIM:(v * N_HEADS + h + 1) * HEAD_DIM, :]
            s = s_ref[j, h]
            ks = jnp.sum(colblk(1) * s, axis=0, keepdims=True)
            qs = jnp.sum(colblk(0) * s, axis=0, keepdims=True)
            beta, eg, qk = sc(SC_BETA + h), sc(SC_EG + h), sc(SC_QKB + h)
            v_row = rb[:, 2 * GROUP_W + h * HEAD_DIM:2 * GROUP_W + (h + 1) * HEAD_DIM]
            v_new = beta * v_row - (beta * eg) * ks
            o_rows.append(eg * qs + qk * v_new)
            s_out_ref[j, h] = s * eg + colblk(1) * v_new
            cm = c_ref[j, h]
            qc = jnp.sum(colblk(2) * cm, axis=0, keepdims=True)
            q_row = rc[:, sl]
            k_row = rc[:, GROUP_W + h * HEAD_DIM:GROUP_W + (h + 1) * HEAD_DIM]
            vc_row = rc[:, 2 * GROUP_W + h * HEAD_DIM:2 * GROUP_W + (h + 1) * HEAD_DIM]
            n_row = n_all[:, sl]
            li, lf, qkc = sc(SC_LI + h), sc(SC_LF + h), sc(SC_QKC + h)
            m_old = _col(m_row, h)
            bb = lf + m_old
            m_t = jnp.maximum(bb, li)
            w_intra = jnp.exp(li - m_t) * qkc
            w_inter = jnp.exp(bb - m_t)
            num = w_inter * qc + w_intra * vc_row
            den = w_inter * jnp.sum(q_row * n_row, axis=-1, keepdims=True) + w_intra
            h_rows.append(num / jnp.maximum(jnp.abs(den), jnp.exp(-m_t)))
            m_new = jnp.maximum(m_old + lf, li)
            dec = jnp.exp(m_old + lf - m_new)
            e = jnp.exp(li - m_new)
            c_out_ref[j, h] = cm * dec + colblk(3) * (e * vc_row)
            n_rows.append(n_row * dec + e * k_row)
            m_new_row = jnp.where(lane_id == h, m_new, m_new_row)
        ob_ref[pl.ds(b, 1), :] = jnp.concatenate(o_rows, axis=1)
        hc_ref[pl.ds(b, 1), :] = jnp.concatenate(h_rows, axis=1)
        n_out_ref[pl.ds(b, 1), :] = jnp.concatenate(n_rows, axis=1)
        m_out_ref[pl.ds(b, 1), :] = m_new_row
        return carry

    lax.fori_loop(0, tb, body, 0, unroll=2)

    @pl.when(i == pl.num_programs(0) - 1)
    def _():
        ys_b, ys_c = [], []
        for h in range(N_HEADS):
            sl = slice(h * HEAD_DIM, (h + 1) * HEAD_DIM)
            o = ob_ref[:, sl]
            o = o * lax.rsqrt(jnp.mean(o * o, axis=-1, keepdims=True) + EPS) * dng_ref[...]
            ys_b.append(o * _silu(z_ref[:, sl]))
            hh = _sigmoid(og_ref[:, sl]) * hc_ref[:, sl]
            hh = hh * lax.rsqrt(jnp.mean(hh * hh, axis=-1, keepdims=True) + EPS) * mlg_ref[h:h + 1, :]
            ys_c.append(hh)
        yb_ref[...] = jnp.concatenate(ys_b, axis=1).astype(yb_ref.dtype)
        yc_ref[...] = jnp.concatenate(ys_c, axis=1).astype(yc_ref.dtype)


def sample_rec(cols, rows_b, rows_c, scal, P, n_st, m_st, dn_norm_g, ml_norm_g, s_all, c_all, s_acc, c_acc,
               layer, *, tb=8):
    nb = P.shape[0]
    W3 = 3 * GROUP_W
    full = lambda shape: pl.BlockSpec(shape, lambda i: (0,) * len(shape))
    state = pl.BlockSpec((None, tb, N_HEADS, HEAD_DIM, HEAD_DIM), lambda i: (layer, i, 0, 0, 0))
    untouched = pl.BlockSpec(memory_space=pl.ANY)
    return pl.pallas_call(
        functools.partial(_sample_rec_kernel, tb=tb),
        grid=(nb // tb,),
        in_specs=[full(cols.shape), full((nb, W3)), full((nb, W3)), full((nb, 128)),
                  pl.BlockSpec((nb, GROUP_W), lambda i: (0, COL_BZ)),
                  pl.BlockSpec((nb, GROUP_W), lambda i: (0, COL_CO)),
                  full((nb, GROUP_W)), full((nb, 128)), full((1, HEAD_DIM)), full((N_HEADS, HEAD_DIM)),
                  state, state, untouched, untouched],
        out_specs=[full((nb, GROUP_W)), full((nb, GROUP_W)), state, state,
                   full((nb, GROUP_W)), full((nb, 128))],
        out_shape=[jax.ShapeDtypeStruct((nb, GROUP_W), BF16),
                   jax.ShapeDtypeStruct((nb, GROUP_W), BF16),
                   jax.ShapeDtypeStruct(s_all.shape, F32),
                   jax.ShapeDtypeStruct(c_all.shape, F32),
                   jax.ShapeDtypeStruct((nb, GROUP_W), F32),
                   jax.ShapeDtypeStruct((nb, 128), F32)],
        input_output_aliases={12: 2, 13: 3},
        scratch_shapes=[pltpu.VMEM((nb, GROUP_W), F32), pltpu.VMEM((nb, GROUP_W), F32)],
        compiler_params=_cparams(("arbitrary",), VMEM_LIMIT),
    )(cols, rows_b, rows_c, scal, P, P, n_st, m_st, dn_norm_g.reshape(1, HEAD_DIM), ml_norm_g, s_all, c_all,
      s_acc, c_acc)


def _sample_xattn_kernel(q_ref, k_ref, v_ref, o_ref, *, tb):
    i = pl.program_id(0)
    n_rows = k_ref.shape[1]
    n_rep = n_rows // (2 * N_HEADS)
    ones = jnp.ones((HEAD_DIM, HEAD_DIM), BF16)
    fold = lambda x: x + pltpu.roll(x, N_HEADS, 0)

    def body(j, carry):
        b = i * tb + j
        q_row = q_ref[pl.ds(b, 1), :]
        heads = [q_row[:, h * HEAD_DIM:(h + 1) * HEAD_DIM] for h in range(N_HEADS)]
        q8 = jnp.concatenate(heads + heads, axis=0)
        k3 = k_ref[j].reshape(n_rep, 2 * N_HEADS, HEAD_DIM)
        prod = (k3 * q8[None]).reshape(n_rows, HEAD_DIM)
        ph, plo = _split(prod)
        s = (jnp.dot(ph, ones, preferred_element_type=F32)
             + jnp.dot(plo, ones, preferred_element_type=F32)) * HEAD_DIM ** -0.5
        s3 = s.reshape(n_rep, 2 * N_HEADS, HEAD_DIM)
        mx = jnp.max(s3, axis=0)
        mx = jnp.maximum(mx, pltpu.roll(mx, N_HEADS, 0))
        e3 = jnp.exp(s3 - mx[None])
        den = fold(jnp.sum(e3, axis=0))
        v3 = v_ref[j].reshape(n_rep, 2 * N_HEADS, HEAD_DIM)
        o8 = fold(jnp.sum(e3 * v3, axis=0)) / den
        o_ref[pl.ds(b, 1), :] = jnp.concatenate([o8[h:h + 1, :] for h in range(N_HEADS)], axis=1)
        return carry

    lax.fori_loop(0, tb, body, 0, unroll=4)


def sample_xattn(q, k_all, v_all, layer, *, tb=8):
    nb = q.shape[0]
    n_rows = k_all.shape[2]
    kv = pl.BlockSpec((None, tb, n_rows, HEAD_DIM), lambda i: (layer, i, 0, 0))
    return pl.pallas_call(
        functools.partial(_sample_xattn_kernel, tb=tb),
        grid=(nb // tb,),
        in_specs=[pl.BlockSpec((nb, GROUP_W), lambda i: (0, 0)), kv, kv],
        out_specs=pl.BlockSpec((nb, GROUP_W), lambda i: (0, 0)),
        out_shape=jax.ShapeDtypeStruct((nb, GROUP_W), F32),
        compiler_params=_cparams(("arbitrary",), VMEM_LIMIT),
    )(q, k_all, v_all)


def _prep_w_in(w_in):
    b0 = 5 * GROUP_W
    c0 = b0 + 2 * N_HEADS
    c1 = c0 + 4 * GROUP_W
    d0 = c1 + 2 * N_HEADS
    d1 = d0 + 3 * GROUP_W
    cut = lambda a, b: w_in[..., a:b].astype(BF16)
    pad = jnp.zeros(w_in.shape[:-1] + (128 - 4 * N_HEADS,), BF16)
    return jnp.concatenate([cut(0, b0), cut(c0, c1), cut(d0, d1), cut(b0, c0), cut(c1, d0), pad], axis=-1)


def _ffn(x, g_pre, g_post, wg, wu, wd, layer, *, tm_up, tm_down):
    act = norm_matmul(x, g_pre, [wg, wu], layer, tm=tm_up, tn=FFN_TN, out_dtype=BF16)
    return matmul_resnorm([act], wd, layer, g_post, x, scale=0.5, tm=tm_down)


def _prompt_layer(x, mem2d, B, T, lw, layer):
    g = lw['norm_g']
    x = _ffn(x, g[0], g[1], lw['ffn1_wg'], lw['ffn1_wu'], lw['ffn1_wd'], layer, tm_up=1024, tm_down=256)
    P = norm_matmul(x, g[2], [lw['w_in']], layer, tm=1024, tn=IN_TN, out_dtype=F32)
    ya, yd, pool_tail, sc_tail = pool_sconv_prompt(P, B, T, lw['pool_w'], lw['pool_scale'], lw['sc_conv_w'])
    yb, dn_s, dn_tail = deltanet_prompt(P, B, T, lw['dn_conv_w'], lw['dn_A_log'], lw['dn_dt_bias'],
                                        lw['dn_norm_g'])
    yc, ml_c, ml_nm = mlstm_prompt(P, B, T, lw['ml_b_i'], lw['ml_b_f'], lw['ml_norm_g'])
    x = matmul_resnorm([ya, yb, yc, yd], lw['w_out'], layer, g[3], x, scale=1.0, tm=512)
    kv = norm_matmul(mem2d, g[8], [lw['x_wkv']], layer, tm=1024, tn=512, out_dtype=F32)
    x = xattn_prompt(x, B, T, g[4], lw['x_wq'], kv, lw['x_wo'], layer, g[5])
    x = _ffn(x, g[6], g[7], lw['ffn2_wg'], lw['ffn2_wu'], lw['ffn2_wd'], layer, tm_up=1024, tm_down=256)
    states = (pool_tail[:, 16 - POOL_STATE:], dn_tail[:, 8 - (DN_CONV - 1):], dn_s, ml_c,
              ml_nm[:, :N_HEADS], ml_nm[:, N_HEADS:, 0], sc_tail[:, 8 - (SC_WIDTH - 1):])
    mem_k = kv[:, :GROUP_W].reshape(B, MEM_LEN, N_HEADS, HEAD_DIM)
    mem_v = kv[:, GROUP_W:].reshape(B, MEM_LEN, N_HEADS, HEAD_DIM)
    return x, states, mem_k, mem_v


def _sample_layer(x, st, big, acc, layer, lw, start_pos):
    pool_st, dnc_st, ml_n, ml_m, scc_st = st
    s_all, c_all, k_all, v_all = big
    s_acc, c_acc = acc
    nb = x.shape[0]
    g = lw['norm_g']
    x = _ffn(x, g[0], g[1], lw['ffn1_wg'], lw['ffn1_wu'], lw['ffn1_wd'], layer, tm_up=128, tm_down=128)
    P = norm_matmul(x, g[2], [lw['w_in']], layer, tm=128, tn=IN_TN, out_dtype=F32)
    (ya, yd, pool_new, dnc_new, scc_new, rows_b, rows_c, scal, cols) = sample_pre(
        P, pool_st.reshape(nb, -1), dnc_st.reshape(nb, -1), scc_st.reshape(nb, -1),
        lw['pool_w'], lw['pool_scale'], lw['dn_conv_w'], lw['dn_A_log'], lw['dn_dt_bias'],
        lw['ml_b_i'], lw['ml_b_f'], lw['sc_conv_w'], start_pos)
    m_pad = jnp.pad(ml_m, ((0, 0), (0, 128 - N_HEADS)))
    yb, yc, s_acc, c_acc, n_new, m_new = sample_rec(
        cols, rows_b, rows_c, scal, P, ml_n.reshape(nb, GROUP_W), m_pad,
        lw['dn_norm_g'], lw['ml_norm_g'], s_all, c_all, s_acc, c_acc, layer)
    x = matmul_resnorm([ya, yb, yc, yd], lw['w_out'], layer, g[3], x, scale=1.0, tm=128)
    q = norm_matmul(x, g[4], [lw['x_wq']], layer, tm=128, tn=GROUP_W, out_dtype=F32)
    o = sample_xattn(q, k_all, v_all, layer)
    x = matmul_resnorm([o], lw['x_wo'], layer, g[5], x, scale=1.0, tm=128)
    x = _ffn(x, g[6], g[7], lw['ffn2_wg'], lw['ffn2_wu'], lw['ffn2_wd'], layer, tm_up=128, tm_down=128)
    states = (pool_new.reshape(pool_st.shape), dnc_new.reshape(dnc_st.shape),
              n_new.reshape(ml_n.shape), m_new[:, :N_HEADS], scc_new.reshape(scc_st.shape))
    return x, states, (s_acc, c_acc)


def kernel(x_prompt, x_sample, mem_prompt, state_pool, state_dn_conv, state_dn_S, state_ml_C, state_ml_n,
           state_ml_m, state_sc_conv, cache_mem_k, cache_mem_v, norm_g, w_in, w_out, pool_w, pool_scale,
           dn_conv_w, dn_A_log, dn_dt_bias, dn_norm_g, ml_b_i, ml_b_f, ml_norm_g, sc_conv_w,
           x_wq, x_wk, x_wv, x_wo, ffn1_wg, ffn1_wu, ffn1_wd, ffn2_wg, ffn2_wu, ffn2_wd):
    depth = norm_g.shape[0]
    B, T, _ = x_prompt.shape
    nb, t_dec, _ = x_sample.shape
    assert t_dec == 1
    start_pos = PAST_LEN

    stacked = dict(w_in=_prep_w_in(w_in), w_out=w_out.astype(BF16), x_wq=x_wq.astype(BF16),
                   x_wkv=jnp.concatenate([x_wk.astype(BF16), x_wv.astype(BF16)], axis=-1),
                   x_wo=x_wo.astype(BF16),
                   ffn1_wg=ffn1_wg.astype(BF16), ffn1_wu=ffn1_wu.astype(BF16), ffn1_wd=ffn1_wd.astype(BF16),
                   ffn2_wg=ffn2_wg.astype(BF16), ffn2_wu=ffn2_wu.astype(BF16), ffn2_wd=ffn2_wd.astype(BF16))

    def layer_weights(l):
        return dict(stacked, norm_g=norm_g[l], pool_w=pool_w[l],
                    pool_scale=pool_scale[l], dn_conv_w=dn_conv_w[l], dn_A_log=dn_A_log[l],
                    dn_dt_bias=dn_dt_bias[l], dn_norm_g=dn_norm_g[l], ml_b_i=ml_b_i[l], ml_b_f=ml_b_f[l],
                    ml_norm_g=ml_norm_g[l], sc_conv_w=sc_conv_w[l])

    mem2d = mem_prompt.reshape(B * MEM_LEN, D_MODEL)
    h = x_prompt.reshape(B * T, D_MODEL)
    p_states, mem_k_list, mem_v_list = [], [], []
    for l in range(depth):
        h, ns, mk, mv = _prompt_layer(h, mem2d, B, T, layer_weights(l), l)
        p_states.append(ns)
        mem_k_list.append(mk)
        mem_v_list.append(mv)
    y_prompt = h.reshape(B, T, D_MODEL)

    s_inputs = (state_pool, state_dn_conv, state_ml_n, state_ml_m, state_sc_conv)
    big = (state_dn_S, state_ml_C,
           cache_mem_k.reshape(depth, nb, MEM_LEN * N_HEADS, HEAD_DIM),
           cache_mem_v.reshape(depth, nb, MEM_LEN * N_HEADS, HEAD_DIM))
    h = x_sample.reshape(nb, D_MODEL)
    s_states = []
    acc = (lax.empty(state_dn_S.shape, F32), lax.empty(state_ml_C.shape, F32))
    for l in range(depth):
        st = tuple(s[l] for s in s_inputs)
        h, ns, acc = _sample_layer(h, st, big, acc, l, layer_weights(l), start_pos)
        s_states.append(ns)
    y_sample = h.reshape(nb, 1, D_MODEL)
    dn_S_s, ml_C_s = acc

    pool_p, dn_conv_p, dn_S_p, ml_C_p, ml_n_p, ml_m_p, sc_conv_p = [jnp.stack(z) for z in zip(*p_states)]
    pool_s, dn_conv_s, ml_n_s, ml_m_s, sc_conv_s = [jnp.stack(z) for z in zip(*s_states)]
    mem_k_p = jnp.stack(mem_k_list)
    mem_v_p = jnp.stack(mem_v_list)
    return (y_prompt, y_sample, pool_p, pool_s, dn_conv_p, dn_conv_s, dn_S_p, dn_S_s, ml_C_p, ml_C_s,
            ml_n_p, ml_n_s, ml_m_p, ml_m_s, sc_conv_p, sc_conv_s, mem_k_p, mem_v_p)
```

```python
import functools

import numpy as np
import jax
import jax.numpy as jnp
from jax import lax
from jax.experimental import pallas as pl
from jax.experimental.pallas import tpu as pltpu

F32 = jnp.float32
BF16 = jnp.bfloat16

EPS = 1e-6
D_MODEL = 2048
GROUP_W = 512
HEAD_DIM = 128
N_HEADS = GROUP_W // HEAD_DIM
CHUNK = 64
POOL_WINDOWS = (2, 4, 8, 16)
POOL_STATE = 15
DN_CONV = 4
SC_WIDTH = 3
MEM_LEN = 256
PAST_LEN = 16384
FFN_TN = 512
COL_A, COL_BQ, COL_BK, COL_BV, COL_BZ = 0, 1, 2, 3, 4
COL_CQ, COL_CK, COL_CV, COL_CO = 5, 6, 7, 8
COL_DB, COL_DC, COL_DH = 9, 10, 11
GATE_COL0 = 12 * GROUP_W
IN_W_PAD = GATE_COL0 + 128
IN_TN = 1280
LANE_BETA, LANE_A, LANE_I, LANE_F = 0, 4, 8, 12
TB = 256
SUB = 128
VMEM_LIMIT = 56 * 2**20


def _cparams(sem, vmem=None):
    return pltpu.CompilerParams(dimension_semantics=sem, vmem_limit_bytes=vmem)


def _bdot(a, b):
    return jnp.dot(a.astype(BF16), b.astype(BF16), preferred_element_type=F32)


def _bdot_nt(a, b):
    return lax.dot_general(a.astype(BF16), b.astype(BF16), (((1,), (1,)), ((), ())),
                           preferred_element_type=F32)


def _sigmoid(x):
    return 1.0 / (1.0 + jnp.exp(-x))


def _silu(x):
    return x * _sigmoid(x)


def _softplus(x):
    return jnp.maximum(x, 0.0) + jnp.log(1.0 + jnp.exp(-jnp.abs(x)))


def _col(x, idx):
    return x[:, idx:idx + 1]


def _resident(shape):
    nd = len(shape)
    return pl.BlockSpec(shape, lambda *_: (0,) * nd, pipeline_mode=pl.Buffered(1))


def _norm_into(x_ref, g_ref, h_ref):
    tm = x_ref.shape[0]
    rs = min(tm, 128)

    def body(i, carry):
        r = pl.multiple_of(i * rs, rs)
        x = x_ref[pl.ds(r, rs), :]
        ms = jnp.mean(x * x, axis=-1, keepdims=True)
        h_ref[pl.ds(r, rs), :] = (x * lax.rsqrt(ms + EPS) * g_ref[...]).astype(BF16)
        return carry

    lax.fori_loop(0, tm // rs, body, 0)


def _zero_past_width(y, n_valid):
    tn = y.shape[1]
    if n_valid % tn == 0:
        return y
    col = pl.program_id(1) * tn + lax.broadcasted_iota(jnp.int32, (1, tn), 1)
    return jnp.where(col < n_valid, y, 0.0)


def _norm_mm_kernel(x_ref, g_ref, w_ref, o_ref, h_ref, *, n_valid):
    @pl.when(pl.program_id(1) == 0)
    def _():
        _norm_into(x_ref, g_ref, h_ref)

    y = jnp.dot(h_ref[...], w_ref[...], preferred_element_type=F32)
    o_ref[...] = _zero_past_width(y, n_valid).astype(o_ref.dtype)


def _norm_swiglu_kernel(x_ref, g_ref, wg_ref, wu_ref, o_ref, h_ref, *, n_valid):
    @pl.when(pl.program_id(1) == 0)
    def _():
        _norm_into(x_ref, g_ref, h_ref)

    h = h_ref[...]
    a = jnp.dot(h, wg_ref[...], preferred_element_type=F32)
    b = jnp.dot(h, wu_ref[...], preferred_element_type=F32)
    o_ref[...] = _zero_past_width(_silu(a) * b, n_valid).astype(o_ref.dtype)


def norm_matmul(x, g, ws, layer, *, tm, tn, out_dtype):
    M, K = x.shape
    n_valid = ws[0].shape[2]
    n_steps = pl.cdiv(n_valid, tn)
    N = n_steps * tn
    tm = min(tm, M)
    kern = _norm_swiglu_kernel if len(ws) == 2 else _norm_mm_kernel
    return pl.pallas_call(
        functools.partial(kern, n_valid=n_valid),
        grid=(M // tm, n_steps),
        in_specs=[pl.BlockSpec((tm, K), lambda i, j: (i, 0)),
                  pl.BlockSpec((1, K), lambda i, j: (0, 0))]
        + [pl.BlockSpec((None, K, tn), lambda i, j: (layer, 0, j)) for _ in ws],
        out_specs=pl.BlockSpec((tm, tn), lambda i, j: (i, j)),
        out_shape=jax.ShapeDtypeStruct((M, N), out_dtype),
        scratch_shapes=[pltpu.VMEM((tm, K), BF16)],
        compiler_params=_cparams(("parallel", "arbitrary"), VMEM_LIMIT),
    )(x, g.reshape(1, K), *ws)


def _mm_resnorm_kernel(*refs, n_a, scale):
    a_refs = refs[:n_a]
    w_ref, g_ref, res_ref, o_ref = refs[n_a:]
    parts = [r[...].astype(BF16) for r in a_refs]
    a = parts[0] if n_a == 1 else jnp.concatenate(parts, axis=1)
    y = jnp.dot(a, w_ref[...], preferred_element_type=F32)
    ms = jnp.mean(y * y, axis=-1, keepdims=True)
    o_ref[...] = res_ref[...] + scale * (y * lax.rsqrt(ms + EPS) * g_ref[...])


def matmul_resnorm(a_list, w, layer, g, res, *, scale, tm):
    M, N = res.shape
    tm = min(tm, M)
    K = w.shape[1]
    widths = [K] if len(a_list) == 1 else [a.shape[1] for a in a_list]
    assert sum(widths) == K
    w_spec = pl.BlockSpec((None, K, N), lambda i: (layer, 0, 0), pipeline_mode=pl.Buffered(1))
    return pl.pallas_call(
        functools.partial(_mm_resnorm_kernel, n_a=len(a_list), scale=scale),
        grid=(M // tm,),
        in_specs=[pl.BlockSpec((tm, wd), lambda i: (i, 0)) for wd in widths]
        + [w_spec, _resident((1, N)), pl.BlockSpec((tm, N), lambda i: (i, 0))],
        out_specs=pl.BlockSpec((tm, N), lambda i: (i, 0)),
        out_shape=jax.ShapeDtypeStruct((M, N), F32),
        compiler_params=_cparams(("parallel",), VMEM_LIMIT),
    )(*a_list, w, g.reshape(1, N), res)


def _chunk_cumsum(x):
    n = x.shape[0]
    row = lax.broadcasted_iota(jnp.int32, (n, 1), 0) % CHUNK
    s = 1
    while s < CHUNK:
        x = x + jnp.where(row >= s, pltpu.roll(x, s, 0), 0.0)
        s *= 2
    return x


def _chunk_last(x):
    n = x.shape[0]
    parts = [jnp.broadcast_to(x[c * CHUNK + CHUNK - 1:c * CHUNK + CHUNK, :], (CHUNK, x.shape[1]))
             for c in range(n // CHUNK)]
    return jnp.concatenate(parts, axis=0)


def _pad_rows(x, c, n_chunks):
    z = jnp.zeros_like(x)
    return jnp.concatenate([x if i == c else z for i in range(n_chunks)], axis=0)


def _blk_masks(n):
    r = lax.broadcasted_iota(jnp.int32, (n, n), 0)
    c = lax.broadcasted_iota(jnp.int32, (n, n), 1)
    same = (r // CHUNK) == (c // CHUNK)
    return same & (c <= r), same & (c < r), r == c


def _split(a):
    ah = a.astype(BF16)
    return ah, (a - ah.astype(F32)).astype(BF16)


def _split_dot(a, b):
    d = lambda x, y: jnp.dot(x, y, preferred_element_type=F32)
    return d(a[0], b[0]) + d(a[0], b[1]) + d(a[1], b[0])


def _unit_lower_inverses(a_list, n):
    r = lax.broadcasted_iota(jnp.int32, (n, n), 0)
    c = lax.broadcasted_iota(jnp.int32, (n, n), 1)
    pair = lambda s: ((r // (2 * s)) == (c // (2 * s))) & ((r // s) != (c // s))
    eye = (r == c).astype(F32)
    xs = [eye - jnp.where(pair(1), a, 0.0) for a in a_list]
    s = 2
    while s < CHUNK:
        m = pair(s)
        x_sp = [_split(x) for x in xs]
        ts = [_split_dot(_split(jnp.where(m, a, 0.0)), x) for a, x in zip(a_list, x_sp)]
        xs = [x - _split_dot(xp, _split(t)) for x, xp, t in zip(xs, x_sp, ts)]
        s *= 2
    return xs


def _pool_sconv_step(ua_ref, db_ref, dc_ref, dh_ref, pw_ref, ps_ref, sw_ref, ya_ref, yd_ref, hist_a, hist_d,
                     *, tb):
    i = pl.program_id(1)

    u = ua_ref[...]
    ext = jnp.concatenate([hist_a[...], u], axis=0)
    a2 = ext + pltpu.roll(ext, 1, 0)
    a4 = a2[:, 128:] + pltpu.roll(a2[:, 128:], 2, 0)
    a8 = a4[:, 128:] + pltpu.roll(a4[:, 128:], 4, 0)
    a16 = a8[:, 128:] + pltpu.roll(a8[:, 128:], 8, 0)
    sums = (a2[16:, :128], a4[16:, :128], a8[16:, :128], a16[16:, :])
    pos = i * tb + lax.broadcasted_iota(jnp.int32, (tb, 1), 0)
    ys = []
    for g, w in enumerate(POOL_WINDOWS):
        cnt = jnp.minimum(pos + 1, w).astype(F32)
        d = sums[g] / cnt - u[:, g * 128:(g + 1) * 128]
        ys.append(_bdot(d, pw_ref[g]))
    ya_ref[...] = (jnp.concatenate(ys, axis=1) * ps_ref[...]).astype(ya_ref.dtype)
    hist_a[...] = u[tb - 16:, :]

    us = dc_ref[...] * dh_ref[...]
    ext = jnp.concatenate([hist_d[...], us], axis=0)
    sw = sw_ref[...]
    y = sw[0:1] * pltpu.roll(ext, 2, 0) + sw[1:2] * pltpu.roll(ext, 1, 0) + sw[2:3] * ext
    yd_ref[...] = (db_ref[...] * y[8:]).astype(yd_ref.dtype)
    hist_d[...] = us[tb - 8:, :]
    return u[tb - 16:, :], us[tb - 8:, :]


def _deltanet_step(q_ref, k_ref, v_ref, z_ref, gt_ref, cw_ref, alog_ref, dtb_ref, ng_ref, y_ref, s_ref, hist_ref,
                   *, tb):
    nc = tb // CHUNK

    qkv = jnp.concatenate([q_ref[...], k_ref[...], v_ref[...]], axis=1)
    ext = jnp.concatenate([hist_ref[...], qkv], axis=0)
    cw = cw_ref[...]
    conv = (cw[0:1] * pltpu.roll(ext, 3, 0) + cw[1:2] * pltpu.roll(ext, 2, 0)
            + cw[2:3] * pltpu.roll(ext, 1, 0) + cw[3:4] * ext)[8:]
    act = _silu(conv)
    hist_ref[...] = qkv[tb - 8:, :]

    gates = gt_ref[...]
    beta_all = _sigmoid(gates)
    g_all = -jnp.exp(alog_ref[...]) * _softplus(gates + dtb_ref[...])
    gcum_all = _chunk_cumsum(g_all)
    glast_all = _chunk_last(gcum_all)
    gcum_t = gcum_all.T
    causal, strict, _ = _blk_masks(SUB)

    heads = range(N_HEADS)
    subs = range(tb // SUB)
    cps = SUB // CHUNK
    hd = lambda base, h: slice(base + h * HEAD_DIM, base + (h + 1) * HEAD_DIM)
    qs, ks, gcs, gls, betas = [], [], [], [], []
    for h in heads:
        qh = act[:, hd(0, h)]
        kh = act[:, hd(GROUP_W, h)]
        qs.append(qh * lax.rsqrt(jnp.sum(qh * qh, axis=-1, keepdims=True) + EPS) * HEAD_DIM ** -0.5)
        ks.append(kh * lax.rsqrt(jnp.sum(kh * kh, axis=-1, keepdims=True) + EPS))
        betas.append(_col(beta_all, LANE_BETA + h))
        gcs.append(_col(gcum_all, LANE_A + h))
        gls.append(_col(glast_all, LANE_A + h))
    k_ts, a_mats, qks = {}, [], {}
    for st in subs:
        R = slice(st * SUB, (st + 1) * SUB)
        for h in heads:
            gr = gcum_t[LANE_A + h:LANE_A + h + 1, R]
            decay = jnp.where(causal, jnp.exp(jnp.where(causal, gcs[h][R] - gr, 0.0)), 0.0)
            k_t = ks[h][R].T.astype(BF16)
            a_mats.append(jnp.where(strict, _bdot(ks[h][R] * betas[h][R], k_t) * decay, 0.0))
            qks[st, h] = jnp.where(causal, _bdot(qs[h][R], k_t) * decay, 0.0).astype(BF16)
            k_ts[st, h] = k_t
    tinvs = _unit_lower_inverses(a_mats, SUB)
    us, ws = {}, {}
    for st in subs:
        R = slice(st * SUB, (st + 1) * SUB)
        for h in heads:
            vb = act[R, hd(2 * GROUP_W, h)] * betas[h][R]
            kbe = ks[h][R] * (betas[h][R] * jnp.exp(gcs[h][R]))
            uw = _bdot(tinvs[st * N_HEADS + h], jnp.concatenate([vb, kbe], axis=1))
            us[st, h] = uw[:, :HEAD_DIM]
            ws[st, h] = uw[:, HEAD_DIM:].astype(BF16)
    q_decs = [(qs[h] * jnp.exp(gcs[h])).astype(BF16) for h in heads]
    tails = [jnp.exp(gls[h] - gcs[h]) for h in heads]
    s_hs = [s_ref[h] for h in heads]
    outs = [[] for _ in heads]
    for c in range(nc):
        r = slice(c * CHUNK, (c + 1) * CHUNK)
        st, cl = divmod(c, cps)
        rl = slice(cl * CHUNK, (cl + 1) * CHUNK)
        for h in heads:
            s_b = s_hs[h].astype(BF16)
            v_new = us[st, h][rl] - jnp.dot(ws[st, h][rl], s_b, preferred_element_type=F32)
            outs[h].append(jnp.dot(q_decs[h][r], s_b, preferred_element_type=F32)
                           + _bdot(qks[st, h][rl], _pad_rows(v_new, cl, cps)))
            s_dec = jnp.exp(jnp.broadcast_to(gls[h][c * CHUNK:c * CHUNK + 1], (HEAD_DIM, 1)))
            s_hs[h] = s_hs[h] * s_dec + _bdot(k_ts[st, h], _pad_rows(v_new * tails[h][r], cl, cps))
    ys = []
    for h in heads:
        s_ref[h] = s_hs[h]
        o = jnp.concatenate(outs[h], axis=0)
        o = o * lax.rsqrt(jnp.mean(o * o, axis=-1, keepdims=True) + EPS) * ng_ref[...]
        ys.append(o * _silu(z_ref[:, hd(0, h)]))
    y_ref[...] = jnp.concatenate(ys, axis=1).astype(y_ref.dtype)
    return qkv[tb - 8:, :]


def _mlstm_step(q_ref, k_ref, v_ref, og_ref, gt_ref, bi_ref, bf_ref, ng_ref, y_ref, c_ref, nm_ref, *, tb):
    nc = tb // CHUNK

    gates = gt_ref[...]
    li_all = gates + bi_ref[...]
    lf_all = -_softplus(-(gates + bf_ref[...]))
    f_all = _chunk_cumsum(lf_all)
    flast_all = _chunk_last(f_all)
    f_t = f_all.T
    li_t = li_all.T
    causal, _, _ = _blk_masks(tb)

    heads = range(N_HEADS)
    hd = lambda h: slice(h * HEAD_DIM, (h + 1) * HEAD_DIM)
    qs, kss, k_ts, evs, eks, intras, den_intras, w_inters, inv_floor, decs = ([] for _ in range(10))
    for h in heads:
        qh = q_ref[:, hd(h)]
        kh = k_ref[:, hd(h)] * HEAD_DIM ** -0.5
        vh = v_ref[:, hd(h)]
        fc = _col(f_all, LANE_F + h)
        fl = _col(flast_all, LANE_F + h)
        a_w = fl - fc + _col(li_all, LANE_I + h)
        fr = f_t[LANE_F + h:LANE_F + h + 1, :]
        lir = li_t[LANE_I + h:LANE_I + h + 1, :]
        dm = jnp.where(causal, fc - fr + lir, -jnp.inf)
        mx = jnp.max(dm, axis=1, keepdims=True)
        m_h = nm_ref[N_HEADS + h:N_HEADS + h + 1, 0:1]
        m_prev, m_next, dec_h = [], [], []
        for c in range(nc):
            r = slice(c * CHUNK, (c + 1) * CHUNK)
            fl_c = fl[c * CHUNK:c * CHUNK + 1]
            m_new = jnp.maximum(m_h + fl_c, jnp.max(a_w[r], axis=0, keepdims=True))
            dec_h.append(jnp.exp(m_h + fl_c - m_new))
            m_prev.append(jnp.broadcast_to(m_h, (CHUNK, 1)))
            m_next.append(jnp.broadcast_to(m_new, (CHUNK, 1)))
            m_h = m_new
        nm_ref[N_HEADS + h:N_HEADS + h + 1, :] = jnp.broadcast_to(m_h, (1, HEAD_DIM))
        b = fc + jnp.concatenate(m_prev, axis=0)
        m_t = jnp.maximum(b, mx)
        k_t = kh.T.astype(BF16)
        w_intra = jnp.exp(dm - m_t) * _bdot(qh, k_t)
        e = jnp.exp(a_w - jnp.concatenate(m_next, axis=0))
        qs.append(qh)
        k_ts.append(k_t)
        evs.append(e * vh)
        eks.append(e * kh)
        intras.append(_bdot(w_intra, vh))
        den_intras.append(jnp.sum(w_intra, axis=-1, keepdims=True))
        w_inters.append(jnp.exp(b - m_t))
        inv_floor.append(jnp.exp(-m_t))
        decs.append(dec_h)
    c_hs = [c_ref[h] for h in heads]
    n_hs = [nm_ref[h:h + 1, :] for h in heads]
    outs = [[] for _ in heads]
    for c in range(nc):
        r = slice(c * CHUNK, (c + 1) * CHUNK)
        for h in heads:
            q_c = qs[h][r]
            num = w_inters[h][r] * _bdot(q_c, c_hs[h]) + intras[h][r]
            den = w_inters[h][r] * jnp.sum(q_c * n_hs[h], axis=-1, keepdims=True) + den_intras[h][r]
            outs[h].append(num / jnp.maximum(jnp.abs(den), inv_floor[h][r]))
            c_hs[h] = c_hs[h] * decs[h][c] + _bdot(k_ts[h], _pad_rows(evs[h][r], c, nc))
            n_hs[h] = n_hs[h] * decs[h][c] + jnp.sum(eks[h][r], axis=0, keepdims=True)
    ys = []
    for h in heads:
        c_ref[h] = c_hs[h]
        nm_ref[h:h + 1, :] = n_hs[h]
        hh = jnp.concatenate(outs[h], axis=0)
        hh = _sigmoid(og_ref[:, hd(h)]) * hh
        hh = hh * lax.rsqrt(jnp.mean(hh * hh, axis=-1, keepdims=True) + EPS) * ng_ref[h:h + 1, :]
        ys.append(hh)
    y_ref[...] = jnp.concatenate(ys, axis=1).astype(y_ref.dtype)


N_POOL_IN, N_DN_IN, N_ML_IN = 7, 9, 8


def _mixers_kernel(*refs, tb):
    ins = iter(refs[:N_POOL_IN + N_DN_IN + N_ML_IN])
    take = lambda n: [next(ins) for _ in range(n)]
    pool_in, dn_in, ml_in = take(N_POOL_IN), take(N_DN_IN), take(N_ML_IN)
    (ya_ref, yd_ref, yb_ref, yc_ref, ptail_ref, stail_ref, s_out_ref, ctail_ref, c_out_ref, nm_out_ref,
     hist_a, hist_d, s_ref, hist_b, c_ref, nm_ref) = refs[N_POOL_IN + N_DN_IN + N_ML_IN:]
    i = pl.program_id(1)

    @pl.when(i == 0)
    def _():
        for r in (hist_a, hist_d, s_ref, hist_b, c_ref, nm_ref):
            r[...] = jnp.zeros_like(r)

    qkv_tail = _deltanet_step(*dn_in, yb_ref, s_ref, hist_b, tb=tb)
    _mlstm_step(*ml_in, yc_ref, c_ref, nm_ref, tb=tb)
    u_tail, us_tail = _pool_sconv_step(*pool_in, ya_ref, yd_ref, hist_a, hist_d, tb=tb)

    @pl.when(i == pl.num_programs(1) - 1)
    def _():
        ptail_ref[0] = u_tail
        stail_ref[0] = us_tail
        s_out_ref[0] = s_ref[...]
        ctail_ref[0] = qkv_tail
        c_out_ref[0] = c_ref[...]
        nm_out_ref[0] = nm_ref[...]


def mixers_prompt(P, B, T, lw):
    nT = T // TB
    M = B * T
    col = lambda c: pl.BlockSpec((TB, GROUP_W), lambda b, i: (b * nT + i, c))
    gates = pl.BlockSpec((TB, 128), lambda b, i: (b * nT + i, GATE_COL0 // 128))
    const = lambda *shape: pl.BlockSpec(shape, lambda b, i: (0,) * len(shape))
    lane_row = lambda v, lane: jnp.zeros((1, 128), F32).at[0, lane:lane + N_HEADS].set(v)
    y_spec = pl.BlockSpec((TB, GROUP_W), lambda b, i: (b * nT + i, 0))
    per_seq = lambda *shape: pl.BlockSpec((1,) + shape, lambda b, i: (b,) + (0,) * len(shape))
    mat = (N_HEADS, HEAD_DIM, HEAD_DIM)
    pool_specs = [col(COL_A), col(COL_DB), col(COL_DC), col(COL_DH),
                  const(4, 128, 128), const(1, GROUP_W), const(SC_WIDTH, GROUP_W)]
    dn_specs = [col(COL_BQ), col(COL_BK), col(COL_BV), col(COL_BZ), gates,
                const(DN_CONV, 3 * GROUP_W), const(1, 128), const(1, 128), const(1, HEAD_DIM)]
    ml_specs = [col(COL_CQ), col(COL_CK), col(COL_CV), col(COL_CO), gates,
                const(1, 128), const(1, 128), const(N_HEADS, HEAD_DIM)]
    assert (len(pool_specs), len(dn_specs), len(ml_specs)) == (N_POOL_IN, N_DN_IN, N_ML_IN)
    y_shape = jax.ShapeDtypeStruct((M, GROUP_W), BF16)
    state = lambda *shape: jax.ShapeDtypeStruct((B,) + shape, F32)
    return pl.pallas_call(
        functools.partial(_mixers_kernel, tb=TB),
        grid=(B, nT),
        in_specs=pool_specs + dn_specs + ml_specs,
        out_specs=[y_spec] * 4 + [per_seq(16, GROUP_W), per_seq(8, GROUP_W), per_seq(*mat),
                                  per_seq(8, 3 * GROUP_W), per_seq(*mat), per_seq(2 * N_HEADS, HEAD_DIM)],
        out_shape=[y_shape] * 4 + [state(16, GROUP_W), state(8, GROUP_W), state(*mat),
                                   state(8, 3 * GROUP_W), state(*mat), state(2 * N_HEADS, HEAD_DIM)],
        scratch_shapes=[pltpu.VMEM((16, GROUP_W), F32), pltpu.VMEM((8, GROUP_W), F32),
                        pltpu.VMEM(mat, F32), pltpu.VMEM((8, 3 * GROUP_W), F32),
                        pltpu.VMEM(mat, F32), pltpu.VMEM((2 * N_HEADS, HEAD_DIM), F32)],
        compiler_params=_cparams(("parallel", "arbitrary"), VMEM_LIMIT),
    )(P, P, P, P, lw['pool_w'], lw['pool_scale'].reshape(1, GROUP_W), lw['sc_conv_w'],
      P, P, P, P, P, lw['dn_conv_w'], lane_row(lw['dn_A_log'], LANE_A), lane_row(lw['dn_dt_bias'], LANE_A),
      lw['dn_norm_g'].reshape(1, HEAD_DIM),
      P, P, P, P, P, lane_row(lw['ml_b_i'], LANE_I), lane_row(lw['ml_b_f'], LANE_F), lw['ml_norm_g'])


def _xattn_prompt_kernel(x_ref, gpre_ref, wq_ref, kv_ref, wo_ref, gpost_ref, o_ref, h_ref):
    _norm_into(x_ref, gpre_ref, h_ref)
    q = jnp.dot(h_ref[...], wq_ref[...], preferred_element_type=F32)
    outs = []
    for h in range(N_HEADS):
        sl = slice(h * HEAD_DIM, (h + 1) * HEAD_DIM)
        k_h = kv_ref[:, sl]
        v_h = kv_ref[:, GROUP_W + h * HEAD_DIM:GROUP_W + (h + 1) * HEAD_DIM]
        s = _bdot_nt(q[:, sl], k_h) * HEAD_DIM ** -0.5
        p = jnp.exp(s - jnp.max(s, axis=-1, keepdims=True))
        p = p / jnp.sum(p, axis=-1, keepdims=True)
        outs.append(_bdot(p, v_h))
    o = jnp.concatenate(outs, axis=1).astype(BF16)
    y = jnp.dot(o, wo_ref[...], preferred_element_type=F32)
    ms = jnp.mean(y * y, axis=-1, keepdims=True)
    o_ref[...] = x_ref[...] + y * lax.rsqrt(ms + EPS) * gpost_ref[...]


def xattn_prompt(x, B, T, g_pre, wq, kv, wo, layer, g_post, *, tq=512):
    nq = T // tq
    M = B * T
    return pl.pallas_call(
        _xattn_prompt_kernel,
        grid=(B, nq),
        in_specs=[pl.BlockSpec((tq, D_MODEL), lambda b, i: (b * nq + i, 0)),
                  pl.BlockSpec((1, D_MODEL), lambda b, i: (0, 0)),
                  pl.BlockSpec((None, D_MODEL, GROUP_W), lambda b, i: (layer, 0, 0)),
                  pl.BlockSpec((MEM_LEN, 2 * GROUP_W), lambda b, i: (b, 0)),
                  pl.BlockSpec((None, GROUP_W, D_MODEL), lambda b, i: (layer, 0, 0)),
                  pl.BlockSpec((1, D_MODEL), lambda b, i: (0, 0))],
        out_specs=pl.BlockSpec((tq, D_MODEL), lambda b, i: (b * nq + i, 0)),
        out_shape=jax.ShapeDtypeStruct((M, D_MODEL), F32),
        scratch_shapes=[pltpu.VMEM((tq, D_MODEL), BF16)],
        compiler_params=_cparams(("parallel", "parallel"), VMEM_LIMIT),
    )(x, g_pre.reshape(1, D_MODEL), wq, kv, wo, g_post.reshape(1, D_MODEL))


SC_BETA, SC_EG, SC_QKB, SC_LI, SC_LF, SC_QKC = 0, 4, 8, 12, 16, 20


def _sample_pre_kernel(p_ref, pool_ref, dnc_ref, scc_ref, pw_ref, ps_ref, cw_ref, alog_ref, dtb_ref,
                       bi_ref, bf_ref, sw_ref,
                       ya_ref, yd_ref, pool_out_ref, dnc_out_ref, scc_out_ref,
                       rows_b_ref, rows_c_ref, scal_ref, cols_ref, *, start_pos):
    nb = p_ref.shape[0]
    blk = lambda c: p_ref[:, c * GROUP_W:(c + 1) * GROUP_W]

    u = blk(COL_A)
    ys = []
    for g, w in enumerate(POOL_WINDOWS):
        lanes = slice(g * 128, (g + 1) * 128)
        tot = u[:, lanes]
        for r in range(POOL_STATE + 1 - w, POOL_STATE):
            tot = tot + pool_ref[:, r * GROUP_W + g * 128:r * GROUP_W + (g + 1) * 128]
        d = tot / float(min(start_pos + 1, w)) - u[:, lanes]
        ys.append(_bdot(d, pw_ref[g]))
    ya_ref[...] = (jnp.concatenate(ys, axis=1) * ps_ref[...]).astype(ya_ref.dtype)
    pool_out_ref[:, :(POOL_STATE - 1) * GROUP_W] = pool_ref[:, GROUP_W:]
    pool_out_ref[:, (POOL_STATE - 1) * GROUP_W:] = u

    us = blk(COL_DC) * blk(COL_DH)
    sw = sw_ref[...]
    y = sw[0:1] * scc_ref[:, :GROUP_W] + sw[1:2] * scc_ref[:, GROUP_W:] + sw[2:3] * us
    yd_ref[...] = (blk(COL_DB) * y).astype(yd_ref.dtype)
    scc_out_ref[:, :GROUP_W] = scc_ref[:, GROUP_W:]
    scc_out_ref[:, GROUP_W:] = us

    qkv = p_ref[:, COL_BQ * GROUP_W:(COL_BV + 1) * GROUP_W]
    cw = cw_ref[...]
    W3 = 3 * GROUP_W
    conv = cw[DN_CONV - 1:DN_CONV] * qkv
    for j in range(DN_CONV - 1):
        conv = conv + cw[j:j + 1] * dnc_ref[:, j * W3:(j + 1) * W3]
    act = _silu(conv)
    dnc_out_ref[:, :(DN_CONV - 2) * W3] = dnc_ref[:, W3:]
    dnc_out_ref[:, (DN_CONV - 2) * W3:] = qkv

    gates = p_ref[:, GATE_COL0:GATE_COL0 + 128]
    beta_all = _sigmoid(gates)
    eg_all = jnp.exp(-jnp.exp(alog_ref[...]) * _softplus(gates + dtb_ref[...]))
    li_all = gates + bi_ref[...]
    lf_all = -_softplus(-(gates + bf_ref[...]))

    lane = lax.broadcasted_iota(jnp.int32, (nb, 128), 1)
    scal = jnp.zeros((nb, 128), F32)

    def put(tab, lane_idx, colv):
        return jnp.where(lane == lane_idx, colv, tab)

    qs, ks = [], []
    for h in range(N_HEADS):
        sl = slice(h * HEAD_DIM, (h + 1) * HEAD_DIM)
        qh = act[:, sl]
        kh = act[:, GROUP_W + h * HEAD_DIM:GROUP_W + (h + 1) * HEAD_DIM]
        qh = qh * lax.rsqrt(jnp.sum(qh * qh, axis=-1, keepdims=True) + EPS) * HEAD_DIM ** -0.5
        kh = kh * lax.rsqrt(jnp.sum(kh * kh, axis=-1, keepdims=True) + EPS)
        qs.append(qh)
        ks.append(kh)
        scal = put(scal, SC_BETA + h, _col(beta_all, LANE_BETA + h))
        scal = put(scal, SC_EG + h, _col(eg_all, LANE_A + h))
        scal = put(scal, SC_QKB + h, jnp.sum(qh * kh, axis=-1, keepdims=True))
        cols_ref[(0 * N_HEADS + h) * HEAD_DIM:(0 * N_HEADS + h + 1) * HEAD_DIM, :] = qh.T.astype(BF16)
        cols_ref[(1 * N_HEADS + h) * HEAD_DIM:(1 * N_HEADS + h + 1) * HEAD_DIM, :] = kh.T.astype(BF16)
    rows_b_ref[...] = jnp.concatenate(qs + ks + [act[:, 2 * GROUP_W:]], axis=1)

    qc = blk(COL_CQ)
    kc = blk(COL_CK) * HEAD_DIM ** -0.5
    for h in range(N_HEADS):
        sl = slice(h * HEAD_DIM, (h + 1) * HEAD_DIM)
        scal = put(scal, SC_LI + h, _col(li_all, LANE_I + h))
        scal = put(scal, SC_LF + h, _col(lf_all, LANE_F + h))
        scal = put(scal, SC_QKC + h, jnp.sum(qc[:, sl] * kc[:, sl], axis=-1, keepdims=True))
        cols_ref[(2 * N_HEADS + h) * HEAD_DIM:(2 * N_HEADS + h + 1) * HEAD_DIM, :] = qc[:, sl].T.astype(BF16)
        cols_ref[(3 * N_HEADS + h) * HEAD_DIM:(3 * N_HEADS + h + 1) * HEAD_DIM, :] = kc[:, sl].T.astype(BF16)
    rows_c_ref[...] = jnp.concatenate([qc, kc, blk(COL_CV)], axis=1)
    scal_ref[...] = scal


def sample_pre(P, pool_st, dnc_st, scc_st, pool_w, pool_scale, conv_w, a_log, dt_bias, b_i, b_f, sc_w,
               start_pos):
    nb = P.shape[0]
    lane_row = lambda v, lane: jnp.zeros((1, 128), F32).at[0, lane:lane + N_HEADS].set(v)
    W3 = 3 * GROUP_W
    out_shape = [jax.ShapeDtypeStruct((nb, GROUP_W), BF16),
                 jax.ShapeDtypeStruct((nb, GROUP_W), BF16),
                 jax.ShapeDtypeStruct(pool_st.shape, F32),
                 jax.ShapeDtypeStruct(dnc_st.shape, F32),
                 jax.ShapeDtypeStruct(scc_st.shape, F32),
                 jax.ShapeDtypeStruct((nb, W3), F32),
                 jax.ShapeDtypeStruct((nb, W3), F32),
                 jax.ShapeDtypeStruct((nb, 128), F32),
                 jax.ShapeDtypeStruct((4 * N_HEADS * HEAD_DIM, nb), BF16)]
    return pl.pallas_call(
        functools.partial(_sample_pre_kernel, start_pos=start_pos),
        out_shape=out_shape,
        compiler_params=pltpu.CompilerParams(vmem_limit_bytes=VMEM_LIMIT),
    )(P, pool_st, dnc_st, scc_st, pool_w, pool_scale.reshape(1, GROUP_W), conv_w,
      lane_row(a_log, LANE_A), lane_row(dt_bias, LANE_A), lane_row(b_i, LANE_I), lane_row(b_f, LANE_F), sc_w)


def _sample_rec_kernel(cols_ref, rows_b_ref, rows_c_ref, scal_ref, z_ref, og_ref, n_ref, m_ref,
                       dng_ref, mlg_ref, s_ref, c_ref, s_acc_ref, c_acc_ref,
                       yb_ref, yc_ref, s_out_ref, c_out_ref, n_out_ref, m_out_ref,
                       ob_ref, hc_ref, *, tb):
    del s_acc_ref, c_acc_ref
    i = pl.program_id(0)
    nb = cols_ref.shape[1]
    row_id = lax.broadcasted_iota(jnp.int32, (nb, 128), 0)
    lane_id = lax.broadcasted_iota(jnp.int32, (1, 128), 1)

    def body(j, carry):
        b = i * tb + j
        onehot = (row_id == b).astype(BF16)
        cols = jnp.dot(cols_ref[...], onehot, preferred_element_type=F32)
        scal = scal_ref[pl.ds(b, 1), :]
        sc = lambda idx: _col(scal, idx)
        m_row = m_ref[pl.ds(b, 1), :]
        rb = rows_b_ref[pl.ds(b, 1), :]
        rc = rows_c_ref[pl.ds(b, 1), :]
        n_all = n_ref[pl.ds(b, 1), :]
        m_new_row = jnp.zeros((1, 128), F32)
        o_rows, h_rows, n_rows = [], [], []
        for h in range(N_HEADS):
            sl = slice(h * HEAD_DIM, (h + 1) * HEAD_DIM)
            colblk = lambda v: cols[(v * N_HEADS + h) * HEAD_DIM:(v * N_HEADS + h + 1) * HEAD_DIM, :]
            s = s_ref[j, h]
            ks = jnp.sum(colblk(1) * s, axis=0, keepdims=True)
            qs = jnp.sum(colblk(0) * s, axis=0, keepdims=True)
            beta, eg, qk = sc(SC_BETA + h), sc(SC_EG + h), sc(SC_QKB + h)
            v_row = rb[:, 2 * GROUP_W + h * HEAD_DIM:2 * GROUP_W + (h + 1) * HEAD_DIM]
            v_new = beta * v_row - (beta * eg) * ks
            o_rows.append(eg * qs + qk * v_new)
            s_out_ref[j, h] = s * eg + colblk(1) * v_new
            cm = c_ref[j, h]
            qc = jnp.sum(colblk(2) * cm, axis=0, keepdims=True)
            q_row = rc[:, sl]
            k_row = rc[:, GROUP_W + h * HEAD_DIM:GROUP_W + (h + 1) * HEAD_DIM]
            vc_row = rc[:, 2 * GROUP_W + h * HEAD_DIM:2 * GROUP_W + (h + 1) * HEAD_DIM]
            n_row = n_all[:, sl]
            li, lf, qkc = sc(SC_LI + h), sc(SC_LF + h), sc(SC_QKC + h)
            m_old = _col(m_row, h)
            bb = lf + m_old
            m_t = jnp.maximum(bb, li)
            w_intra = jnp.exp(li - m_t) * qkc
            w_inter = jnp.exp(bb - m_t)
            num = w_inter * qc + w_intra * vc_row
            den = w_inter * jnp.sum(q_row * n_row, axis=-1, keepdims=True) + w_intra
            h_rows.append(num / jnp.maximum(jnp.abs(den), jnp.exp(-m_t)))
            m_new = jnp.maximum(m_old + lf, li)
            dec = jnp.exp(m_old + lf - m_new)
            e = jnp.exp(li - m_new)
            c_out_ref[j, h] = cm * dec + colblk(3) * (e * vc_row)
            n_rows.append(n_row * dec + e * k_row)
            m_new_row = jnp.where(lane_id == h, m_new, m_new_row)
        ob_ref[pl.ds(b, 1), :] = jnp.concatenate(o_rows, axis=1)
        hc_ref[pl.ds(b, 1), :] = jnp.concatenate(h_rows, axis=1)
        n_out_ref[pl.ds(b, 1), :] = jnp.concatenate(n_rows, axis=1)
        m_out_ref[pl.ds(b, 1), :] = m_new_row
        return carry

    lax.fori_loop(0, tb, body, 0, unroll=2)

    @pl.when(i == pl.num_programs(0) - 1)
    def _():
        ys_b, ys_c = [], []
        for h in range(N_HEADS):
            sl = slice(h * HEAD_DIM, (h + 1) * HEAD_DIM)
            o = ob_ref[:, sl]
            o = o * lax.rsqrt(jnp.mean(o * o, axis=-1, keepdims=True) + EPS) * dng_ref[...]
            ys_b.append(o * _silu(z_ref[:, sl]))
            hh = _sigmoid(og_ref[:, sl]) * hc_ref[:, sl]
            hh = hh * lax.rsqrt(jnp.mean(hh * hh, axis=-1, keepdims=True) + EPS) * mlg_ref[h:h + 1, :]
            ys_c.append(hh)
        yb_ref[...] = jnp.concatenate(ys_b, axis=1).astype(yb_ref.dtype)
        yc_ref[...] = jnp.concatenate(ys_c, axis=1).astype(yc_ref.dtype)


def sample_rec(cols, rows_b, rows_c, scal, P, n_st, m_st, dn_norm_g, ml_norm_g, s_all, c_all, s_acc, c_acc,
               layer, *, tb=8):
    nb = P.shape[0]
    W3 = 3 * GROUP_W
    full = lambda shape: pl.BlockSpec(shape, lambda i: (0,) * len(shape))
    state = pl.BlockSpec((None, tb, N_HEADS, HEAD_DIM, HEAD_DIM), lambda i: (layer, i, 0, 0, 0))
    untouched = pl.BlockSpec(memory_space=pl.ANY)
    return pl.pallas_call(
        functools.partial(_sample_rec_kernel, tb=tb),
        grid=(nb // tb,),
        in_specs=[full(cols.shape), full((nb, W3)), full((nb, W3)), full((nb, 128)),
                  pl.BlockSpec((nb, GROUP_W), lambda i: (0, COL_BZ)),
                  pl.BlockSpec((nb, GROUP_W), lambda i: (0, COL_CO)),
                  full((nb, GROUP_W)), full((nb, 128)), full((1, HEAD_DIM)), full((N_HEADS, HEAD_DIM)),
                  state, state, untouched, untouched],
        out_specs=[full((nb, GROUP_W)), full((nb, GROUP_W)), state, state,
                   full((nb, GROUP_W)), full((nb, 128))],
        out_shape=[jax.ShapeDtypeStruct((nb, GROUP_W), BF16),
                   jax.ShapeDtypeStruct((nb, GROUP_W), BF16),
                   jax.ShapeDtypeStruct(s_all.shape, F32),
                   jax.ShapeDtypeStruct(c_all.shape, F32),
                   jax.ShapeDtypeStruct((nb, GROUP_W), F32),
                   jax.ShapeDtypeStruct((nb, 128), F32)],
        input_output_aliases={12: 2, 13: 3},
        scratch_shapes=[pltpu.VMEM((nb, GROUP_W), F32), pltpu.VMEM((nb, GROUP_W), F32)],
        compiler_params=_cparams(("arbitrary",), VMEM_LIMIT),
    )(cols, rows_b, rows_c, scal, P, P, n_st, m_st, dn_norm_g.reshape(1, HEAD_DIM), ml_norm_g, s_all, c_all,
      s_acc, c_acc)


def _sample_xattn_kernel(q_ref, k_ref, v_ref, o_ref, *, tb):
    i = pl.program_id(0)
    n_rows = k_ref.shape[1]
    n_rep = n_rows // (2 * N_HEADS)
    ones = jnp.ones((HEAD_DIM, HEAD_DIM), BF16)
    fold = lambda x: x + pltpu.roll(x, N_HEADS, 0)

    def body(j, carry):
        b = i * tb + j
        q_row = q_ref[pl.ds(b, 1), :]
        heads = [q_row[:, h * HEAD_DIM:(h + 1) * HEAD_DIM] for h in range(N_HEADS)]
        q8 = jnp.concatenate(heads + heads, axis=0)
        k3 = k_ref[j].reshape(n_rep, 2 * N_HEADS, HEAD_DIM)
        prod = (k3 * q8[None]).reshape(n_rows, HEAD_DIM)
        ph, plo = _split(prod)
        s = (jnp.dot(ph, ones, preferred_element_type=F32)
             + jnp.dot(plo, ones, preferred_element_type=F32)) * HEAD_DIM ** -0.5
        s3 = s.reshape(n_rep, 2 * N_HEADS, HEAD_DIM)
        mx = jnp.max(s3, axis=0)
        mx = jnp.maximum(mx, pltpu.roll(mx, N_HEADS, 0))
        e3 = jnp.exp(s3 - mx[None])
        den = fold(jnp.sum(e3, axis=0))
        v3 = v_ref[j].reshape(n_rep, 2 * N_HEADS, HEAD_DIM)
        o8 = fold(jnp.sum(e3 * v3, axis=0)) / den
        o_ref[pl.ds(b, 1), :] = jnp.concatenate([o8[h:h + 1, :] for h in range(N_HEADS)], axis=1)
        return carry

    lax.fori_loop(0, tb, body, 0, unroll=4)


def sample_xattn(q, k_all, v_all, layer, *, tb=8):
    nb = q.shape[0]
    n_rows = k_all.shape[2]
    kv = pl.BlockSpec((None, tb, n_rows, HEAD_DIM), lambda i: (layer, i, 0, 0))
    return pl.pallas_call(
        functools.partial(_sample_xattn_kernel, tb=tb),
        grid=(nb // tb,),
        in_specs=[pl.BlockSpec((nb, GROUP_W), lambda i: (0, 0)), kv, kv],
        out_specs=pl.BlockSpec((nb, GROUP_W), lambda i: (0, 0)),
        out_shape=jax.ShapeDtypeStruct((nb, GROUP_W), F32),
        compiler_params=_cparams(("arbitrary",), VMEM_LIMIT),
    )(q, k_all, v_all)


W_B0 = 5 * GROUP_W
W_C0 = W_B0 + 2 * N_HEADS
W_C1 = W_C0 + 4 * GROUP_W
W_D0 = W_C1 + 2 * N_HEADS
W_D1 = W_D0 + 3 * GROUP_W


def _w_in_prep_kernel(w_ref, o_ref):
    x = w_ref[...]
    tr = x.shape[0]
    lane = lax.broadcasted_iota(jnp.int32, (tr, 128), 1)
    gates = jnp.where(lane < 2 * N_HEADS, x[:, W_B0:W_B0 + 128],
                      jnp.where(lane < 4 * N_HEADS, x[:, W_C1 - 2 * N_HEADS:W_C1 - 2 * N_HEADS + 128], 0.0))
    out = jnp.concatenate([x[:, :W_B0], x[:, W_C0:W_C1], x[:, W_D0:W_D1], gates], axis=1)
    o_ref[...] = out.astype(BF16)


def _prep_w_in(w_in, *, tr=256):
    depth, K, n_in = w_in.shape
    assert n_in == W_D1
    return pl.pallas_call(
        _w_in_prep_kernel,
        grid=(depth, K // tr),
        in_specs=[pl.BlockSpec((None, tr, n_in), lambda l, i: (l, i, 0))],
        out_specs=pl.BlockSpec((None, tr, IN_W_PAD), lambda l, i: (l, i, 0)),
        out_shape=jax.ShapeDtypeStruct((depth, K, IN_W_PAD), BF16),
        compiler_params=_cparams(("parallel", "parallel"), VMEM_LIMIT),
    )(w_in)


def _ffn(x, g_pre, g_post, wg, wu, wd, layer, *, tm_up, tm_down):
    act = norm_matmul(x, g_pre, [wg, wu], layer, tm=tm_up, tn=FFN_TN, out_dtype=BF16)
    return matmul_resnorm([act], wd, layer, g_post, x, scale=0.5, tm=tm_down)


def _prompt_layer(x, mem2d, B, T, lw, layer):
    g = lw['norm_g']
    x = _ffn(x, g[0], g[1], lw['ffn1_wg'], lw['ffn1_wu'], lw['ffn1_wd'], layer, tm_up=1024, tm_down=256)
    P = norm_matmul(x, g[2], [lw['w_in']], layer, tm=1024, tn=IN_TN, out_dtype=F32)
    ya, yd, yb, yc, pool_tail, sc_tail, dn_s, dn_tail, ml_c, ml_nm = mixers_prompt(P, B, T, lw)
    x = matmul_resnorm([ya, yb, yc, yd], lw['w_out'], layer, g[3], x, scale=1.0, tm=512)
    kv = norm_matmul(mem2d, g[8], [lw['x_wkv']], layer, tm=1024, tn=512, out_dtype=F32)
    x = xattn_prompt(x, B, T, g[4], lw['x_wq'], kv, lw['x_wo'], layer, g[5])
    x = _ffn(x, g[6], g[7], lw['ffn2_wg'], lw['ffn2_wu'], lw['ffn2_wd'], layer, tm_up=1024, tm_down=256)
    states = (pool_tail[:, 16 - POOL_STATE:], dn_tail[:, 8 - (DN_CONV - 1):], dn_s, ml_c,
              ml_nm[:, :N_HEADS], ml_nm[:, N_HEADS:, 0], sc_tail[:, 8 - (SC_WIDTH - 1):])
    mem_k = kv[:, :GROUP_W].reshape(B, MEM_LEN, N_HEADS, HEAD_DIM)
    mem_v = kv[:, GROUP_W:].reshape(B, MEM_LEN, N_HEADS, HEAD_DIM)
    return x, states, mem_k, mem_v


def _sample_layer(x, st, big, acc, layer, lw, start_pos):
    pool_st, dnc_st, ml_n, ml_m, scc_st = st
    s_all, c_all, k_all, v_all = big
    s_acc, c_acc = acc
    nb = x.shape[0]
    g = lw['norm_g']
    x = _ffn(x, g[0], g[1], lw['ffn1_wg'], lw['ffn1_wu'], lw['ffn1_wd'], layer, tm_up=128, tm_down=128)
    P = norm_matmul(x, g[2], [lw['w_in']], layer, tm=128, tn=IN_TN, out_dtype=F32)
    (ya, yd, pool_new, dnc_new, scc_new, rows_b, rows_c, scal, cols) = sample_pre(
        P, pool_st.reshape(nb, -1), dnc_st.reshape(nb, -1), scc_st.reshape(nb, -1),
        lw['pool_w'], lw['pool_scale'], lw['dn_conv_w'], lw['dn_A_log'], lw['dn_dt_bias'],
        lw['ml_b_i'], lw['ml_b_f'], lw['sc_conv_w'], start_pos)
    m_pad = jnp.pad(ml_m, ((0, 0), (0, 128 - N_HEADS)))
    yb, yc, s_acc, c_acc, n_new, m_new = sample_rec(
        cols, rows_b, rows_c, scal, P, ml_n.reshape(nb, GROUP_W), m_pad,
        lw['dn_norm_g'], lw['ml_norm_g'], s_all, c_all, s_acc, c_acc, layer)
    x = matmul_resnorm([ya, yb, yc, yd], lw['w_out'], layer, g[3], x, scale=1.0, tm=128)
    q = norm_matmul(x, g[4], [lw['x_wq']], layer, tm=128, tn=GROUP_W, out_dtype=F32)
    o = sample_xattn(q, k_all, v_all, layer)
    x = matmul_resnorm([o], lw['x_wo'], layer, g[5], x, scale=1.0, tm=128)
    x = _ffn(x, g[6], g[7], lw['ffn2_wg'], lw['ffn2_wu'], lw['ffn2_wd'], layer, tm_up=128, tm_down=128)
    states = (pool_new.reshape(pool_st.shape), dnc_new.reshape(dnc_st.shape),
              n_new.reshape(ml_n.shape), m_new[:, :N_HEADS], scc_new.reshape(scc_st.shape))
    return x, states, (s_acc, c_acc)


def kernel(x_prompt, x_sample, mem_prompt, state_pool, state_dn_conv, state_dn_S, state_ml_C, state_ml_n,
           state_ml_m, state_sc_conv, cache_mem_k, cache_mem_v, norm_g, w_in, w_out, pool_w, pool_scale,
           dn_conv_w, dn_A_log, dn_dt_bias, dn_norm_g, ml_b_i, ml_b_f, ml_norm_g, sc_conv_w,
           x_wq, x_wk, x_wv, x_wo, ffn1_wg, ffn1_wu, ffn1_wd, ffn2_wg, ffn2_wu, ffn2_wd):
    depth = norm_g.shape[0]
    B, T, _ = x_prompt.shape
    nb, t_dec, _ = x_sample.shape
    assert t_dec == 1
    start_pos = PAST_LEN

    stacked = dict(w_in=_prep_w_in(w_in), w_out=w_out.astype(BF16), x_wq=x_wq.astype(BF16),
                   x_wkv=jnp.concatenate([x_wk.astype(BF16), x_wv.astype(BF16)], axis=-1),
                   x_wo=x_wo.astype(BF16),
                   ffn1_wg=ffn1_wg.astype(BF16), ffn1_wu=ffn1_wu.astype(BF16), ffn1_wd=ffn1_wd.astype(BF16),
                   ffn2_wg=ffn2_wg.astype(BF16), ffn2_wu=ffn2_wu.astype(BF16), ffn2_wd=ffn2_wd.astype(BF16))

    def layer_weights(l):
        return dict(stacked, norm_g=norm_g[l], pool_w=pool_w[l],
                    pool_scale=pool_scale[l], dn_conv_w=dn_conv_w[l], dn_A_log=dn_A_log[l],
                    dn_dt_bias=dn_dt_bias[l], dn_norm_g=dn_norm_g[l], ml_b_i=ml_b_i[l], ml_b_f=ml_b_f[l],
                    ml_norm_g=ml_norm_g[l], sc_conv_w=sc_conv_w[l])

    mem2d = mem_prompt.reshape(B * MEM_LEN, D_MODEL)
    h = x_prompt.reshape(B * T, D_MODEL)
    p_states, mem_k_list, mem_v_list = [], [], []
    for l in range(depth):
        h, ns, mk, mv = _prompt_layer(h, mem2d, B, T, layer_weights(l), l)
        p_states.append(ns)
        mem_k_list.append(mk)
        mem_v_list.append(mv)
    y_prompt = h.reshape(B, T, D_MODEL)

    s_inputs = (state_pool, state_dn_conv, state_ml_n, state_ml_m, state_sc_conv)
    big = (state_dn_S, state_ml_C,
           cache_mem_k.reshape(depth, nb, MEM_LEN * N_HEADS, HEAD_DIM),
           cache_mem_v.reshape(depth, nb, MEM_LEN * N_HEADS, HEAD_DIM))
    h = x_sample.reshape(nb, D_MODEL)
    s_states = []
    acc = (lax.empty(state_dn_S.shape, F32), lax.empty(state_ml_C.shape, F32))
    for l in range(depth):
        st = tuple(s[l] for s in s_inputs)
        h, ns, acc = _sample_layer(h, st, big, acc, l, layer_weights(l), start_pos)
        s_states.append(ns)
    y_sample = h.reshape(nb, 1, D_MODEL)
    dn_S_s, ml_C_s = acc

    pool_p, dn_conv_p, dn_S_p, ml_C_p, ml_n_p, ml_m_p, sc_conv_p = [jnp.stack(z) for z in zip(*p_states)]
    pool_s, dn_conv_s, ml_n_s, ml_m_s, sc_conv_s = [jnp.stack(z) for z in zip(*s_states)]
    mem_k_p = jnp.stack(mem_k_list)
    mem_v_p = jnp.stack(mem_v_list)
    return (y_prompt, y_sample, pool_p, pool_s, dn_conv_p, dn_conv_s, dn_S_p, dn_S_s, ml_C_p, ml_C_s,
            ml_n_p, ml_n_s, ml_m_p, ml_m_s, sc_conv_p, sc_conv_s, mem_k_p, mem_v_p)
```

```python
import functools

import numpy as np
import jax
import jax.numpy as jnp
from jax import lax
from jax.experimental import pallas as pl
from jax.experimental.pallas import tpu as pltpu

F32 = jnp.float32
BF16 = jnp.bfloat16

EPS = 1e-6
D_MODEL = 2048
GROUP_W = 512
HEAD_DIM = 128
N_HEADS = GROUP_W // HEAD_DIM
CHUNK = 64
POOL_WINDOWS = (2, 4, 8, 16)
POOL_STATE = 15
DN_CONV = 4
SC_WIDTH = 3
MEM_LEN = 256
PAST_LEN = 16384
FFN_TN = 512
COL_A, COL_BQ, COL_BK, COL_BV, COL_BZ = 0, 1, 2, 3, 4
COL_CQ, COL_CK, COL_CV, COL_CO = 5, 6, 7, 8
COL_DB, COL_DC, COL_DH = 9, 10, 11
GATE_COL0 = 12 * GROUP_W
IN_W_PAD = GATE_COL0 + 128
IN_TN = 1280
LANE_BETA, LANE_A, LANE_I, LANE_F = 0, 4, 8, 12
TB = 256
SUB = 128
VMEM_LIMIT = 56 * 2**20


def _cparams(sem, vmem=None):
    return pltpu.CompilerParams(dimension_semantics=sem, vmem_limit_bytes=vmem)


def _bdot(a, b):
    return jnp.dot(a.astype(BF16), b.astype(BF16), preferred_element_type=F32)


def _bdot_nt(a, b):
    return lax.dot_general(a.astype(BF16), b.astype(BF16), (((1,), (1,)), ((), ())),
                           preferred_element_type=F32)


def _sigmoid(x):
    return 1.0 / (1.0 + jnp.exp(-x))


def _silu(x):
    return x * _sigmoid(x)


def _softplus(x):
    return jnp.maximum(x, 0.0) + jnp.log(1.0 + jnp.exp(-jnp.abs(x)))


def _col(x, idx):
    return x[:, idx:idx + 1]


def _resident(shape):
    nd = len(shape)
    return pl.BlockSpec(shape, lambda *_: (0,) * nd, pipeline_mode=pl.Buffered(1))


def _norm_into(x_ref, g_ref, h_ref):
    tm = x_ref.shape[0]
    rs = min(tm, 128)

    def body(i, carry):
        r = pl.multiple_of(i * rs, rs)
        x = x_ref[pl.ds(r, rs), :]
        ms = jnp.mean(x * x, axis=-1, keepdims=True)
        h_ref[pl.ds(r, rs), :] = (x * lax.rsqrt(ms + EPS) * g_ref[...]).astype(BF16)
        return carry

    lax.fori_loop(0, tm // rs, body, 0)


def _zero_past_width(y, n_valid):
    tn = y.shape[1]
    if n_valid % tn == 0:
        return y
    col = pl.program_id(1) * tn + lax.broadcasted_iota(jnp.int32, (1, tn), 1)
    return jnp.where(col < n_valid, y, 0.0)


def _norm_mm_kernel(x_ref, g_ref, w_ref, o_ref, h_ref, *, n_valid, w_is_nk):
    @pl.when(pl.program_id(1) == 0)
    def _():
        _norm_into(x_ref, g_ref, h_ref)

    if w_is_nk:
        y = lax.dot_general(h_ref[...], w_ref[...], (((1,), (1,)), ((), ())), preferred_element_type=F32)
    else:
        y = jnp.dot(h_ref[...], w_ref[...], preferred_element_type=F32)
    o_ref[...] = _zero_past_width(y, n_valid).astype(o_ref.dtype)


def _norm_swiglu_kernel(x_ref, g_ref, wg_ref, wu_ref, o_ref, h_ref, *, n_valid):
    @pl.when(pl.program_id(1) == 0)
    def _():
        _norm_into(x_ref, g_ref, h_ref)

    h = h_ref[...]
    a = jnp.dot(h, wg_ref[...], preferred_element_type=F32)
    b = jnp.dot(h, wu_ref[...], preferred_element_type=F32)
    o_ref[...] = _zero_past_width(_silu(a) * b, n_valid).astype(o_ref.dtype)


def norm_matmul(x, g, ws, layer, *, tm, tn, out_dtype, w_is_nk=False):
    M, K = x.shape
    n_valid = ws[0].shape[1 if w_is_nk else 2]
    n_steps = pl.cdiv(n_valid, tn)
    N = n_steps * tn
    tm = min(tm, M)
    if len(ws) == 2:
        assert not w_is_nk
        kern = functools.partial(_norm_swiglu_kernel, n_valid=n_valid)
    else:
        kern = functools.partial(_norm_mm_kernel, n_valid=n_valid, w_is_nk=w_is_nk)
    w_spec = (pl.BlockSpec((None, tn, K), lambda i, j: (layer, j, 0)) if w_is_nk
              else pl.BlockSpec((None, K, tn), lambda i, j: (layer, 0, j)))
    return pl.pallas_call(
        kern,
        grid=(M // tm, n_steps),
        in_specs=[pl.BlockSpec((tm, K), lambda i, j: (i, 0)),
                  pl.BlockSpec((1, K), lambda i, j: (0, 0))]
        + [w_spec for _ in ws],
        out_specs=pl.BlockSpec((tm, tn), lambda i, j: (i, j)),
        out_shape=jax.ShapeDtypeStruct((M, N), out_dtype),
        scratch_shapes=[pltpu.VMEM((tm, K), BF16)],
        compiler_params=_cparams(("parallel", "arbitrary"), VMEM_LIMIT),
    )(x, g.reshape(1, K), *ws)


def _mm_resnorm_kernel(*refs, n_a, scale):
    a_refs = refs[:n_a]
    w_ref, g_ref, res_ref, o_ref = refs[n_a:]
    parts = [r[...].astype(BF16) for r in a_refs]
    a = parts[0] if n_a == 1 else jnp.concatenate(parts, axis=1)
    y = jnp.dot(a, w_ref[...], preferred_element_type=F32)
    ms = jnp.mean(y * y, axis=-1, keepdims=True)
    o_ref[...] = res_ref[...] + scale * (y * lax.rsqrt(ms + EPS) * g_ref[...])


def matmul_resnorm(a_list, w, layer, g, res, *, scale, tm):
    M, N = res.shape
    tm = min(tm, M)
    K = w.shape[1]
    widths = [K] if len(a_list) == 1 else [a.shape[1] for a in a_list]
    assert sum(widths) == K
    w_spec = pl.BlockSpec((None, K, N), lambda i: (layer, 0, 0), pipeline_mode=pl.Buffered(1))
    return pl.pallas_call(
        functools.partial(_mm_resnorm_kernel, n_a=len(a_list), scale=scale),
        grid=(M // tm,),
        in_specs=[pl.BlockSpec((tm, wd), lambda i: (i, 0)) for wd in widths]
        + [w_spec, _resident((1, N)), pl.BlockSpec((tm, N), lambda i: (i, 0))],
        out_specs=pl.BlockSpec((tm, N), lambda i: (i, 0)),
        out_shape=jax.ShapeDtypeStruct((M, N), F32),
        compiler_params=_cparams(("parallel",), VMEM_LIMIT),
    )(*a_list, w, g.reshape(1, N), res)


def _chunk_cumsum(x):
    n = x.shape[0]
    row = lax.broadcasted_iota(jnp.int32, (n, 1), 0) % CHUNK
    s = 1
    while s < CHUNK:
        x = x + jnp.where(row >= s, pltpu.roll(x, s, 0), 0.0)
        s *= 2
    return x


def _chunk_last(x):
    n = x.shape[0]
    parts = [jnp.broadcast_to(x[c * CHUNK + CHUNK - 1:c * CHUNK + CHUNK, :], (CHUNK, x.shape[1]))
             for c in range(n // CHUNK)]
    return jnp.concatenate(parts, axis=0)


def _pad_rows(x, c, n_chunks):
    z = jnp.zeros_like(x)
    return jnp.concatenate([x if i == c else z for i in range(n_chunks)], axis=0)


def _blk_masks(n):
    r = lax.broadcasted_iota(jnp.int32, (n, n), 0)
    c = lax.broadcasted_iota(jnp.int32, (n, n), 1)
    same = (r // CHUNK) == (c // CHUNK)
    return same & (c <= r), same & (c < r), r == c


def _split(a):
    ah = a.astype(BF16)
    return ah, (a - ah.astype(F32)).astype(BF16)


def _split_dot(a, b):
    d = lambda x, y: jnp.dot(x, y, preferred_element_type=F32)
    return d(a[0], b[0]) + d(a[0], b[1]) + d(a[1], b[0])


def _unit_lower_inverses(a_list, n):
    r = lax.broadcasted_iota(jnp.int32, (n, n), 0)
    c = lax.broadcasted_iota(jnp.int32, (n, n), 1)
    pair = lambda s: ((r // (2 * s)) == (c // (2 * s))) & ((r // s) != (c // s))
    eye = (r == c).astype(F32)
    xs = [eye - jnp.where(pair(1), a, 0.0) for a in a_list]
    s = 2
    while s < CHUNK:
        m = pair(s)
        x_sp = [_split(x) for x in xs]
        ts = [_split_dot(_split(jnp.where(m, a, 0.0)), x) for a, x in zip(a_list, x_sp)]
        xs = [x - _split_dot(xp, _split(t)) for x, xp, t in zip(xs, x_sp, ts)]
        s *= 2
    return xs


def _pool_sconv_step(ua_ref, db_ref, dc_ref, dh_ref, pw_ref, ps_ref, sw_ref, ya_ref, yd_ref, hist_a, hist_d,
                     *, tb):
    i = pl.program_id(1)

    u = ua_ref[...]
    ext = jnp.concatenate([hist_a[...], u], axis=0)
    a2 = ext + pltpu.roll(ext, 1, 0)
    a4 = a2[:, 128:] + pltpu.roll(a2[:, 128:], 2, 0)
    a8 = a4[:, 128:] + pltpu.roll(a4[:, 128:], 4, 0)
    a16 = a8[:, 128:] + pltpu.roll(a8[:, 128:], 8, 0)
    sums = (a2[16:, :128], a4[16:, :128], a8[16:, :128], a16[16:, :])
    pos = i * tb + lax.broadcasted_iota(jnp.int32, (tb, 1), 0)
    ys = []
    for g, w in enumerate(POOL_WINDOWS):
        cnt = jnp.minimum(pos + 1, w).astype(F32)
        d = sums[g] / cnt - u[:, g * 128:(g + 1) * 128]
        ys.append(_bdot(d, pw_ref[g]))
    ya_ref[...] = (jnp.concatenate(ys, axis=1) * ps_ref[...]).astype(ya_ref.dtype)
    hist_a[...] = u[tb - 16:, :]

    us = dc_ref[...] * dh_ref[...]
    ext = jnp.concatenate([hist_d[...], us], axis=0)
    sw = sw_ref[...]
    y = sw[0:1] * pltpu.roll(ext, 2, 0) + sw[1:2] * pltpu.roll(ext, 1, 0) + sw[2:3] * ext
    yd_ref[...] = (db_ref[...] * y[8:]).astype(yd_ref.dtype)
    hist_d[...] = us[tb - 8:, :]
    return u[tb - 16:, :], us[tb - 8:, :]


def _deltanet_step(q_ref, k_ref, v_ref, z_ref, gt_ref, cw_ref, alog_ref, dtb_ref, ng_ref, y_ref, s_ref, hist_ref,
                   *, tb):
    nc = tb // CHUNK

    qkv = jnp.concatenate([q_ref[...], k_ref[...], v_ref[...]], axis=1)
    ext = jnp.concatenate([hist_ref[...], qkv], axis=0)
    cw = cw_ref[...]
    conv = (cw[0:1] * pltpu.roll(ext, 3, 0) + cw[1:2] * pltpu.roll(ext, 2, 0)
            + cw[2:3] * pltpu.roll(ext, 1, 0) + cw[3:4] * ext)[8:]
    act = _silu(conv)
    hist_ref[...] = qkv[tb - 8:, :]

    gates = gt_ref[...]
    beta_all = _sigmoid(gates)
    g_all = -jnp.exp(alog_ref[...]) * _softplus(gates + dtb_ref[...])
    gcum_all = _chunk_cumsum(g_all)
    glast_all = _chunk_last(gcum_all)
    gcum_t = gcum_all.T
    causal, strict, _ = _blk_masks(SUB)

    heads = range(N_HEADS)
    subs = range(tb // SUB)
    cps = SUB // CHUNK
    hd = lambda base, h: slice(base + h * HEAD_DIM, base + (h + 1) * HEAD_DIM)
    qs, ks, gcs, gls, betas = [], [], [], [], []
    for h in heads:
        qh = act[:, hd(0, h)]
        kh = act[:, hd(GROUP_W, h)]
        qs.append(qh * lax.rsqrt(jnp.sum(qh * qh, axis=-1, keepdims=True) + EPS) * HEAD_DIM ** -0.5)
        ks.append(kh * lax.rsqrt(jnp.sum(kh * kh, axis=-1, keepdims=True) + EPS))
        betas.append(_col(beta_all, LANE_BETA + h))
        gcs.append(_col(gcum_all, LANE_A + h))
        gls.append(_col(glast_all, LANE_A + h))
    k_ts, a_mats, qks = {}, [], {}
    for st in subs:
        R = slice(st * SUB, (st + 1) * SUB)
        for h in heads:
            gr = gcum_t[LANE_A + h:LANE_A + h + 1, R]
            decay = jnp.where(causal, jnp.exp(jnp.where(causal, gcs[h][R] - gr, 0.0)), 0.0)
            k_t = ks[h][R].T.astype(BF16)
            a_mats.append(jnp.where(strict, _bdot(ks[h][R] * betas[h][R], k_t) * decay, 0.0))
            qks[st, h] = jnp.where(causal, _bdot(qs[h][R], k_t) * decay, 0.0).astype(BF16)
            k_ts[st, h] = k_t
    tinvs = _unit_lower_inverses(a_mats, SUB)
    us, ws = {}, {}
    for st in subs:
        R = slice(st * SUB, (st + 1) * SUB)
        for h in heads:
            vb = act[R, hd(2 * GROUP_W, h)] * betas[h][R]
            kbe = ks[h][R] * (betas[h][R] * jnp.exp(gcs[h][R]))
            uw = _bdot(tinvs[st * N_HEADS + h], jnp.concatenate([vb, kbe], axis=1))
            us[st, h] = uw[:, :HEAD_DIM]
            ws[st, h] = uw[:, HEAD_DIM:].astype(BF16)
    q_decs = [(qs[h] * jnp.exp(gcs[h])).astype(BF16) for h in heads]
    tails = [jnp.exp(gls[h] - gcs[h]) for h in heads]
    s_hs = [s_ref[h] for h in heads]
    outs = [[] for _ in heads]
    for c in range(nc):
        r = slice(c * CHUNK, (c + 1) * CHUNK)
        st, cl = divmod(c, cps)
        rl = slice(cl * CHUNK, (cl + 1) * CHUNK)
        for h in heads:
            s_b = s_hs[h].astype(BF16)
            v_new = us[st, h][rl] - jnp.dot(ws[st, h][rl], s_b, preferred_element_type=F32)
            outs[h].append(jnp.dot(q_decs[h][r], s_b, preferred_element_type=F32)
                           + _bdot(qks[st, h][rl], _pad_rows(v_new, cl, cps)))
            s_dec = jnp.exp(jnp.broadcast_to(gls[h][c * CHUNK:c * CHUNK + 1], (HEAD_DIM, 1)))
            s_hs[h] = s_hs[h] * s_dec + _bdot(k_ts[st, h], _pad_rows(v_new * tails[h][r], cl, cps))
    ys = []
    for h in heads:
        s_ref[h] = s_hs[h]
        o = jnp.concatenate(outs[h], axis=0)
        o = o * lax.rsqrt(jnp.mean(o * o, axis=-1, keepdims=True) + EPS) * ng_ref[...]
        ys.append(o * _silu(z_ref[:, hd(0, h)]))
    y_ref[...] = jnp.concatenate(ys, axis=1).astype(y_ref.dtype)
    return qkv[tb - 8:, :]


def _mlstm_step(q_ref, k_ref, v_ref, og_ref, gt_ref, bi_ref, bf_ref, ng_ref, y_ref, c_ref, nm_ref, *, tb):
    nc = tb // CHUNK

    gates = gt_ref[...]
    li_all = gates + bi_ref[...]
    lf_all = -_softplus(-(gates + bf_ref[...]))
    f_all = _chunk_cumsum(lf_all)
    flast_all = _chunk_last(f_all)
    f_t = f_all.T
    li_t = li_all.T
    causal, _, _ = _blk_masks(tb)

    heads = range(N_HEADS)
    hd = lambda h: slice(h * HEAD_DIM, (h + 1) * HEAD_DIM)
    qs, kss, k_ts, evs, eks, intras, den_intras, w_inters, inv_floor, decs = ([] for _ in range(10))
    for h in heads:
        qh = q_ref[:, hd(h)]
        kh = k_ref[:, hd(h)] * HEAD_DIM ** -0.5
        vh = v_ref[:, hd(h)]
        fc = _col(f_all, LANE_F + h)
        fl = _col(flast_all, LANE_F + h)
        a_w = fl - fc + _col(li_all, LANE_I + h)
        fr = f_t[LANE_F + h:LANE_F + h + 1, :]
        lir = li_t[LANE_I + h:LANE_I + h + 1, :]
        dm = jnp.where(causal, fc - fr + lir, -jnp.inf)
        mx = jnp.max(dm, axis=1, keepdims=True)
        m_h = nm_ref[N_HEADS + h:N_HEADS + h + 1, 0:1]
        m_prev, m_next, dec_h = [], [], []
        for c in range(nc):
            r = slice(c * CHUNK, (c + 1) * CHUNK)
            fl_c = fl[c * CHUNK:c * CHUNK + 1]
            m_new = jnp.maximum(m_h + fl_c, jnp.max(a_w[r], axis=0, keepdims=True))
            dec_h.append(jnp.exp(m_h + fl_c - m_new))
            m_prev.append(jnp.broadcast_to(m_h, (CHUNK, 1)))
            m_next.append(jnp.broadcast_to(m_new, (CHUNK, 1)))
            m_h = m_new
        nm_ref[N_HEADS + h:N_HEADS + h + 1, :] = jnp.broadcast_to(m_h, (1, HEAD_DIM))
        b = fc + jnp.concatenate(m_prev, axis=0)
        m_t = jnp.maximum(b, mx)
        k_t = kh.T.astype(BF16)
        w_intra = jnp.exp(dm - m_t) * _bdot(qh, k_t)
        e = jnp.exp(a_w - jnp.concatenate(m_next, axis=0))
        qs.append(qh)
        k_ts.append(k_t)
        evs.append(e * vh)
        eks.append(e * kh)
        intras.append(_bdot(w_intra, vh))
        den_intras.append(jnp.sum(w_intra, axis=-1, keepdims=True))
        w_inters.append(jnp.exp(b - m_t))
        inv_floor.append(jnp.exp(-m_t))
        decs.append(dec_h)
    c_hs = [c_ref[h] for h in heads]
    n_hs = [nm_ref[h:h + 1, :] for h in heads]
    outs = [[] for _ in heads]
    for c in range(nc):
        r = slice(c * CHUNK, (c + 1) * CHUNK)
        for h in heads:
            q_c = qs[h][r]
            num = w_inters[h][r] * _bdot(q_c, c_hs[h]) + intras[h][r]
            den = w_inters[h][r] * jnp.sum(q_c * n_hs[h], axis=-1, keepdims=True) + den_intras[h][r]
            outs[h].append(num / jnp.maximum(jnp.abs(den), inv_floor[h][r]))
            c_hs[h] = c_hs[h] * decs[h][c] + _bdot(k_ts[h], _pad_rows(evs[h][r], c, nc))
            n_hs[h] = n_hs[h] * decs[h][c] + jnp.sum(eks[h][r], axis=0, keepdims=True)
    ys = []
    for h in heads:
        c_ref[h] = c_hs[h]
        nm_ref[h:h + 1, :] = n_hs[h]
        hh = jnp.concatenate(outs[h], axis=0)
        hh = _sigmoid(og_ref[:, hd(h)]) * hh
        hh = hh * lax.rsqrt(jnp.mean(hh * hh, axis=-1, keepdims=True) + EPS) * ng_ref[h:h + 1, :]
        ys.append(hh)
    y_ref[...] = jnp.concatenate(ys, axis=1).astype(y_ref.dtype)


N_POOL_IN, N_DN_IN, N_ML_IN = 7, 9, 8


def _mixers_kernel(*refs, tb):
    ins = iter(refs[:N_POOL_IN + N_DN_IN + N_ML_IN])
    take = lambda n: [next(ins) for _ in range(n)]
    pool_in, dn_in, ml_in = take(N_POOL_IN), take(N_DN_IN), take(N_ML_IN)
    (ya_ref, yd_ref, yb_ref, yc_ref, ptail_ref, stail_ref, s_out_ref, ctail_ref, c_out_ref, nm_out_ref,
     hist_a, hist_d, s_ref, hist_b, c_ref, nm_ref) = refs[N_POOL_IN + N_DN_IN + N_ML_IN:]
    i = pl.program_id(1)

    @pl.when(i == 0)
    def _():
        for r in (hist_a, hist_d, s_ref, hist_b, c_ref, nm_ref):
            r[...] = jnp.zeros_like(r)

    qkv_tail = _deltanet_step(*dn_in, yb_ref, s_ref, hist_b, tb=tb)
    _mlstm_step(*ml_in, yc_ref, c_ref, nm_ref, tb=tb)
    u_tail, us_tail = _pool_sconv_step(*pool_in, ya_ref, yd_ref, hist_a, hist_d, tb=tb)

    @pl.when(i == pl.num_programs(1) - 1)
    def _():
        ptail_ref[0] = u_tail
        stail_ref[0] = us_tail
        s_out_ref[0] = s_ref[...]
        ctail_ref[0] = qkv_tail
        c_out_ref[0] = c_ref[...]
        nm_out_ref[0] = nm_ref[...]


def mixers_prompt(P, B, T, lw):
    nT = T // TB
    M = B * T
    col = lambda c: pl.BlockSpec((TB, GROUP_W), lambda b, i: (b * nT + i, c))
    gates = pl.BlockSpec((TB, 128), lambda b, i: (b * nT + i, GATE_COL0 // 128))
    const = lambda *shape: pl.BlockSpec(shape, lambda b, i: (0,) * len(shape))
    lane_row = lambda v, lane: jnp.zeros((1, 128), F32).at[0, lane:lane + N_HEADS].set(v)
    y_spec = pl.BlockSpec((TB, GROUP_W), lambda b, i: (b * nT + i, 0))
    per_seq = lambda *shape: pl.BlockSpec((1,) + shape, lambda b, i: (b,) + (0,) * len(shape))
    mat = (N_HEADS, HEAD_DIM, HEAD_DIM)
    pool_specs = [col(COL_A), col(COL_DB), col(COL_DC), col(COL_DH),
                  const(4, 128, 128), const(1, GROUP_W), const(SC_WIDTH, GROUP_W)]
    dn_specs = [col(COL_BQ), col(COL_BK), col(COL_BV), col(COL_BZ), gates,
                const(DN_CONV, 3 * GROUP_W), const(1, 128), const(1, 128), const(1, HEAD_DIM)]
    ml_specs = [col(COL_CQ), col(COL_CK), col(COL_CV), col(COL_CO), gates,
                const(1, 128), const(1, 128), const(N_HEADS, HEAD_DIM)]
    assert (len(pool_specs), len(dn_specs), len(ml_specs)) == (N_POOL_IN, N_DN_IN, N_ML_IN)
    y_shape = jax.ShapeDtypeStruct((M, GROUP_W), BF16)
    state = lambda *shape: jax.ShapeDtypeStruct((B,) + shape, F32)
    return pl.pallas_call(
        functools.partial(_mixers_kernel, tb=TB),
        grid=(B, nT),
        in_specs=pool_specs + dn_specs + ml_specs,
        out_specs=[y_spec] * 4 + [per_seq(16, GROUP_W), per_seq(8, GROUP_W), per_seq(*mat),
                                  per_seq(8, 3 * GROUP_W), per_seq(*mat), per_seq(2 * N_HEADS, HEAD_DIM)],
        out_shape=[y_shape] * 4 + [state(16, GROUP_W), state(8, GROUP_W), state(*mat),
                                   state(8, 3 * GROUP_W), state(*mat), state(2 * N_HEADS, HEAD_DIM)],
        scratch_shapes=[pltpu.VMEM((16, GROUP_W), F32), pltpu.VMEM((8, GROUP_W), F32),
                        pltpu.VMEM(mat, F32), pltpu.VMEM((8, 3 * GROUP_W), F32),
                        pltpu.VMEM(mat, F32), pltpu.VMEM((2 * N_HEADS, HEAD_DIM), F32)],
        compiler_params=_cparams(("parallel", "arbitrary"), VMEM_LIMIT),
    )(P, P, P, P, lw['pool_w'], lw['pool_scale'].reshape(1, GROUP_W), lw['sc_conv_w'],
      P, P, P, P, P, lw['dn_conv_w'], lane_row(lw['dn_A_log'], LANE_A), lane_row(lw['dn_dt_bias'], LANE_A),
      lw['dn_norm_g'].reshape(1, HEAD_DIM),
      P, P, P, P, P, lane_row(lw['ml_b_i'], LANE_I), lane_row(lw['ml_b_f'], LANE_F), lw['ml_norm_g'])


def _xattn_prompt_kernel(x_ref, gpre_ref, wq_ref, kv_ref, wo_ref, gpost_ref, o_ref, h_ref):
    _norm_into(x_ref, gpre_ref, h_ref)
    q = jnp.dot(h_ref[...], wq_ref[...], preferred_element_type=F32)
    outs = []
    for h in range(N_HEADS):
        sl = slice(h * HEAD_DIM, (h + 1) * HEAD_DIM)
        k_h = kv_ref[:, sl]
        v_h = kv_ref[:, GROUP_W + h * HEAD_DIM:GROUP_W + (h + 1) * HEAD_DIM]
        s = _bdot_nt(q[:, sl], k_h) * HEAD_DIM ** -0.5
        p = jnp.exp(s - jnp.max(s, axis=-1, keepdims=True))
        p = p / jnp.sum(p, axis=-1, keepdims=True)
        outs.append(_bdot(p, v_h))
    o = jnp.concatenate(outs, axis=1).astype(BF16)
    y = jnp.dot(o, wo_ref[...], preferred_element_type=F32)
    ms = jnp.mean(y * y, axis=-1, keepdims=True)
    o_ref[...] = x_ref[...] + y * lax.rsqrt(ms + EPS) * gpost_ref[...]


def xattn_prompt(x, B, T, g_pre, wq, kv, wo, layer, g_post, *, tq=512):
    nq = T // tq
    M = B * T
    return pl.pallas_call(
        _xattn_prompt_kernel,
        grid=(B, nq),
        in_specs=[pl.BlockSpec((tq, D_MODEL), lambda b, i: (b * nq + i, 0)),
                  pl.BlockSpec((1, D_MODEL), lambda b, i: (0, 0)),
                  pl.BlockSpec((None, D_MODEL, GROUP_W), lambda b, i: (layer, 0, 0)),
                  pl.BlockSpec((MEM_LEN, 2 * GROUP_W), lambda b, i: (b, 0)),
                  pl.BlockSpec((None, GROUP_W, D_MODEL), lambda b, i: (layer, 0, 0)),
                  pl.BlockSpec((1, D_MODEL), lambda b, i: (0, 0))],
        out_specs=pl.BlockSpec((tq, D_MODEL), lambda b, i: (b * nq + i, 0)),
        out_shape=jax.ShapeDtypeStruct((M, D_MODEL), F32),
        scratch_shapes=[pltpu.VMEM((tq, D_MODEL), BF16)],
        compiler_params=_cparams(("parallel", "parallel"), VMEM_LIMIT),
    )(x, g_pre.reshape(1, D_MODEL), wq, kv, wo, g_post.reshape(1, D_MODEL))


SC_BETA, SC_EG, SC_QKB, SC_LI, SC_LF, SC_QKC = 0, 4, 8, 12, 16, 20


def _sample_pre_kernel(p_ref, pool_ref, dnc_ref, scc_ref, pw_ref, ps_ref, cw_ref, alog_ref, dtb_ref,
                       bi_ref, bf_ref, sw_ref,
                       ya_ref, yd_ref, pool_out_ref, dnc_out_ref, scc_out_ref,
                       rows_b_ref, rows_c_ref, scal_ref, cols_ref, *, start_pos):
    nb = p_ref.shape[0]
    blk = lambda c: p_ref[:, c * GROUP_W:(c + 1) * GROUP_W]

    u = blk(COL_A)
    ys = []
    for g, w in enumerate(POOL_WINDOWS):
        lanes = slice(g * 128, (g + 1) * 128)
        tot = u[:, lanes]
        for r in range(POOL_STATE + 1 - w, POOL_STATE):
            tot = tot + pool_ref[:, r * GROUP_W + g * 128:r * GROUP_W + (g + 1) * 128]
        d = tot / float(min(start_pos + 1, w)) - u[:, lanes]
        ys.append(_bdot(d, pw_ref[g]))
    ya_ref[...] = (jnp.concatenate(ys, axis=1) * ps_ref[...]).astype(ya_ref.dtype)
    pool_out_ref[:, :(POOL_STATE - 1) * GROUP_W] = pool_ref[:, GROUP_W:]
    pool_out_ref[:, (POOL_STATE - 1) * GROUP_W:] = u

    us = blk(COL_DC) * blk(COL_DH)
    sw = sw_ref[...]
    y = sw[0:1] * scc_ref[:, :GROUP_W] + sw[1:2] * scc_ref[:, GROUP_W:] + sw[2:3] * us
    yd_ref[...] = (blk(COL_DB) * y).astype(yd_ref.dtype)
    scc_out_ref[:, :GROUP_W] = scc_ref[:, GROUP_W:]
    scc_out_ref[:, GROUP_W:] = us

    qkv = p_ref[:, COL_BQ * GROUP_W:(COL_BV + 1) * GROUP_W]
    cw = cw_ref[...]
    W3 = 3 * GROUP_W
    conv = cw[DN_CONV - 1:DN_CONV] * qkv
    for j in range(DN_CONV - 1):
        conv = conv + cw[j:j + 1] * dnc_ref[:, j * W3:(j + 1) * W3]
    act = _silu(conv)
    dnc_out_ref[:, :(DN_CONV - 2) * W3] = dnc_ref[:, W3:]
    dnc_out_ref[:, (DN_CONV - 2) * W3:] = qkv

    gates = p_ref[:, GATE_COL0:GATE_COL0 + 128]
    beta_all = _sigmoid(gates)
    eg_all = jnp.exp(-jnp.exp(alog_ref[...]) * _softplus(gates + dtb_ref[...]))
    li_all = gates + bi_ref[...]
    lf_all = -_softplus(-(gates + bf_ref[...]))

    lane = lax.broadcasted_iota(jnp.int32, (nb, 128), 1)
    scal = jnp.zeros((nb, 128), F32)

    def put(tab, lane_idx, colv):
        return jnp.where(lane == lane_idx, colv, tab)

    qs, ks = [], []
    for h in range(N_HEADS):
        sl = slice(h * HEAD_DIM, (h + 1) * HEAD_DIM)
        qh = act[:, sl]
        kh = act[:, GROUP_W + h * HEAD_DIM:GROUP_W + (h + 1) * HEAD_DIM]
        qh = qh * lax.rsqrt(jnp.sum(qh * qh, axis=-1, keepdims=True) + EPS) * HEAD_DIM ** -0.5
        kh = kh * lax.rsqrt(jnp.sum(kh * kh, axis=-1, keepdims=True) + EPS)
        qs.append(qh)
        ks.append(kh)
        scal = put(scal, SC_BETA + h, _col(beta_all, LANE_BETA + h))
        scal = put(scal, SC_EG + h, _col(eg_all, LANE_A + h))
        scal = put(scal, SC_QKB + h, jnp.sum(qh * kh, axis=-1, keepdims=True))
        cols_ref[(0 * N_HEADS + h) * HEAD_DIM:(0 * N_HEADS + h + 1) * HEAD_DIM, :] = qh.T.astype(BF16)
        cols_ref[(1 * N_HEADS + h) * HEAD_DIM:(1 * N_HEADS + h + 1) * HEAD_DIM, :] = kh.T.astype(BF16)
    rows_b_ref[...] = jnp.concatenate(qs + ks + [act[:, 2 * GROUP_W:]], axis=1)

    qc = blk(COL_CQ)
    kc = blk(COL_CK) * HEAD_DIM ** -0.5
    for h in range(N_HEADS):
        sl = slice(h * HEAD_DIM, (h + 1) * HEAD_DIM)
        scal = put(scal, SC_LI + h, _col(li_all, LANE_I + h))
        scal = put(scal, SC_LF + h, _col(lf_all, LANE_F + h))
        scal = put(scal, SC_QKC + h, jnp.sum(qc[:, sl] * kc[:, sl], axis=-1, keepdims=True))
        cols_ref[(2 * N_HEADS + h) * HEAD_DIM:(2 * N_HEADS + h + 1) * HEAD_DIM, :] = qc[:, sl].T.astype(BF16)
        cols_ref[(3 * N_HEADS + h) * HEAD_DIM:(3 * N_HEADS + h + 1) * HEAD_DIM, :] = kc[:, sl].T.astype(BF16)
    rows_c_ref[...] = jnp.concatenate([qc, kc, blk(COL_CV)], axis=1)
    scal_ref[...] = scal


def sample_pre(P, pool_st, dnc_st, scc_st, pool_w, pool_scale, conv_w, a_log, dt_bias, b_i, b_f, sc_w,
               start_pos):
    nb = P.shape[0]
    lane_row = lambda v, lane: jnp.zeros((1, 128), F32).at[0, lane:lane + N_HEADS].set(v)
    W3 = 3 * GROUP_W
    out_shape = [jax.ShapeDtypeStruct((nb, GROUP_W), BF16),
                 jax.ShapeDtypeStruct((nb, GROUP_W), BF16),
                 jax.ShapeDtypeStruct(pool_st.shape, F32),
                 jax.ShapeDtypeStruct(dnc_st.shape, F32),
                 jax.ShapeDtypeStruct(scc_st.shape, F32),
                 jax.ShapeDtypeStruct((nb, W3), F32),
                 jax.ShapeDtypeStruct((nb, W3), F32),
                 jax.ShapeDtypeStruct((nb, 128), F32),
                 jax.ShapeDtypeStruct((4 * N_HEADS * HEAD_DIM, nb), BF16)]
    return pl.pallas_call(
        functools.partial(_sample_pre_kernel, start_pos=start_pos),
        out_shape=out_shape,
        compiler_params=pltpu.CompilerParams(vmem_limit_bytes=VMEM_LIMIT),
    )(P, pool_st, dnc_st, scc_st, pool_w, pool_scale.reshape(1, GROUP_W), conv_w,
      lane_row(a_log, LANE_A), lane_row(dt_bias, LANE_A), lane_row(b_i, LANE_I), lane_row(b_f, LANE_F), sc_w)


def _sample_rec_kernel(cols_ref, rows_b_ref, rows_c_ref, scal_ref, z_ref, og_ref, n_ref, m_ref,
                       dng_ref, mlg_ref, s_ref, c_ref, s_acc_ref, c_acc_ref,
                       yb_ref, yc_ref, s_out_ref, c_out_ref, n_out_ref, m_out_ref,
                       ob_ref, hc_ref, *, tb):
    del s_acc_ref, c_acc_ref
    i = pl.program_id(0)
    nb = cols_ref.shape[1]
    row_id = lax.broadcasted_iota(jnp.int32, (nb, 128), 0)
    lane_id = lax.broadcasted_iota(jnp.int32, (1, 128), 1)

    def body(j, carry):
        b = i * tb + j
        onehot = (row_id == b).astype(BF16)
        cols = jnp.dot(cols_ref[...], onehot, preferred_element_type=F32)
        scal = scal_ref[pl.ds(b, 1), :]
        sc = lambda idx: _col(scal, idx)
        m_row = m_ref[pl.ds(b, 1), :]
        rb = rows_b_ref[pl.ds(b, 1), :]
        rc = rows_c_ref[pl.ds(b, 1), :]
        n_all = n_ref[pl.ds(b, 1), :]
        m_new_row = jnp.zeros((1, 128), F32)
        o_rows, h_rows, n_rows = [], [], []
        for h in range(N_HEADS):
            sl = slice(h * HEAD_DIM, (h + 1) * HEAD_DIM)
            colblk = lambda v: cols[(v * N_HEADS + h) * HEAD_DIM:(v * N_HEADS + h + 1) * HEAD_DIM, :]
            s = s_ref[j, h]
            ks = jnp.sum(colblk(1) * s, axis=0, keepdims=True)
            qs = jnp.sum(colblk(0) * s, axis=0, keepdims=True)
            beta, eg, qk = sc(SC_BETA + h), sc(SC_EG + h), sc(SC_QKB + h)
            v_row = rb[:, 2 * GROUP_W + h * HEAD_DIM:2 * GROUP_W + (h + 1) * HEAD_DIM]
            v_new = beta * v_row - (beta * eg) * ks
            o_rows.append(eg * qs + qk * v_new)
            s_out_ref[j, h] = s * eg + colblk(1) * v_new
            cm = c_ref[j, h]
            qc = jnp.sum(colblk(2) * cm, axis=0, keepdims=True)
            q_row = rc[:, sl]
            k_row = rc[:, GROUP_W + h * HEAD_DIM:GROUP_W + (h + 1) * HEAD_DIM]
            vc_row = rc[:, 2 * GROUP_W + h * HEAD_DIM:2 * GROUP_W + (h + 1) * HEAD_DIM]
            n_row = n_all[:, sl]
            li, lf, qkc = sc(SC_LI + h), sc(SC_LF + h), sc(SC_QKC + h)
            m_old = _col(m_row, h)
            bb = lf + m_old
            m_t = jnp.maximum(bb, li)
            w_intra = jnp.exp(li - m_t) * qkc
            w_inter = jnp.exp(bb - m_t)
            num = w_inter * qc + w_intra * vc_row
            den = w_inter * jnp.sum(q_row * n_row, axis=-1, keepdims=True) + w_intra
            h_rows.append(num / jnp.maximum(jnp.abs(den), jnp.exp(-m_t)))
            m_new = jnp.maximum(m_old + lf, li)
            dec = jnp.exp(m_old + lf - m_new)
            e = jnp.exp(li - m_new)
            c_out_ref[j, h] = cm * dec + colblk(3) * (e * vc_row)
            n_rows.append(n_row * dec + e * k_row)
            m_new_row = jnp.where(lane_id == h, m_new, m_new_row)
        ob_ref[pl.ds(b, 1), :] = jnp.concatenate(o_rows, axis=1)
        hc_ref[pl.ds(b, 1), :] = jnp.concatenate(h_rows, axis=1)
        n_out_ref[pl.ds(b, 1), :] = jnp.concatenate(n_rows, axis=1)
        m_out_ref[pl.ds(b, 1), :] = m_new_row
        return carry

    lax.fori_loop(0, tb, body, 0, unroll=2)

    @pl.when(i == pl.num_programs(0) - 1)
    def _():
        ys_b, ys_c = [], []
        for h in range(N_HEADS):
            sl = slice(h * HEAD_DIM, (h + 1) * HEAD_DIM)
            o = ob_ref[:, sl]
            o = o * lax.rsqrt(jnp.mean(o * o, axis=-1, keepdims=True) + EPS) * dng_ref[...]
            ys_b.append(o * _silu(z_ref[:, sl]))
            hh = _sigmoid(og_ref[:, sl]) * hc_ref[:, sl]
            hh = hh * lax.rsqrt(jnp.mean(hh * hh, axis=-1, keepdims=True) + EPS) * mlg_ref[h:h + 1, :]
            ys_c.append(hh)
        yb_ref[...] = jnp.concatenate(ys_b, axis=1).astype(yb_ref.dtype)
        yc_ref[...] = jnp.concatenate(ys_c, axis=1).astype(yc_ref.dtype)


def sample_rec(cols, rows_b, rows_c, scal, P, n_st, m_st, dn_norm_g, ml_norm_g, s_all, c_all, s_acc, c_acc,
               layer, *, tb=8):
    nb = P.shape[0]
    W3 = 3 * GROUP_W
    full = lambda shape: pl.BlockSpec(shape, lambda i: (0,) * len(shape))
    state = pl.BlockSpec((None, tb, N_HEADS, HEAD_DIM, HEAD_DIM), lambda i: (layer, i, 0, 0, 0))
    untouched = pl.BlockSpec(memory_space=pl.ANY)
    return pl.pallas_call(
        functools.partial(_sample_rec_kernel, tb=tb),
        grid=(nb // tb,),
        in_specs=[full(cols.shape), full((nb, W3)), full((nb, W3)), full((nb, 128)),
                  pl.BlockSpec((nb, GROUP_W), lambda i: (0, COL_BZ)),
                  pl.BlockSpec((nb, GROUP_W), lambda i: (0, COL_CO)),
                  full((nb, GROUP_W)), full((nb, 128)), full((1, HEAD_DIM)), full((N_HEADS, HEAD_DIM)),
                  state, state, untouched, untouched],
        out_specs=[full((nb, GROUP_W)), full((nb, GROUP_W)), state, state,
                   full((nb, GROUP_W)), full((nb, 128))],
        out_shape=[jax.ShapeDtypeStruct((nb, GROUP_W), BF16),
                   jax.ShapeDtypeStruct((nb, GROUP_W), BF16),
                   jax.ShapeDtypeStruct(s_all.shape, F32),
                   jax.ShapeDtypeStruct(c_all.shape, F32),
                   jax.ShapeDtypeStruct((nb, GROUP_W), F32),
                   jax.ShapeDtypeStruct((nb, 128), F32)],
        input_output_aliases={12: 2, 13: 3},
        scratch_shapes=[pltpu.VMEM((nb, GROUP_W), F32), pltpu.VMEM((nb, GROUP_W), F32)],
        compiler_params=_cparams(("arbitrary",), VMEM_LIMIT),
    )(cols, rows_b, rows_c, scal, P, P, n_st, m_st, dn_norm_g.reshape(1, HEAD_DIM), ml_norm_g, s_all, c_all,
      s_acc, c_acc)


def _sample_xattn_kernel(q_ref, k_ref, v_ref, o_ref, *, tb):
    i = pl.program_id(0)
    n_rows = k_ref.shape[1]
    n_rep = n_rows // (2 * N_HEADS)
    ones = jnp.ones((HEAD_DIM, HEAD_DIM), BF16)
    fold = lambda x: x + pltpu.roll(x, N_HEADS, 0)

    def body(j, carry):
        b = i * tb + j
        q_row = q_ref[pl.ds(b, 1), :]
        heads = [q_row[:, h * HEAD_DIM:(h + 1) * HEAD_DIM] for h in range(N_HEADS)]
        q8 = jnp.concatenate(heads + heads, axis=0)
        k3 = k_ref[j].reshape(n_rep, 2 * N_HEADS, HEAD_DIM)
        prod = (k3 * q8[None]).reshape(n_rows, HEAD_DIM)
        ph, plo = _split(prod)
        s = (jnp.dot(ph, ones, preferred_element_type=F32)
             + jnp.dot(plo, ones, preferred_element_type=F32)) * HEAD_DIM ** -0.5
        s3 = s.reshape(n_rep, 2 * N_HEADS, HEAD_DIM)
        mx = jnp.max(s3, axis=0)
        mx = jnp.maximum(mx, pltpu.roll(mx, N_HEADS, 0))
        e3 = jnp.exp(s3 - mx[None])
        den = fold(jnp.sum(e3, axis=0))
        v3 = v_ref[j].reshape(n_rep, 2 * N_HEADS, HEAD_DIM)
        o8 = fold(jnp.sum(e3 * v3, axis=0)) / den
        o_ref[pl.ds(b, 1), :] = jnp.concatenate([o8[h:h + 1, :] for h in range(N_HEADS)], axis=1)
        return carry

    lax.fori_loop(0, tb, body, 0, unroll=4)


def sample_xattn(q, k_all, v_all, layer, *, tb=8):
    nb = q.shape[0]
    n_rows = k_all.shape[2]
    kv = pl.BlockSpec((None, tb, n_rows, HEAD_DIM), lambda i: (layer, i, 0, 0))
    return pl.pallas_call(
        functools.partial(_sample_xattn_kernel, tb=tb),
        grid=(nb // tb,),
        in_specs=[pl.BlockSpec((nb, GROUP_W), lambda i: (0, 0)), kv, kv],
        out_specs=pl.BlockSpec((nb, GROUP_W), lambda i: (0, 0)),
        out_shape=jax.ShapeDtypeStruct((nb, GROUP_W), F32),
        compiler_params=_cparams(("arbitrary",), VMEM_LIMIT),
    )(q, k_all, v_all)


W_B0 = 5 * GROUP_W
W_C0 = W_B0 + 2 * N_HEADS
W_C1 = W_C0 + 4 * GROUP_W
W_D0 = W_C1 + 2 * N_HEADS
W_D1 = W_D0 + 3 * GROUP_W


def _w_in_prep_kernel(w_ref, o_ref):
    tk = w_ref.shape[1]
    o_ref[:W_B0, :] = w_ref[:W_B0, :].astype(BF16)
    o_ref[W_B0:W_B0 + (W_C1 - W_C0), :] = w_ref[W_C0:W_C1, :].astype(BF16)
    o_ref[W_B0 + (W_C1 - W_C0):GATE_COL0, :] = w_ref[W_D0:W_D1, :].astype(BF16)
    gates = jnp.concatenate([w_ref[W_B0:W_C0, :], w_ref[W_C1:W_D0, :],
                             jnp.zeros((128 - 4 * N_HEADS, tk), F32)], axis=0)
    o_ref[GATE_COL0:, :] = gates.astype(BF16)


def _prep_w_in(w_in, *, tk=256):
    w_t = jnp.swapaxes(w_in, 1, 2)
    depth, n_in, K = w_t.shape
    assert n_in == W_D1
    return pl.pallas_call(
        _w_in_prep_kernel,
        grid=(depth, K // tk),
        in_specs=[pl.BlockSpec((None, n_in, tk), lambda l, i: (l, 0, i))],
        out_specs=pl.BlockSpec((None, IN_W_PAD, tk), lambda l, i: (l, 0, i)),
        out_shape=jax.ShapeDtypeStruct((depth, IN_W_PAD, K), BF16),
        compiler_params=_cparams(("parallel", "parallel"), VMEM_LIMIT),
    )(w_t)


def _ffn(x, g_pre, g_post, wg, wu, wd, layer, *, tm_up, tm_down):
    act = norm_matmul(x, g_pre, [wg, wu], layer, tm=tm_up, tn=FFN_TN, out_dtype=BF16)
    return matmul_resnorm([act], wd, layer, g_post, x, scale=0.5, tm=tm_down)


def _prompt_layer(x, mem2d, B, T, lw, layer):
    g = lw['norm_g']
    x = _ffn(x, g[0], g[1], lw['ffn1_wg'], lw['ffn1_wu'], lw['ffn1_wd'], layer, tm_up=1024, tm_down=256)
    P = norm_matmul(x, g[2], [lw['w_in']], layer, tm=1024, tn=IN_TN, out_dtype=F32, w_is_nk=True)
    ya, yd, yb, yc, pool_tail, sc_tail, dn_s, dn_tail, ml_c, ml_nm = mixers_prompt(P, B, T, lw)
    x = matmul_resnorm([ya, yb, yc, yd], lw['w_out'], layer, g[3], x, scale=1.0, tm=512)
    kv = norm_matmul(mem2d, g[8], [lw['x_wkv']], layer, tm=1024, tn=512, out_dtype=F32)
    x = xattn_prompt(x, B, T, g[4], lw['x_wq'], kv, lw['x_wo'], layer, g[5])
    x = _ffn(x, g[6], g[7], lw['ffn2_wg'], lw['ffn2_wu'], lw['ffn2_wd'], layer, tm_up=1024, tm_down=256)
    states = (pool_tail[:, 16 - POOL_STATE:], dn_tail[:, 8 - (DN_CONV - 1):], dn_s, ml_c,
              ml_nm[:, :N_HEADS], ml_nm[:, N_HEADS:, 0], sc_tail[:, 8 - (SC_WIDTH - 1):])
    mem_k = kv[:, :GROUP_W].reshape(B, MEM_LEN, N_HEADS, HEAD_DIM)
    mem_v = kv[:, GROUP_W:].reshape(B, MEM_LEN, N_HEADS, HEAD_DIM)
    return x, states, mem_k, mem_v


def _sample_layer(x, st, big, acc, layer, lw, start_pos):
    pool_st, dnc_st, ml_n, ml_m, scc_st = st
    s_all, c_all, k_all, v_all = big
    s_acc, c_acc = acc
    nb = x.shape[0]
    g = lw['norm_g']
    x = _ffn(x, g[0], g[1], lw['ffn1_wg'], lw['ffn1_wu'], lw['ffn1_wd'], layer, tm_up=128, tm_down=128)
    P = norm_matmul(x, g[2], [lw['w_in']], layer, tm=128, tn=IN_TN, out_dtype=F32, w_is_nk=True)
    (ya, yd, pool_new, dnc_new, scc_new, rows_b, rows_c, scal, cols) = sample_pre(
        P, pool_st.reshape(nb, -1), dnc_st.reshape(nb, -1), scc_st.reshape(nb, -1),
        lw['pool_w'], lw['pool_scale'], lw['dn_conv_w'], lw['dn_A_log'], lw['dn_dt_bias'],
        lw['ml_b_i'], lw['ml_b_f'], lw['sc_conv_w'], start_pos)
    m_pad = jnp.pad(ml_m, ((0, 0), (0, 128 - N_HEADS)))
    yb, yc, s_acc, c_acc, n_new, m_new = sample_rec(
        cols, rows_b, rows_c, scal, P, ml_n.reshape(nb, GROUP_W), m_pad,
        lw['dn_norm_g'], lw['ml_norm_g'], s_all, c_all, s_acc, c_acc, layer)
    x = matmul_resnorm([ya, yb, yc, yd], lw['w_out'], layer, g[3], x, scale=1.0, tm=128)
    q = norm_matmul(x, g[4], [lw['x_wq']], layer, tm=128, tn=GROUP_W, out_dtype=F32)
    o = sample_xattn(q, k_all, v_all, layer)
    x = matmul_resnorm([o], lw['x_wo'], layer, g[5], x, scale=1.0, tm=128)
    x = _ffn(x, g[6], g[7], lw['ffn2_wg'], lw['ffn2_wu'], lw['ffn2_wd'], layer, tm_up=128, tm_down=128)
    states = (pool_new.reshape(pool_st.shape), dnc_new.reshape(dnc_st.shape),
              n_new.reshape(ml_n.shape), m_new[:, :N_HEADS], scc_new.reshape(scc_st.shape))
    return x, states, (s_acc, c_acc)


def kernel(x_prompt, x_sample, mem_prompt, state_pool, state_dn_conv, state_dn_S, state_ml_C, state_ml_n,
           state_ml_m, state_sc_conv, cache_mem_k, cache_mem_v, norm_g, w_in, w_out, pool_w, pool_scale,
           dn_conv_w, dn_A_log, dn_dt_bias, dn_norm_g, ml_b_i, ml_b_f, ml_norm_g, sc_conv_w,
           x_wq, x_wk, x_wv, x_wo, ffn1_wg, ffn1_wu, ffn1_wd, ffn2_wg, ffn2_wu, ffn2_wd):
    depth = norm_g.shape[0]
    B, T, _ = x_prompt.shape
    nb, t_dec, _ = x_sample.shape
    assert t_dec == 1
    start_pos = PAST_LEN

    stacked = dict(w_in=_prep_w_in(w_in), w_out=w_out.astype(BF16), x_wq=x_wq.astype(BF16),
                   x_wkv=jnp.concatenate([x_wk.astype(BF16), x_wv.astype(BF16)], axis=-1),
                   x_wo=x_wo.astype(BF16),
                   ffn1_wg=ffn1_wg.astype(BF16), ffn1_wu=ffn1_wu.astype(BF16), ffn1_wd=ffn1_wd.astype(BF16),
                   ffn2_wg=ffn2_wg.astype(BF16), ffn2_wu=ffn2_wu.astype(BF16), ffn2_wd=ffn2_wd.astype(BF16))

    def layer_weights(l):
        return dict(stacked, norm_g=norm_g[l], pool_w=pool_w[l],
                    pool_scale=pool_scale[l], dn_conv_w=dn_conv_w[l], dn_A_log=dn_A_log[l],
                    dn_dt_bias=dn_dt_bias[l], dn_norm_g=dn_norm_g[l], ml_b_i=ml_b_i[l], ml_b_f=ml_b_f[l],
                    ml_norm_g=ml_norm_g[l], sc_conv_w=sc_conv_w[l])

    mem2d = mem_prompt.reshape(B * MEM_LEN, D_MODEL)
    h = x_prompt.reshape(B * T, D_MODEL)
    p_states, mem_k_list, mem_v_list = [], [], []
    for l in range(depth):
        h, ns, mk, mv = _prompt_layer(h, mem2d, B, T, layer_weights(l), l)
        p_states.append(ns)
        mem_k_list.append(mk)
        mem_v_list.append(mv)
    y_prompt = h.reshape(B, T, D_MODEL)

    s_inputs = (state_pool, state_dn_conv, state_ml_n, state_ml_m, state_sc_conv)
    big = (state_dn_S, state_ml_C,
           cache_mem_k.reshape(depth, nb, MEM_LEN * N_HEADS, HEAD_DIM),
           cache_mem_v.reshape(depth, nb, MEM_LEN * N_HEADS, HEAD_DIM))
    h = x_sample.reshape(nb, D_MODEL)
    s_states = []
    acc = (lax.empty(state_dn_S.shape, F32), lax.empty(state_ml_C.shape, F32))
    for l in range(depth):
        st = tuple(s[l] for s in s_inputs)
        h, ns, acc = _sample_layer(h, st, big, acc, l, layer_weights(l), start_pos)
        s_states.append(ns)
    y_sample = h.reshape(nb, 1, D_MODEL)
    dn_S_s, ml_C_s = acc

    pool_p, dn_conv_p, dn_S_p, ml_C_p, ml_n_p, ml_m_p, sc_conv_p = [jnp.stack(z) for z in zip(*p_states)]
    pool_s, dn_conv_s, ml_n_s, ml_m_s, sc_conv_s = [jnp.stack(z) for z in zip(*s_states)]
    mem_k_p = jnp.stack(mem_k_list)
    mem_v_p = jnp.stack(mem_v_list)
    return (y_prompt, y_sample, pool_p, pool_s, dn_conv_p, dn_conv_s, dn_S_p, dn_S_s, ml_C_p, ml_C_s,
            ml_n_p, ml_n_s, ml_m_p, ml_m_s, sc_conv_p, sc_conv_s, mem_k_p, mem_v_p)
```

```python
import functools

import numpy as np
import jax
import jax.numpy as jnp
from jax import lax
from jax.experimental import pallas as pl
from jax.experimental.pallas import tpu as pltpu

F32 = jnp.float32
BF16 = jnp.bfloat16

EPS = 1e-6
D_MODEL = 2048
GROUP_W = 512
HEAD_DIM = 128
N_HEADS = GROUP_W // HEAD_DIM
CHUNK = 64
POOL_WINDOWS = (2, 4, 8, 16)
POOL_STATE = 15
DN_CONV = 4
SC_WIDTH = 3
MEM_LEN = 256
PAST_LEN = 16384
FFN_TN = 512
COL_A, COL_BQ, COL_BK, COL_BV, COL_BZ = 0, 1, 2, 3, 4
COL_CQ, COL_CK, COL_CV, COL_CO = 5, 6, 7, 8
COL_DB, COL_DC, COL_DH = 9, 10, 11
GATE_COL0 = 12 * GROUP_W
IN_W_PAD = GATE_COL0 + 128
IN_TN = 1280
LANE_BETA, LANE_A, LANE_I, LANE_F = 0, 4, 8, 12
TB = 256
SUB = 128
VMEM_LIMIT = 60 * 2**20


def _cparams(sem, vmem=None):
    return pltpu.CompilerParams(dimension_semantics=sem, vmem_limit_bytes=vmem)


def _bdot(a, b):
    return jnp.dot(a.astype(BF16), b.astype(BF16), preferred_element_type=F32)


def _bdot_nt(a, b):
    return lax.dot_general(a.astype(BF16), b.astype(BF16), (((1,), (1,)), ((), ())),
                           preferred_element_type=F32)


def _sigmoid(x):
    return 0.5 * jnp.tanh(0.5 * x) + 0.5


def _silu(x):
    return x * _sigmoid(x)


def _softplus(x):
    return jnp.maximum(x, 0.0) + jnp.log(1.0 + jnp.exp(-jnp.abs(x)))


def _col(x, idx):
    return x[:, idx:idx + 1]


def _resident(shape):
    nd = len(shape)
    return pl.BlockSpec(shape, lambda *_: (0,) * nd, pipeline_mode=pl.Buffered(1))


def _norm_into(x_ref, g_ref, h_ref):
    tm = x_ref.shape[0]
    rs = min(tm, 128)

    def body(i, carry):
        r = pl.multiple_of(i * rs, rs)
        x = x_ref[pl.ds(r, rs), :]
        ms = jnp.mean(x * x, axis=-1, keepdims=True)
        h_ref[pl.ds(r, rs), :] = (x * lax.rsqrt(ms + EPS) * g_ref[...]).astype(BF16)
        return carry

    lax.fori_loop(0, tm // rs, body, 0)


def _zero_past_width(y, n_valid):
    tn = y.shape[1]
    if n_valid % tn == 0:
        return y
    col = pl.program_id(1) * tn + lax.broadcasted_iota(jnp.int32, (1, tn), 1)
    return jnp.where(col < n_valid, y, 0.0)


def _norm_mm_kernel(x_ref, g_ref, w_ref, o_ref, h_ref, *, n_valid, w_is_nk):
    @pl.when(pl.program_id(1) == 0)
    def _():
        _norm_into(x_ref, g_ref, h_ref)

    if w_is_nk:
        y = lax.dot_general(h_ref[...], w_ref[...], (((1,), (1,)), ((), ())), preferred_element_type=F32)
    else:
        y = jnp.dot(h_ref[...], w_ref[...], preferred_element_type=F32)
    o_ref[...] = _zero_past_width(y, n_valid).astype(o_ref.dtype)


def _norm_swiglu_kernel(x_ref, g_ref, wg_ref, wu_ref, o_ref, h_ref, *, n_valid):
    @pl.when(pl.program_id(1) == 0)
    def _():
        _norm_into(x_ref, g_ref, h_ref)

    h = h_ref[...]
    a = jnp.dot(h, wg_ref[...], preferred_element_type=F32)
    b = jnp.dot(h, wu_ref[...], preferred_element_type=F32)
    o_ref[...] = _zero_past_width(_silu(a) * b, n_valid).astype(o_ref.dtype)


def norm_matmul(x, g, ws, layer, *, tm, tn, out_dtype, w_is_nk=False):
    M, K = x.shape
    n_valid = ws[0].shape[1 if w_is_nk else 2]
    n_steps = pl.cdiv(n_valid, tn)
    N = n_steps * tn
    tm = min(tm, M)
    if len(ws) == 2:
        assert not w_is_nk
        kern = functools.partial(_norm_swiglu_kernel, n_valid=n_valid)
    else:
        kern = functools.partial(_norm_mm_kernel, n_valid=n_valid, w_is_nk=w_is_nk)
    w_spec = (pl.BlockSpec((None, tn, K), lambda i, j: (layer, j, 0)) if w_is_nk
              else pl.BlockSpec((None, K, tn), lambda i, j: (layer, 0, j)))
    return pl.pallas_call(
        kern,
        grid=(M // tm, n_steps),
        in_specs=[pl.BlockSpec((tm, K), lambda i, j: (i, 0)),
                  pl.BlockSpec((1, K), lambda i, j: (0, 0))]
        + [w_spec for _ in ws],
        out_specs=pl.BlockSpec((tm, tn), lambda i, j: (i, j)),
        out_shape=jax.ShapeDtypeStruct((M, N), out_dtype),
        scratch_shapes=[pltpu.VMEM((tm, K), BF16)],
        compiler_params=_cparams(("parallel", "arbitrary"), VMEM_LIMIT),
    )(x, g.reshape(1, K), *ws)


def _mm_resnorm_kernel(*refs, n_a, scale):
    a_refs = refs[:n_a]
    w_ref, g_ref, res_ref, o_ref = refs[n_a:]
    parts = [r[...].astype(BF16) for r in a_refs]
    a = parts[0] if n_a == 1 else jnp.concatenate(parts, axis=1)
    y = jnp.dot(a, w_ref[...], preferred_element_type=F32)
    ms = jnp.mean(y * y, axis=-1, keepdims=True)
    o_ref[...] = res_ref[...] + scale * (y * lax.rsqrt(ms + EPS) * g_ref[...])


def matmul_resnorm(a_list, w, layer, g, res, *, scale, tm):
    M, N = res.shape
    tm = min(tm, M)
    K = w.shape[1]
    widths = [K] if len(a_list) == 1 else [a.shape[1] for a in a_list]
    assert sum(widths) == K
    w_spec = pl.BlockSpec((None, K, N), lambda i: (layer, 0, 0), pipeline_mode=pl.Buffered(1))
    return pl.pallas_call(
        functools.partial(_mm_resnorm_kernel, n_a=len(a_list), scale=scale),
        grid=(M // tm,),
        in_specs=[pl.BlockSpec((tm, wd), lambda i: (i, 0)) for wd in widths]
        + [w_spec, _resident((1, N)), pl.BlockSpec((tm, N), lambda i: (i, 0))],
        out_specs=pl.BlockSpec((tm, N), lambda i: (i, 0)),
        out_shape=jax.ShapeDtypeStruct((M, N), F32),
        compiler_params=_cparams(("parallel",), VMEM_LIMIT),
    )(*a_list, w, g.reshape(1, N), res)


def _chunk_cumsum(x):
    n = x.shape[0]
    row = lax.broadcasted_iota(jnp.int32, (n, 1), 0) % CHUNK
    s = 1
    while s < CHUNK:
        x = x + jnp.where(row >= s, pltpu.roll(x, s, 0), 0.0)
        s *= 2
    return x


def _chunk_last(x):
    n = x.shape[0]
    parts = [jnp.broadcast_to(x[c * CHUNK + CHUNK - 1:c * CHUNK + CHUNK, :], (CHUNK, x.shape[1]))
             for c in range(n // CHUNK)]
    return jnp.concatenate(parts, axis=0)


def _pad_rows(x, c, n_chunks):
    z = jnp.zeros_like(x)
    return jnp.concatenate([x if i == c else z for i in range(n_chunks)], axis=0)


def _blk_masks(n):
    r = lax.broadcasted_iota(jnp.int32, (n, n), 0)
    c = lax.broadcasted_iota(jnp.int32, (n, n), 1)
    same = (r ^ c) < CHUNK
    return same & (c <= r), same & (c < r), r == c


def _split(a):
    ah = a.astype(BF16)
    return ah, (a - ah.astype(F32)).astype(BF16)


def _split_dot(a, b):
    d = lambda x, y: jnp.dot(x, y, preferred_element_type=F32)
    return d(a[0], b[0]) + d(a[0], b[1]) + d(a[1], b[0])


def _unit_lower_inverses(a_list, n):
    r = lax.broadcasted_iota(jnp.int32, (n, n), 0)
    c = lax.broadcasted_iota(jnp.int32, (n, n), 1)
    rc = r ^ c
    pair = lambda s: (rc >= s) & (rc < 2 * s)
    eye = (r == c).astype(F32)
    xs = [eye - jnp.where(pair(1), a, 0.0) for a in a_list]
    s = 2
    while s < CHUNK:
        m = pair(s)
        x_sp = [_split(x) for x in xs]
        ts = [_split_dot(_split(jnp.where(m, a, 0.0)), x) for a, x in zip(a_list, x_sp)]
        xs = [x - _split_dot(xp, _split(t)) for x, xp, t in zip(xs, x_sp, ts)]
        s *= 2
    return xs


def _pool_sconv_step(ua_ref, db_ref, dc_ref, dh_ref, pw_ref, ps_ref, sw_ref, ya_ref, yd_ref, hist_a, hist_d,
                     *, tb):
    i = pl.program_id(1)

    u = ua_ref[...]
    ext = jnp.concatenate([hist_a[...], u], axis=0)
    a2 = ext + pltpu.roll(ext, 1, 0)
    a4 = a2[:, 128:] + pltpu.roll(a2[:, 128:], 2, 0)
    a8 = a4[:, 128:] + pltpu.roll(a4[:, 128:], 4, 0)
    a16 = a8[:, 128:] + pltpu.roll(a8[:, 128:], 8, 0)
    sums = (a2[16:, :128], a4[16:, :128], a8[16:, :128], a16[16:, :])
    pos = i * tb + lax.broadcasted_iota(jnp.int32, (tb, 1), 0)
    ys = []
    for g, w in enumerate(POOL_WINDOWS):
        cnt = jnp.minimum(pos + 1, w).astype(F32)
        d = sums[g] / cnt - u[:, g * 128:(g + 1) * 128]
        ys.append(_bdot(d, pw_ref[g]))
    ya_ref[...] = (jnp.concatenate(ys, axis=1) * ps_ref[...]).astype(ya_ref.dtype)
    hist_a[...] = u[tb - 16:, :]

    us = dc_ref[...] * dh_ref[...]
    ext = jnp.concatenate([hist_d[...], us], axis=0)
    sw = sw_ref[...]
    y = sw[0:1] * pltpu.roll(ext, 2, 0) + sw[1:2] * pltpu.roll(ext, 1, 0) + sw[2:3] * ext
    yd_ref[...] = (db_ref[...] * y[8:]).astype(yd_ref.dtype)
    hist_d[...] = us[tb - 8:, :]
    return u[tb - 16:, :], us[tb - 8:, :]


def _deltanet_step(q_ref, k_ref, v_ref, z_ref, gt_ref, cw_ref, alog_ref, dtb_ref, ng_ref, y_ref, s_ref, hist_ref,
                   *, tb):
    nc = tb // CHUNK

    qkv = jnp.concatenate([q_ref[...], k_ref[...], v_ref[...]], axis=1)
    ext = jnp.concatenate([hist_ref[...], qkv], axis=0)
    cw = cw_ref[...]
    conv = (cw[0:1] * pltpu.roll(ext, 3, 0) + cw[1:2] * pltpu.roll(ext, 2, 0)
            + cw[2:3] * pltpu.roll(ext, 1, 0) + cw[3:4] * ext)[8:]
    act = _silu(conv)
    hist_ref[...] = qkv[tb - 8:, :]

    gates = gt_ref[...]
    beta_all = _sigmoid(gates)
    g_all = -jnp.exp(alog_ref[...]) * _softplus(gates + dtb_ref[...])
    gcum_all = _chunk_cumsum(g_all)
    glast_all = _chunk_last(gcum_all)
    gcum_t = gcum_all.T
    causal, strict, _ = _blk_masks(SUB)

    heads = range(N_HEADS)
    subs = range(tb // SUB)
    cps = SUB // CHUNK
    hd = lambda base, h: slice(base + h * HEAD_DIM, base + (h + 1) * HEAD_DIM)
    qs, ks, gcs, gls, betas = [], [], [], [], []
    for h in heads:
        qh = act[:, hd(0, h)]
        kh = act[:, hd(GROUP_W, h)]
        qs.append(qh * lax.rsqrt(jnp.sum(qh * qh, axis=-1, keepdims=True) + EPS) * HEAD_DIM ** -0.5)
        ks.append(kh * lax.rsqrt(jnp.sum(kh * kh, axis=-1, keepdims=True) + EPS))
        betas.append(_col(beta_all, LANE_BETA + h))
        gcs.append(_col(gcum_all, LANE_A + h))
        gls.append(_col(glast_all, LANE_A + h))
    k_ts, a_mats, qks = {}, [], {}
    for st in subs:
        R = slice(st * SUB, (st + 1) * SUB)
        for h in heads:
            gr = gcum_t[LANE_A + h:LANE_A + h + 1, R]
            decay = jnp.where(causal, jnp.exp(jnp.where(causal, gcs[h][R] - gr, 0.0)), 0.0)
            k_t = ks[h][R].T.astype(BF16)
            a_mats.append(jnp.where(strict, _bdot(ks[h][R] * betas[h][R], k_t) * decay, 0.0))
            qks[st, h] = jnp.where(causal, _bdot(qs[h][R], k_t) * decay, 0.0).astype(BF16)
            k_ts[st, h] = k_t
    tinvs = _unit_lower_inverses(a_mats, SUB)
    us, ws = {}, {}
    for st in subs:
        R = slice(st * SUB, (st + 1) * SUB)
        for h in heads:
            vb = act[R, hd(2 * GROUP_W, h)] * betas[h][R]
            kbe = ks[h][R] * (betas[h][R] * jnp.exp(gcs[h][R]))
            uw = _bdot(tinvs[st * N_HEADS + h], jnp.concatenate([vb, kbe], axis=1))
            us[st, h] = uw[:, :HEAD_DIM]
            ws[st, h] = uw[:, HEAD_DIM:].astype(BF16)
    q_decs = [(qs[h] * jnp.exp(gcs[h])).astype(BF16) for h in heads]
    tails = [jnp.exp(gls[h] - gcs[h]) for h in heads]
    s_hs = [s_ref[h] for h in heads]
    outs = [[] for _ in heads]
    for c in range(nc):
        r = slice(c * CHUNK, (c + 1) * CHUNK)
        st, cl = divmod(c, cps)
        rl = slice(cl * CHUNK, (cl + 1) * CHUNK)
        for h in heads:
            s_b = s_hs[h].astype(BF16)
            v_new = us[st, h][rl] - jnp.dot(ws[st, h][rl], s_b, preferred_element_type=F32)
            outs[h].append(jnp.dot(q_decs[h][r], s_b, preferred_element_type=F32)
                           + _bdot(qks[st, h][rl], _pad_rows(v_new, cl, cps)))
            s_dec = jnp.exp(jnp.broadcast_to(gls[h][c * CHUNK:c * CHUNK + 1], (HEAD_DIM, 1)))
            s_hs[h] = s_hs[h] * s_dec + _bdot(k_ts[st, h], _pad_rows(v_new * tails[h][r], cl, cps))
    ys = []
    for h in heads:
        s_ref[h] = s_hs[h]
        o = jnp.concatenate(outs[h], axis=0)
        o = o * lax.rsqrt(jnp.mean(o * o, axis=-1, keepdims=True) + EPS) * ng_ref[...]
        ys.append(o * _silu(z_ref[:, hd(0, h)]))
    y_ref[...] = jnp.concatenate(ys, axis=1).astype(y_ref.dtype)
    return qkv[tb - 8:, :]


def _mlstm_step(q_ref, k_ref, v_ref, og_ref, gt_ref, bi_ref, bf_ref, ng_ref, y_ref, c_ref, nm_ref, *, tb):
    nc = tb // CHUNK

    gates = gt_ref[...]
    li_all = gates + bi_ref[...]
    lf_all = -_softplus(-(gates + bf_ref[...]))
    f_all = _chunk_cumsum(lf_all)
    flast_all = _chunk_last(f_all)
    f_t = f_all.T
    li_t = li_all.T
    causal, _, _ = _blk_masks(tb)

    heads = range(N_HEADS)
    hd = lambda h: slice(h * HEAD_DIM, (h + 1) * HEAD_DIM)
    qs, kss, k_ts, evs, eks, intras, den_intras, w_inters, inv_floor, decs = ([] for _ in range(10))
    for h in heads:
        qh = q_ref[:, hd(h)]
        kh = k_ref[:, hd(h)] * HEAD_DIM ** -0.5
        vh = v_ref[:, hd(h)]
        fc = _col(f_all, LANE_F + h)
        fl = _col(flast_all, LANE_F + h)
        a_w = fl - fc + _col(li_all, LANE_I + h)
        fr = f_t[LANE_F + h:LANE_F + h + 1, :]
        lir = li_t[LANE_I + h:LANE_I + h + 1, :]
        dm = jnp.where(causal, fc - fr + lir, -jnp.inf)
        mx = jnp.max(dm, axis=1, keepdims=True)
        m_h = nm_ref[N_HEADS + h:N_HEADS + h + 1, 0:1]
        m_prev, m_next, dec_h = [], [], []
        for c in range(nc):
            r = slice(c * CHUNK, (c + 1) * CHUNK)
            fl_c = fl[c * CHUNK:c * CHUNK + 1]
            m_new = jnp.maximum(m_h + fl_c, jnp.max(a_w[r], axis=0, keepdims=True))
            dec_h.append(jnp.exp(m_h + fl_c - m_new))
            m_prev.append(jnp.broadcast_to(m_h, (CHUNK, 1)))
            m_next.append(jnp.broadcast_to(m_new, (CHUNK, 1)))
            m_h = m_new
        nm_ref[N_HEADS + h:N_HEADS + h + 1, :] = jnp.broadcast_to(m_h, (1, HEAD_DIM))
        b = fc + jnp.concatenate(m_prev, axis=0)
        m_t = jnp.maximum(b, mx)
        k_t = kh.T.astype(BF16)
        w_intra = jnp.exp(dm - m_t) * _bdot(qh, k_t)
        e = jnp.exp(a_w - jnp.concatenate(m_next, axis=0))
        qs.append(qh)
        k_ts.append(k_t)
        evs.append(e * vh)
        eks.append(e * kh)
        intras.append(_bdot(w_intra, vh))
        den_intras.append(jnp.sum(w_intra, axis=-1, keepdims=True))
        w_inters.append(jnp.exp(b - m_t))
        inv_floor.append(jnp.exp(-m_t))
        decs.append(dec_h)
    c_hs = [c_ref[h] for h in heads]
    n_hs = [nm_ref[h:h + 1, :] for h in heads]
    outs = [[] for _ in heads]
    for c in range(nc):
        r = slice(c * CHUNK, (c + 1) * CHUNK)
        for h in heads:
            q_c = qs[h][r]
            num = w_inters[h][r] * _bdot(q_c, c_hs[h]) + intras[h][r]
            den = w_inters[h][r] * jnp.sum(q_c * n_hs[h], axis=-1, keepdims=True) + den_intras[h][r]
            outs[h].append(num / jnp.maximum(jnp.abs(den), inv_floor[h][r]))
            c_hs[h] = c_hs[h] * decs[h][c] + _bdot(k_ts[h], _pad_rows(evs[h][r], c, nc))
            n_hs[h] = n_hs[h] * decs[h][c] + jnp.sum(eks[h][r], axis=0, keepdims=True)
    ys = []
    for h in heads:
        c_ref[h] = c_hs[h]
        nm_ref[h:h + 1, :] = n_hs[h]
        hh = jnp.concatenate(outs[h], axis=0)
        hh = _sigmoid(og_ref[:, hd(h)]) * hh
        hh = hh * lax.rsqrt(jnp.mean(hh * hh, axis=-1, keepdims=True) + EPS) * ng_ref[h:h + 1, :]
        ys.append(hh)
    y_ref[...] = jnp.concatenate(ys, axis=1).astype(y_ref.dtype)


N_POOL_IN, N_DN_IN, N_ML_IN = 7, 9, 8


def _mixers_kernel(*refs, tb):
    ins = iter(refs[:N_POOL_IN + N_DN_IN + N_ML_IN])
    take = lambda n: [next(ins) for _ in range(n)]
    pool_in, dn_in, ml_in = take(N_POOL_IN), take(N_DN_IN), take(N_ML_IN)
    (ya_ref, yd_ref, yb_ref, yc_ref, ptail_ref, stail_ref, s_out_ref, ctail_ref, c_out_ref, nm_out_ref,
     hist_a, hist_d, s_ref, hist_b, c_ref, nm_ref) = refs[N_POOL_IN + N_DN_IN + N_ML_IN:]
    i = pl.program_id(1)

    @pl.when(i == 0)
    def _():
        for r in (hist_a, hist_d, s_ref, hist_b, c_ref, nm_ref):
            r[...] = jnp.zeros_like(r)

    qkv_tail = _deltanet_step(*dn_in, yb_ref, s_ref, hist_b, tb=tb)
    _mlstm_step(*ml_in, yc_ref, c_ref, nm_ref, tb=tb)
    u_tail, us_tail = _pool_sconv_step(*pool_in, ya_ref, yd_ref, hist_a, hist_d, tb=tb)

    @pl.when(i == pl.num_programs(1) - 1)
    def _():
        ptail_ref[0] = u_tail
        stail_ref[0] = us_tail
        s_out_ref[0] = s_ref[...]
        ctail_ref[0] = qkv_tail
        c_out_ref[0] = c_ref[...]
        nm_out_ref[0] = nm_ref[...]


def mixers_prompt(P, B, T, lw):
    nT = T // TB
    M = B * T
    col = lambda c: pl.BlockSpec((TB, GROUP_W), lambda b, i: (b * nT + i, c))
    gates = pl.BlockSpec((TB, 128), lambda b, i: (b * nT + i, GATE_COL0 // 128))
    const = lambda *shape: pl.BlockSpec(shape, lambda b, i: (0,) * len(shape))
    lane_row = lambda v, lane: jnp.zeros((1, 128), F32).at[0, lane:lane + N_HEADS].set(v)
    y_spec = pl.BlockSpec((TB, GROUP_W), lambda b, i: (b * nT + i, 0))
    per_seq = lambda *shape: pl.BlockSpec((1,) + shape, lambda b, i: (b,) + (0,) * len(shape))
    mat = (N_HEADS, HEAD_DIM, HEAD_DIM)
    pool_specs = [col(COL_A), col(COL_DB), col(COL_DC), col(COL_DH),
                  const(4, 128, 128), const(1, GROUP_W), const(SC_WIDTH, GROUP_W)]
    dn_specs = [col(COL_BQ), col(COL_BK), col(COL_BV), col(COL_BZ), gates,
                const(DN_CONV, 3 * GROUP_W), const(1, 128), const(1, 128), const(1, HEAD_DIM)]
    ml_specs = [col(COL_CQ), col(COL_CK), col(COL_CV), col(COL_CO), gates,
                const(1, 128), const(1, 128), const(N_HEADS, HEAD_DIM)]
    assert (len(pool_specs), len(dn_specs), len(ml_specs)) == (N_POOL_IN, N_DN_IN, N_ML_IN)
    y_shape = jax.ShapeDtypeStruct((M, GROUP_W), BF16)
    state = lambda *shape: jax.ShapeDtypeStruct((B,) + shape, F32)
    return pl.pallas_call(
        functools.partial(_mixers_kernel, tb=TB),
        grid=(B, nT),
        in_specs=pool_specs + dn_specs + ml_specs,
        out_specs=[y_spec] * 4 + [per_seq(16, GROUP_W), per_seq(8, GROUP_W), per_seq(*mat),
                                  per_seq(8, 3 * GROUP_W), per_seq(*mat), per_seq(2 * N_HEADS, HEAD_DIM)],
        out_shape=[y_shape] * 4 + [state(16, GROUP_W), state(8, GROUP_W), state(*mat),
                                   state(8, 3 * GROUP_W), state(*mat), state(2 * N_HEADS, HEAD_DIM)],
        scratch_shapes=[pltpu.VMEM((16, GROUP_W), F32), pltpu.VMEM((8, GROUP_W), F32),
                        pltpu.VMEM(mat, F32), pltpu.VMEM((8, 3 * GROUP_W), F32),
                        pltpu.VMEM(mat, F32), pltpu.VMEM((2 * N_HEADS, HEAD_DIM), F32)],
        compiler_params=_cparams(("parallel", "arbitrary"), VMEM_LIMIT),
    )(P, P, P, P, lw['pool_w'], lw['pool_scale'].reshape(1, GROUP_W), lw['sc_conv_w'],
      P, P, P, P, P, lw['dn_conv_w'], lane_row(lw['dn_A_log'], LANE_A), lane_row(lw['dn_dt_bias'], LANE_A),
      lw['dn_norm_g'].reshape(1, HEAD_DIM),
      P, P, P, P, P, lane_row(lw['ml_b_i'], LANE_I), lane_row(lw['ml_b_f'], LANE_F), lw['ml_norm_g'])


def _xattn_prompt_kernel(x_ref, gpre_ref, wq_ref, kv_ref, wo_ref, gpost_ref, o_ref, h_ref):
    _norm_into(x_ref, gpre_ref, h_ref)
    q = jnp.dot(h_ref[...], wq_ref[...], preferred_element_type=F32)
    heads = range(N_HEADS)
    hd = lambda base, h: slice(base + h * HEAD_DIM, base + (h + 1) * HEAD_DIM)
    ss = [_bdot_nt(q[:, hd(0, h)], kv_ref[:, hd(0, h)]) * HEAD_DIM ** -0.5 for h in heads]
    es = [jnp.exp(s - jnp.max(s, axis=-1, keepdims=True)) for s in ss]
    outs = [_bdot(e, kv_ref[:, hd(GROUP_W, h)]) / jnp.sum(e, axis=-1, keepdims=True)
            for h, e in zip(heads, es)]
    o = jnp.concatenate(outs, axis=1).astype(BF16)
    y = jnp.dot(o, wo_ref[...], preferred_element_type=F32)
    ms = jnp.mean(y * y, axis=-1, keepdims=True)
    o_ref[...] = x_ref[...] + y * lax.rsqrt(ms + EPS) * gpost_ref[...]


def xattn_prompt(x, B, T, g_pre, wq, kv, wo, layer, g_post, *, tq=512):
    nq = T // tq
    M = B * T
    return pl.pallas_call(
        _xattn_prompt_kernel,
        grid=(B, nq),
        in_specs=[pl.BlockSpec((tq, D_MODEL), lambda b, i: (b * nq + i, 0)),
                  pl.BlockSpec((1, D_MODEL), lambda b, i: (0, 0)),
                  pl.BlockSpec((None, D_MODEL, GROUP_W), lambda b, i: (layer, 0, 0)),
                  pl.BlockSpec((MEM_LEN, 2 * GROUP_W), lambda b, i: (b, 0)),
                  pl.BlockSpec((None, GROUP_W, D_MODEL), lambda b, i: (layer, 0, 0)),
                  pl.BlockSpec((1, D_MODEL), lambda b, i: (0, 0))],
        out_specs=pl.BlockSpec((tq, D_MODEL), lambda b, i: (b * nq + i, 0)),
        out_shape=jax.ShapeDtypeStruct((M, D_MODEL), F32),
        scratch_shapes=[pltpu.VMEM((tq, D_MODEL), BF16)],
        compiler_params=_cparams(("parallel", "parallel"), VMEM_LIMIT),
    )(x, g_pre.reshape(1, D_MODEL), wq, kv, wo, g_post.reshape(1, D_MODEL))


SC_BETA, SC_EG, SC_QKB, SC_LI, SC_LF, SC_QKC = 0, 4, 8, 12, 16, 20


def _sample_pre_kernel(p_ref, pool_ref, dnc_ref, scc_ref, pw_ref, ps_ref, cw_ref, alog_ref, dtb_ref,
                       bi_ref, bf_ref, sw_ref,
                       ya_ref, yd_ref, pool_out_ref, dnc_out_ref, scc_out_ref,
                       rows_b_ref, rows_c_ref, scal_ref, cols_ref, *, start_pos):
    nb = p_ref.shape[0]
    blk = lambda c: p_ref[:, c * GROUP_W:(c + 1) * GROUP_W]

    u = blk(COL_A)
    ys = []
    for g, w in enumerate(POOL_WINDOWS):
        lanes = slice(g * 128, (g + 1) * 128)
        tot = u[:, lanes]
        for r in range(POOL_STATE + 1 - w, POOL_STATE):
            tot = tot + pool_ref[:, r * GROUP_W + g * 128:r * GROUP_W + (g + 1) * 128]
        d = tot / float(min(start_pos + 1, w)) - u[:, lanes]
        ys.append(_bdot(d, pw_ref[g]))
    ya_ref[...] = (jnp.concatenate(ys, axis=1) * ps_ref[...]).astype(ya_ref.dtype)
    pool_out_ref[:, :(POOL_STATE - 1) * GROUP_W] = pool_ref[:, GROUP_W:]
    pool_out_ref[:, (POOL_STATE - 1) * GROUP_W:] = u

    us = blk(COL_DC) * blk(COL_DH)
    sw = sw_ref[...]
    y = sw[0:1] * scc_ref[:, :GROUP_W] + sw[1:2] * scc_ref[:, GROUP_W:] + sw[2:3] * us
    yd_ref[...] = (blk(COL_DB) * y).astype(yd_ref.dtype)
    scc_out_ref[:, :GROUP_W] = scc_ref[:, GROUP_W:]
    scc_out_ref[:, GROUP_W:] = us

    qkv = p_ref[:, COL_BQ * GROUP_W:(COL_BV + 1) * GROUP_W]
    cw = cw_ref[...]
    W3 = 3 * GROUP_W
    conv = cw[DN_CONV - 1:DN_CONV] * qkv
    for j in range(DN_CONV - 1):
        conv = conv + cw[j:j + 1] * dnc_ref[:, j * W3:(j + 1) * W3]
    act = _silu(conv)
    dnc_out_ref[:, :(DN_CONV - 2) * W3] = dnc_ref[:, W3:]
    dnc_out_ref[:, (DN_CONV - 2) * W3:] = qkv

    gates = p_ref[:, GATE_COL0:GATE_COL0 + 128]
    beta_all = _sigmoid(gates)
    eg_all = jnp.exp(-jnp.exp(alog_ref[...]) * _softplus(gates + dtb_ref[...]))
    li_all = gates + bi_ref[...]
    lf_all = -_softplus(-(gates + bf_ref[...]))

    lane = lax.broadcasted_iota(jnp.int32, (nb, 128), 1)
    scal = jnp.zeros((nb, 128), F32)

    def put(tab, lane_idx, colv):
        return jnp.where(lane == lane_idx, colv, tab)

    qs, ks = [], []
    for h in range(N_HEADS):
        sl = slice(h * HEAD_DIM, (h + 1) * HEAD_DIM)
        qh = act[:, sl]
        kh = act[:, GROUP_W + h * HEAD_DIM:GROUP_W + (h + 1) * HEAD_DIM]
        qh = qh * lax.rsqrt(jnp.sum(qh * qh, axis=-1, keepdims=True) + EPS) * HEAD_DIM ** -0.5
        kh = kh * lax.rsqrt(jnp.sum(kh * kh, axis=-1, keepdims=True) + EPS)
        qs.append(qh)
        ks.append(kh)
        scal = put(scal, SC_BETA + h, _col(beta_all, LANE_BETA + h))
        scal = put(scal, SC_EG + h, _col(eg_all, LANE_A + h))
        scal = put(scal, SC_QKB + h, jnp.sum(qh * kh, axis=-1, keepdims=True))
        cols_ref[(0 * N_HEADS + h) * HEAD_DIM:(0 * N_HEADS + h + 1) * HEAD_DIM, :] = qh.T.astype(BF16)
        cols_ref[(1 * N_HEADS + h) * HEAD_DIM:(1 * N_HEADS + h + 1) * HEAD_DIM, :] = kh.T.astype(BF16)
    rows_b_ref[...] = jnp.concatenate(qs + ks + [act[:, 2 * GROUP_W:]], axis=1)

    qc = blk(COL_CQ)
    kc = blk(COL_CK) * HEAD_DIM ** -0.5
    for h in range(N_HEADS):
        sl = slice(h * HEAD_DIM, (h + 1) * HEAD_DIM)
        scal = put(scal, SC_LI + h, _col(li_all, LANE_I + h))
        scal = put(scal, SC_LF + h, _col(lf_all, LANE_F + h))
        scal = put(scal, SC_QKC + h, jnp.sum(qc[:, sl] * kc[:, sl], axis=-1, keepdims=True))
        cols_ref[(2 * N_HEADS + h) * HEAD_DIM:(2 * N_HEADS + h + 1) * HEAD_DIM, :] = qc[:, sl].T.astype(BF16)
        cols_ref[(3 * N_HEADS + h) * HEAD_DIM:(3 * N_HEADS + h + 1) * HEAD_DIM, :] = kc[:, sl].T.astype(BF16)
    rows_c_ref[...] = jnp.concatenate([qc, kc, blk(COL_CV)], axis=1)
    scal_ref[...] = scal


def sample_pre(P, pool_st, dnc_st, scc_st, pool_w, pool_scale, conv_w, a_log, dt_bias, b_i, b_f, sc_w,
               start_pos):
    nb = P.shape[0]
    lane_row = lambda v, lane: jnp.zeros((1, 128), F32).at[0, lane:lane + N_HEADS].set(v)
    W3 = 3 * GROUP_W
    out_shape = [jax.ShapeDtypeStruct((nb, GROUP_W), BF16),
                 jax.ShapeDtypeStruct((nb, GROUP_W), BF16),
                 jax.ShapeDtypeStruct(pool_st.shape, F32),
                 jax.ShapeDtypeStruct(dnc_st.shape, F32),
                 jax.ShapeDtypeStruct(scc_st.shape, F32),
                 jax.ShapeDtypeStruct((nb, W3), F32),
                 jax.ShapeDtypeStruct((nb, W3), F32),
                 jax.ShapeDtypeStruct((nb, 128), F32),
                 jax.ShapeDtypeStruct((4 * N_HEADS * HEAD_DIM, nb), BF16)]
    return pl.pallas_call(
        functools.partial(_sample_pre_kernel, start_pos=start_pos),
        out_shape=out_shape,
        compiler_params=pltpu.CompilerParams(vmem_limit_bytes=VMEM_LIMIT),
    )(P, pool_st, dnc_st, scc_st, pool_w, pool_scale.reshape(1, GROUP_W), conv_w,
      lane_row(a_log, LANE_A), lane_row(dt_bias, LANE_A), lane_row(b_i, LANE_I), lane_row(b_f, LANE_F), sc_w)


def _sample_rec_kernel(cols_ref, rows_b_ref, rows_c_ref, scal_ref, z_ref, og_ref, n_ref, m_ref,
                       dng_ref, mlg_ref, s_ref, c_ref, s_acc_ref, c_acc_ref,
                       yb_ref, yc_ref, s_out_ref, c_out_ref, n_out_ref, m_out_ref,
                       ob_ref, hc_ref, *, tb):
    del s_acc_ref, c_acc_ref
    i = pl.program_id(0)
    nb = cols_ref.shape[1]
    row_id = lax.broadcasted_iota(jnp.int32, (nb, 128), 0)
    lane_id = lax.broadcasted_iota(jnp.int32, (1, 128), 1)

    def body(j, carry):
        b = i * tb + j
        onehot = (row_id == b).astype(BF16)
        cols = jnp.dot(cols_ref[...], onehot, preferred_element_type=F32)
        scal = scal_ref[pl.ds(b, 1), :]
        sc = lambda idx: _col(scal, idx)
        m_row = m_ref[pl.ds(b, 1), :]
        rb = rows_b_ref[pl.ds(b, 1), :]
        rc = rows_c_ref[pl.ds(b, 1), :]
        n_all = n_ref[pl.ds(b, 1), :]
        m_new_row = jnp.zeros((1, 128), F32)
        o_rows, h_rows, n_rows = [], [], []
        for h in range(N_HEADS):
            sl = slice(h * HEAD_DIM, (h + 1) * HEAD_DIM)
            colblk = lambda v: cols[(v * N_HEADS + h) * HEAD_DIM:(v * N_HEADS + h + 1) * HEAD_DIM, :]
            s = s_ref[j, h]
            ks = jnp.sum(colblk(1) * s, axis=0, keepdims=True)
            qs = jnp.sum(colblk(0) * s, axis=0, keepdims=True)
            beta, eg, qk = sc(SC_BETA + h), sc(SC_EG + h), sc(SC_QKB + h)
            v_row = rb[:, 2 * GROUP_W + h * HEAD_DIM:2 * GROUP_W + (h + 1) * HEAD_DIM]
            v_new = beta * v_row - (beta * eg) * ks
            o_rows.append(eg * qs + qk * v_new)
            s_out_ref[j, h] = s * eg + colblk(1) * v_new
            cm = c_ref[j, h]
            qc = jnp.sum(colblk(2) * cm, axis=0, keepdims=True)
            q_row = rc[:, sl]
            k_row = rc[:, GROUP_W + h * HEAD_DIM:GROUP_W + (h + 1) * HEAD_DIM]
            vc_row = rc[:, 2 * GROUP_W + h * HEAD_DIM:2 * GROUP_W + (h + 1) * HEAD_DIM]
            n_row = n_all[:, sl]
            li, lf, qkc = sc(SC_LI + h), sc(SC_LF + h), sc(SC_QKC + h)
            m_old = _col(m_row, h)
            bb = lf + m_old
            m_t = jnp.maximum(bb, li)
            w_intra = jnp.exp(li - m_t) * qkc
            w_inter = jnp.exp(bb - m_t)
            num = w_inter * qc + w_intra * vc_row
            den = w_inter * jnp.sum(q_row * n_row, axis=-1, keepdims=True) + w_intra
            h_rows.append(num / jnp.maximum(jnp.abs(den), jnp.exp(-m_t)))
            m_new = jnp.maximum(m_old + lf, li)
            dec = jnp.exp(m_old + lf - m_new)
            e = jnp.exp(li - m_new)
            c_out_ref[j, h] = cm * dec + colblk(3) * (e * vc_row)
            n_rows.append(n_row * dec + e * k_row)
            m_new_row = jnp.where(lane_id == h, m_new, m_new_row)
        ob_ref[pl.ds(b, 1), :] = jnp.concatenate(o_rows, axis=1)
        hc_ref[pl.ds(b, 1), :] = jnp.concatenate(h_rows, axis=1)
        n_out_ref[pl.ds(b, 1), :] = jnp.concatenate(n_rows, axis=1)
        m_out_ref[pl.ds(b, 1), :] = m_new_row
        return carry

    lax.fori_loop(0, tb, body, 0, unroll=2)

    @pl.when(i == pl.num_programs(0) - 1)
    def _():
        ys_b, ys_c = [], []
        for h in range(N_HEADS):
            sl = slice(h * HEAD_DIM, (h + 1) * HEAD_DIM)
            o = ob_ref[:, sl]
            o = o * lax.rsqrt(jnp.mean(o * o, axis=-1, keepdims=True) + EPS) * dng_ref[...]
            ys_b.append(o * _silu(z_ref[:, sl]))
            hh = _sigmoid(og_ref[:, sl]) * hc_ref[:, sl]
            hh = hh * lax.rsqrt(jnp.mean(hh * hh, axis=-1, keepdims=True) + EPS) * mlg_ref[h:h + 1, :]
            ys_c.append(hh)
        yb_ref[...] = jnp.concatenate(ys_b, axis=1).astype(yb_ref.dtype)
        yc_ref[...] = jnp.concatenate(ys_c, axis=1).astype(yc_ref.dtype)


def sample_rec(cols, rows_b, rows_c, scal, P, n_st, m_st, dn_norm_g, ml_norm_g, s_all, c_all, s_acc, c_acc,
               layer, *, tb=8):
    nb = P.shape[0]
    W3 = 3 * GROUP_W
    full = lambda shape: pl.BlockSpec(shape, lambda i: (0,) * len(shape))
    state = pl.BlockSpec((None, tb, N_HEADS, HEAD_DIM, HEAD_DIM), lambda i: (layer, i, 0, 0, 0))
    untouched = pl.BlockSpec(memory_space=pl.ANY)
    return pl.pallas_call(
        functools.partial(_sample_rec_kernel, tb=tb),
        grid=(nb // tb,),
        in_specs=[full(cols.shape), full((nb, W3)), full((nb, W3)), full((nb, 128)),
                  pl.BlockSpec((nb, GROUP_W), lambda i: (0, COL_BZ)),
                  pl.BlockSpec((nb, GROUP_W), lambda i: (0, COL_CO)),
                  full((nb, GROUP_W)), full((nb, 128)), full((1, HEAD_DIM)), full((N_HEADS, HEAD_DIM)),
                  state, state, untouched, untouched],
        out_specs=[full((nb, GROUP_W)), full((nb, GROUP_W)), state, state,
                   full((nb, GROUP_W)), full((nb, 128))],
        out_shape=[jax.ShapeDtypeStruct((nb, GROUP_W), BF16),
                   jax.ShapeDtypeStruct((nb, GROUP_W), BF16),
                   jax.ShapeDtypeStruct(s_all.shape, F32),
                   jax.ShapeDtypeStruct(c_all.shape, F32),
                   jax.ShapeDtypeStruct((nb, GROUP_W), F32),
                   jax.ShapeDtypeStruct((nb, 128), F32)],
        input_output_aliases={12: 2, 13: 3},
        scratch_shapes=[pltpu.VMEM((nb, GROUP_W), F32), pltpu.VMEM((nb, GROUP_W), F32)],
        compiler_params=_cparams(("arbitrary",), VMEM_LIMIT),
    )(cols, rows_b, rows_c, scal, P, P, n_st, m_st, dn_norm_g.reshape(1, HEAD_DIM), ml_norm_g, s_all, c_all,
      s_acc, c_acc)


def _sample_xattn_kernel(q_ref, k_ref, v_ref, o_ref, *, tb):
    i = pl.program_id(0)
    n_rows = k_ref.shape[1]
    n_rep = n_rows // (2 * N_HEADS)
    ones = jnp.ones((HEAD_DIM, HEAD_DIM), BF16)
    fold = lambda x: x + pltpu.roll(x, N_HEADS, 0)

    def body(j, carry):
        b = i * tb + j
        q_row = q_ref[pl.ds(b, 1), :]
        heads = [q_row[:, h * HEAD_DIM:(h + 1) * HEAD_DIM] for h in range(N_HEADS)]
        q8 = jnp.concatenate(heads + heads, axis=0) * HEAD_DIM ** -0.5
        k3 = k_ref[j].reshape(n_rep, 2 * N_HEADS, HEAD_DIM)
        prod = (k3 * q8[None]).reshape(n_rows, HEAD_DIM)
        s = _bdot(prod, ones)
        s3 = s.reshape(n_rep, 2 * N_HEADS, HEAD_DIM)
        mx = jnp.max(s3, axis=0)
        mx = jnp.maximum(mx, pltpu.roll(mx, N_HEADS, 0))
        e3 = jnp.exp(s3 - mx[None])
        den = fold(jnp.sum(e3, axis=0))
        v3 = v_ref[j].reshape(n_rep, 2 * N_HEADS, HEAD_DIM)
        o8 = fold(jnp.sum(e3 * v3, axis=0)) / den
        o_ref[pl.ds(b, 1), :] = jnp.concatenate([o8[h:h + 1, :] for h in range(N_HEADS)], axis=1)
        return carry

    lax.fori_loop(0, tb, body, 0, unroll=4)


def sample_xattn(q, k_all, v_all, layer, *, tb=8):
    nb = q.shape[0]
    n_rows = k_all.shape[2]
    kv = pl.BlockSpec((None, tb, n_rows, HEAD_DIM), lambda i: (layer, i, 0, 0))
    return pl.pallas_call(
        functools.partial(_sample_xattn_kernel, tb=tb),
        grid=(nb // tb,),
        in_specs=[pl.BlockSpec((nb, GROUP_W), lambda i: (0, 0)), kv, kv],
        out_specs=pl.BlockSpec((nb, GROUP_W), lambda i: (0, 0)),
        out_shape=jax.ShapeDtypeStruct((nb, GROUP_W), F32),
        compiler_params=_cparams(("arbitrary",), VMEM_LIMIT),
    )(q, k_all, v_all)


W_B0 = 5 * GROUP_W
W_C0 = W_B0 + 2 * N_HEADS
W_C1 = W_C0 + 4 * GROUP_W
W_D0 = W_C1 + 2 * N_HEADS
W_D1 = W_D0 + 3 * GROUP_W


def _w_in_prep_kernel(w_ref, o_ref):
    tk = w_ref.shape[1]
    o_ref[:W_B0, :] = w_ref[:W_B0, :].astype(BF16)
    o_ref[W_B0:W_B0 + (W_C1 - W_C0), :] = w_ref[W_C0:W_C1, :].astype(BF16)
    o_ref[W_B0 + (W_C1 - W_C0):GATE_COL0, :] = w_ref[W_D0:W_D1, :].astype(BF16)
    gates = jnp.concatenate([w_ref[W_B0:W_C0, :], w_ref[W_C1:W_D0, :],
                             jnp.zeros((128 - 4 * N_HEADS, tk), F32)], axis=0)
    o_ref[GATE_COL0:, :] = gates.astype(BF16)


def _prep_w_in(w_in, *, tk=256):
    w_t = jnp.swapaxes(w_in, 1, 2)
    depth, n_in, K = w_t.shape
    assert n_in == W_D1
    return pl.pallas_call(
        _w_in_prep_kernel,
        grid=(depth, K // tk),
        in_specs=[pl.BlockSpec((None, n_in, tk), lambda l, i: (l, 0, i))],
        out_specs=pl.BlockSpec((None, IN_W_PAD, tk), lambda l, i: (l, 0, i)),
        out_shape=jax.ShapeDtypeStruct((depth, IN_W_PAD, K), BF16),
        compiler_params=_cparams(("parallel", "parallel"), VMEM_LIMIT),
    )(w_t)


def _ffn(x, g_pre, g_post, wg, wu, wd, layer, *, tm_up, tm_down):
    act = norm_matmul(x, g_pre, [wg, wu], layer, tm=tm_up, tn=FFN_TN, out_dtype=BF16)
    return matmul_resnorm([act], wd, layer, g_post, x, scale=0.5, tm=tm_down)


def _prompt_layer(x, mem2d, B, T, lw, layer):
    g = lw['norm_g']
    x = _ffn(x, g[0], g[1], lw['ffn1_wg'], lw['ffn1_wu'], lw['ffn1_wd'], layer, tm_up=1024, tm_down=512)
    P = norm_matmul(x, g[2], [lw['w_in']], layer, tm=1024, tn=IN_TN, out_dtype=F32, w_is_nk=True)
    ya, yd, yb, yc, pool_tail, sc_tail, dn_s, dn_tail, ml_c, ml_nm = mixers_prompt(P, B, T, lw)
    x = matmul_resnorm([ya, yb, yc, yd], lw['w_out'], layer, g[3], x, scale=1.0, tm=512)
    kv = norm_matmul(mem2d, g[8], [lw['x_wkv']], layer, tm=1024, tn=512, out_dtype=F32)
    x = xattn_prompt(x, B, T, g[4], lw['x_wq'], kv, lw['x_wo'], layer, g[5])
    x = _ffn(x, g[6], g[7], lw['ffn2_wg'], lw['ffn2_wu'], lw['ffn2_wd'], layer, tm_up=1024, tm_down=512)
    states = (pool_tail[:, 16 - POOL_STATE:], dn_tail[:, 8 - (DN_CONV - 1):], dn_s, ml_c,
              ml_nm[:, :N_HEADS], ml_nm[:, N_HEADS:, 0], sc_tail[:, 8 - (SC_WIDTH - 1):])
    mem_k = kv[:, :GROUP_W].reshape(B, MEM_LEN, N_HEADS, HEAD_DIM)
    mem_v = kv[:, GROUP_W:].reshape(B, MEM_LEN, N_HEADS, HEAD_DIM)
    return x, states, mem_k, mem_v


def _sample_layer(x, st, big, acc, layer, lw, start_pos):
    pool_st, dnc_st, ml_n, ml_m, scc_st = st
    s_all, c_all, k_all, v_all = big
    s_acc, c_acc = acc
    nb = x.shape[0]
    g = lw['norm_g']
    x = _ffn(x, g[0], g[1], lw['ffn1_wg'], lw['ffn1_wu'], lw['ffn1_wd'], layer, tm_up=128, tm_down=128)
    P = norm_matmul(x, g[2], [lw['w_in']], layer, tm=128, tn=IN_TN, out_dtype=F32, w_is_nk=True)
    (ya, yd, pool_new, dnc_new, scc_new, rows_b, rows_c, scal, cols) = sample_pre(
        P, pool_st.reshape(nb, -1), dnc_st.reshape(nb, -1), scc_st.reshape(nb, -1),
        lw['pool_w'], lw['pool_scale'], lw['dn_conv_w'], lw['dn_A_log'], lw['dn_dt_bias'],
        lw['ml_b_i'], lw['ml_b_f'], lw['sc_conv_w'], start_pos)
    m_pad = jnp.pad(ml_m, ((0, 0), (0, 128 - N_HEADS)))
    yb, yc, s_acc, c_acc, n_new, m_new = sample_rec(
        cols, rows_b, rows_c, scal, P, ml_n.reshape(nb, GROUP_W), m_pad,
        lw['dn_norm_g'], lw['ml_norm_g'], s_all, c_all, s_acc, c_acc, layer)
    x = matmul_resnorm([ya, yb, yc, yd], lw['w_out'], layer, g[3], x, scale=1.0, tm=128)
    q = norm_matmul(x, g[4], [lw['x_wq']], layer, tm=128, tn=GROUP_W, out_dtype=F32)
    o = sample_xattn(q, k_all, v_all, layer)
    x = matmul_resnorm([o], lw['x_wo'], layer, g[5], x, scale=1.0, tm=128)
    x = _ffn(x, g[6], g[7], lw['ffn2_wg'], lw['ffn2_wu'], lw['ffn2_wd'], layer, tm_up=128, tm_down=128)
    states = (pool_new.reshape(pool_st.shape), dnc_new.reshape(dnc_st.shape),
              n_new.reshape(ml_n.shape), m_new[:, :N_HEADS], scc_new.reshape(scc_st.shape))
    return x, states, (s_acc, c_acc)


def kernel(x_prompt, x_sample, mem_prompt, state_pool, state_dn_conv, state_dn_S, state_ml_C, state_ml_n,
           state_ml_m, state_sc_conv, cache_mem_k, cache_mem_v, norm_g, w_in, w_out, pool_w, pool_scale,
           dn_conv_w, dn_A_log, dn_dt_bias, dn_norm_g, ml_b_i, ml_b_f, ml_norm_g, sc_conv_w,
           x_wq, x_wk, x_wv, x_wo, ffn1_wg, ffn1_wu, ffn1_wd, ffn2_wg, ffn2_wu, ffn2_wd):
    depth = norm_g.shape[0]
    B, T, _ = x_prompt.shape
    nb, t_dec, _ = x_sample.shape
    assert t_dec == 1
    start_pos = PAST_LEN

    stacked = dict(w_in=_prep_w_in(w_in), w_out=w_out.astype(BF16), x_wq=x_wq.astype(BF16),
                   x_wkv=jnp.concatenate([x_wk.astype(BF16), x_wv.astype(BF16)], axis=-1),
                   x_wo=x_wo.astype(BF16),
                   ffn1_wg=ffn1_wg.astype(BF16), ffn1_wu=ffn1_wu.astype(BF16), ffn1_wd=ffn1_wd.astype(BF16),
                   ffn2_wg=ffn2_wg.astype(BF16), ffn2_wu=ffn2_wu.astype(BF16), ffn2_wd=ffn2_wd.astype(BF16))

    def layer_weights(l):
        return dict(stacked, norm_g=norm_g[l], pool_w=pool_w[l],
                    pool_scale=pool_scale[l], dn_conv_w=dn_conv_w[l], dn_A_log=dn_A_log[l],
                    dn_dt_bias=dn_dt_bias[l], dn_norm_g=dn_norm_g[l], ml_b_i=ml_b_i[l], ml_b_f=ml_b_f[l],
                    ml_norm_g=ml_norm_g[l], sc_conv_w=sc_conv_w[l])

    mem2d = mem_prompt.reshape(B * MEM_LEN, D_MODEL)
    h = x_prompt.reshape(B * T, D_MODEL)
    p_states, mem_k_list, mem_v_list = [], [], []
    for l in range(depth):
        h, ns, mk, mv = _prompt_layer(h, mem2d, B, T, layer_weights(l), l)
        p_states.append(ns)
        mem_k_list.append(mk)
        mem_v_list.append(mv)
    y_prompt = h.reshape(B, T, D_MODEL)

    s_inputs = (state_pool, state_dn_conv, state_ml_n, state_ml_m, state_sc_conv)
    big = (state_dn_S, state_ml_C,
           cache_mem_k.reshape(depth, nb, MEM_LEN * N_HEADS, HEAD_DIM),
           cache_mem_v.reshape(depth, nb, MEM_LEN * N_HEADS, HEAD_DIM))
    h = x_sample.reshape(nb, D_MODEL)
    s_states = []
    acc = (lax.empty(state_dn_S.shape, F32), lax.empty(state_ml_C.shape, F32))
    for l in range(depth):
        st = tuple(s[l] for s in s_inputs)
        h, ns, acc = _sample_layer(h, st, big, acc, l, layer_weights(l), start_pos)
        s_states.append(ns)
    y_sample = h.reshape(nb, 1, D_MODEL)
    dn_S_s, ml_C_s = acc

    pool_p, dn_conv_p, dn_S_p, ml_C_p, ml_n_p, ml_m_p, sc_conv_p = [jnp.stack(z) for z in zip(*p_states)]
    pool_s, dn_conv_s, ml_n_s, ml_m_s, sc_conv_s = [jnp.stack(z) for z in zip(*s_states)]
    mem_k_p = jnp.stack(mem_k_list)
    mem_v_p = jnp.stack(mem_v_list)
    return (y_prompt, y_sample, pool_p, pool_s, dn_conv_p, dn_conv_s, dn_S_p, dn_S_s, ml_C_p, ml_C_s,
            ml_n_p, ml_n_s, ml_m_p, ml_m_s, sc_conv_p, sc_conv_s, mem_k_p, mem_v_p)
```

```python
import functools

import numpy as np
import jax
import jax.numpy as jnp
from jax import lax
from jax.experimental import pallas as pl
from jax.experimental.pallas import tpu as pltpu

F32 = jnp.float32
BF16 = jnp.bfloat16

EPS = 1e-6
D_MODEL = 2048
GROUP_W = 512
HEAD_DIM = 128
N_HEADS = GROUP_W // HEAD_DIM
CHUNK = 64
POOL_WINDOWS = (2, 4, 8, 16)
POOL_STATE = 15
DN_CONV = 4
SC_WIDTH = 3
MEM_LEN = 256
PAST_LEN = 16384
FFN_TN = 512
COL_A, COL_BQ, COL_BK, COL_BV, COL_BZ = 0, 1, 2, 3, 4
COL_CQ, COL_CK, COL_CV, COL_CO = 5, 6, 7, 8
COL_DB, COL_DC, COL_DH = 9, 10, 11
GATE_COL0 = 12 * GROUP_W
IN_W_PAD = GATE_COL0 + 128
IN_TN = 1280
LANE_BETA, LANE_A, LANE_I, LANE_F = 0, 4, 8, 12
TB = 256
SUB = 128
VMEM_LIMIT = 60 * 2**20


def _cparams(sem, vmem=None):
    return pltpu.CompilerParams(dimension_semantics=sem, vmem_limit_bytes=vmem)


def _bdot(a, b):
    return jnp.dot(a.astype(BF16), b.astype(BF16), preferred_element_type=F32)


def _bdot_nt(a, b):
    return lax.dot_general(a.astype(BF16), b.astype(BF16), (((1,), (1,)), ((), ())),
                           preferred_element_type=F32)


def _sigmoid(x):
    return 0.5 * jnp.tanh(0.5 * x) + 0.5


def _silu(x):
    return x * _sigmoid(x)


def _softplus(x):
    return jnp.maximum(x, 0.0) + jnp.log(1.0 + jnp.exp(-jnp.abs(x)))


def _col(x, idx):
    return x[:, idx:idx + 1]


def _resident(shape):
    nd = len(shape)
    return pl.BlockSpec(shape, lambda *_: (0,) * nd, pipeline_mode=pl.Buffered(1))


def _norm_into(x_ref, g_ref, h_ref):
    tm = x_ref.shape[0]
    rs = min(tm, 128)

    def body(i, carry):
        r = pl.multiple_of(i * rs, rs)
        x = x_ref[pl.ds(r, rs), :]
        ms = jnp.mean(x * x, axis=-1, keepdims=True)
        h_ref[pl.ds(r, rs), :] = (x * lax.rsqrt(ms + EPS) * g_ref[...]).astype(BF16)
        return carry

    lax.fori_loop(0, tm // rs, body, 0)


def _zero_past_width(y, n_valid):
    tn = y.shape[1]
    if n_valid % tn == 0:
        return y
    col = pl.program_id(1) * tn + lax.broadcasted_iota(jnp.int32, (1, tn), 1)
    return jnp.where(col < n_valid, y, 0.0)


def _norm_mm_kernel(x_ref, g_ref, w_ref, o_ref, h_ref, *, n_valid, w_is_nk):
    @pl.when(pl.program_id(1) == 0)
    def _():
        _norm_into(x_ref, g_ref, h_ref)

    if w_is_nk:
        y = lax.dot_general(h_ref[...], w_ref[...], (((1,), (1,)), ((), ())), preferred_element_type=F32)
    else:
        y = jnp.dot(h_ref[...], w_ref[...], preferred_element_type=F32)
    o_ref[...] = _zero_past_width(y, n_valid).astype(o_ref.dtype)


def _norm_swiglu_kernel(x_ref, g_ref, wg_ref, wu_ref, o_ref, *rest, n_valid, emit_bf16):
    h_ref = rest[-1]

    @pl.when(pl.program_id(1) == 0)
    def _():
        _norm_into(x_ref, g_ref, h_ref)

    h = h_ref[...]
    wg = wg_ref[...].astype(BF16)
    wu = wu_ref[...].astype(BF16)
    a = jnp.dot(h, wg, preferred_element_type=F32)
    b = jnp.dot(h, wu, preferred_element_type=F32)
    o_ref[...] = _zero_past_width(_silu(a) * b, n_valid).astype(o_ref.dtype)

    if emit_bf16:
        wg_out_ref, wu_out_ref = rest[:2]

        @pl.when(pl.program_id(0) == 0)
        def _():
            wg_out_ref[...] = wg
            wu_out_ref[...] = wu


def norm_matmul(x, g, ws, layer, *, tm, tn, out_dtype, w_is_nk=False, emit_bf16=False):
    M, K = x.shape
    n_valid = ws[0].shape[1 if w_is_nk else 2]
    n_steps = pl.cdiv(n_valid, tn)
    N = n_steps * tn
    tm = min(tm, M)
    if len(ws) == 2:
        assert not w_is_nk
        kern = functools.partial(_norm_swiglu_kernel, n_valid=n_valid, emit_bf16=emit_bf16)
    else:
        assert not emit_bf16
        kern = functools.partial(_norm_mm_kernel, n_valid=n_valid, w_is_nk=w_is_nk)
    w_spec = (pl.BlockSpec((None, tn, K), lambda i, j: (layer, j, 0)) if w_is_nk
              else pl.BlockSpec((None, K, tn), lambda i, j: (layer, 0, j)))
    out_specs = pl.BlockSpec((tm, tn), lambda i, j: (i, j))
    out_shape = jax.ShapeDtypeStruct((M, N), out_dtype)
    if emit_bf16:
        w_out = pl.BlockSpec((K, tn), lambda i, j: (0, jnp.where(i == 0, j, n_steps - 1)))
        out_specs = [out_specs, w_out, w_out]
        out_shape = [out_shape] + [jax.ShapeDtypeStruct((K, n_valid), BF16)] * 2
    return pl.pallas_call(
        kern,
        grid=(M // tm, n_steps),
        in_specs=[pl.BlockSpec((tm, K), lambda i, j: (i, 0)),
                  pl.BlockSpec((1, K), lambda i, j: (0, 0))]
        + [w_spec for _ in ws],
        out_specs=out_specs,
        out_shape=out_shape,
        scratch_shapes=[pltpu.VMEM((tm, K), BF16)],
        compiler_params=_cparams(("arbitrary", "arbitrary") if emit_bf16 else ("parallel", "arbitrary"),
                                 VMEM_LIMIT),
    )(x, g.reshape(1, K), *ws)


def _mm_resnorm_kernel(*refs, n_a, scale):
    a_refs = refs[:n_a]
    w_ref, g_ref, res_ref, o_ref = refs[n_a:]
    parts = [r[...].astype(BF16) for r in a_refs]
    a = parts[0] if n_a == 1 else jnp.concatenate(parts, axis=1)
    y = jnp.dot(a, w_ref[...], preferred_element_type=F32)
    ms = jnp.mean(y * y, axis=-1, keepdims=True)
    o_ref[...] = res_ref[...] + scale * (y * lax.rsqrt(ms + EPS) * g_ref[...])


def matmul_resnorm(a_list, w, layer, g, res, *, scale, tm):
    M, N = res.shape
    tm = min(tm, M)
    K = w.shape[1]
    widths = [K] if len(a_list) == 1 else [a.shape[1] for a in a_list]
    assert sum(widths) == K
    w_spec = pl.BlockSpec((None, K, N), lambda i: (layer, 0, 0), pipeline_mode=pl.Buffered(1))
    return pl.pallas_call(
        functools.partial(_mm_resnorm_kernel, n_a=len(a_list), scale=scale),
        grid=(M // tm,),
        in_specs=[pl.BlockSpec((tm, wd), lambda i: (i, 0)) for wd in widths]
        + [w_spec, _resident((1, N)), pl.BlockSpec((tm, N), lambda i: (i, 0))],
        out_specs=pl.BlockSpec((tm, N), lambda i: (i, 0)),
        out_shape=jax.ShapeDtypeStruct((M, N), F32),
        compiler_params=_cparams(("parallel",), VMEM_LIMIT),
    )(*a_list, w, g.reshape(1, N), res)


def _chunk_cumsum(x):
    n = x.shape[0]
    row = lax.broadcasted_iota(jnp.int32, (n, 1), 0) % CHUNK
    s = 1
    while s < CHUNK:
        x = x + jnp.where(row >= s, pltpu.roll(x, s, 0), 0.0)
        s *= 2
    return x


def _chunk_last(x):
    n = x.shape[0]
    parts = [jnp.broadcast_to(x[c * CHUNK + CHUNK - 1:c * CHUNK + CHUNK, :], (CHUNK, x.shape[1]))
             for c in range(n // CHUNK)]
    return jnp.concatenate(parts, axis=0)


def _pad_rows(x, c, n_chunks):
    z = jnp.zeros_like(x)
    return jnp.concatenate([x if i == c else z for i in range(n_chunks)], axis=0)


def _blk_masks(n):
    r = lax.broadcasted_iota(jnp.int32, (n, n), 0)
    c = lax.broadcasted_iota(jnp.int32, (n, n), 1)
    same = (r ^ c) < CHUNK
    return same & (c <= r), same & (c < r), r == c


def _split(a):
    ah = a.astype(BF16)
    return ah, (a - ah.astype(F32)).astype(BF16)


def _split_dot(a, b):
    d = lambda x, y: jnp.dot(x, y, preferred_element_type=F32)
    return d(a[0], b[0]) + d(a[0], b[1]) + d(a[1], b[0])


def _unit_lower_inverses(a_list, n):
    r = lax.broadcasted_iota(jnp.int32, (n, n), 0)
    c = lax.broadcasted_iota(jnp.int32, (n, n), 1)
    rc = r ^ c
    pair = lambda s: (rc >= s) & (rc < 2 * s)
    eye = (r == c).astype(F32)
    xs = [eye - jnp.where(pair(1), a, 0.0) for a in a_list]
    s = 2
    while s < CHUNK:
        m = pair(s)
        x_sp = [_split(x) for x in xs]
        ts = [_split_dot(_split(jnp.where(m, a, 0.0)), x) for a, x in zip(a_list, x_sp)]
        xs = [x - _split_dot(xp, _split(t)) for x, xp, t in zip(xs, x_sp, ts)]
        s *= 2
    return xs


def _pool_sconv_step(ua_ref, db_ref, dc_ref, dh_ref, pw_ref, ps_ref, sw_ref, ya_ref, yd_ref, hist_a, hist_d,
                     *, tb):
    i = pl.program_id(1)

    u = ua_ref[...]
    ext = jnp.concatenate([hist_a[...], u], axis=0)
    a2 = ext + pltpu.roll(ext, 1, 0)
    a4 = a2[:, 128:] + pltpu.roll(a2[:, 128:], 2, 0)
    a8 = a4[:, 128:] + pltpu.roll(a4[:, 128:], 4, 0)
    a16 = a8[:, 128:] + pltpu.roll(a8[:, 128:], 8, 0)
    sums = (a2[16:, :128], a4[16:, :128], a8[16:, :128], a16[16:, :])
    pos = i * tb + lax.broadcasted_iota(jnp.int32, (tb, 1), 0)
    ys = []
    for g, w in enumerate(POOL_WINDOWS):
        cnt = jnp.minimum(pos + 1, w).astype(F32)
        d = sums[g] / cnt - u[:, g * 128:(g + 1) * 128]
        ys.append(_bdot(d, pw_ref[g]))
    ya_ref[...] = (jnp.concatenate(ys, axis=1) * ps_ref[...]).astype(ya_ref.dtype)
    hist_a[...] = u[tb - 16:, :]

    us = dc_ref[...] * dh_ref[...]
    ext = jnp.concatenate([hist_d[...], us], axis=0)
    sw = sw_ref[...]
    y = sw[0:1] * pltpu.roll(ext, 2, 0) + sw[1:2] * pltpu.roll(ext, 1, 0) + sw[2:3] * ext
    yd_ref[...] = (db_ref[...] * y[8:]).astype(yd_ref.dtype)
    hist_d[...] = us[tb - 8:, :]
    return u[tb - 16:, :], us[tb - 8:, :]


def _deltanet_step(q_ref, k_ref, v_ref, z_ref, gt_ref, cw_ref, alog_ref, dtb_ref, ng_ref, y_ref, s_ref, hist_ref,
                   *, tb):
    nc = tb // CHUNK

    qkv = jnp.concatenate([q_ref[...], k_ref[...], v_ref[...]], axis=1)
    ext = jnp.concatenate([hist_ref[...], qkv], axis=0)
    cw = cw_ref[...]
    conv = (cw[0:1] * pltpu.roll(ext, 3, 0) + cw[1:2] * pltpu.roll(ext, 2, 0)
            + cw[2:3] * pltpu.roll(ext, 1, 0) + cw[3:4] * ext)[8:]
    act = _silu(conv)
    hist_ref[...] = qkv[tb - 8:, :]

    gates = gt_ref[...]
    beta_all = _sigmoid(gates)
    g_all = -jnp.exp(alog_ref[...]) * _softplus(gates + dtb_ref[...])
    gcum_all = _chunk_cumsum(g_all)
    glast_all = _chunk_last(gcum_all)
    gcum_t = gcum_all.T
    causal, strict, _ = _blk_masks(SUB)

    heads = range(N_HEADS)
    subs = range(tb // SUB)
    cps = SUB // CHUNK
    hd = lambda base, h: slice(base + h * HEAD_DIM, base + (h + 1) * HEAD_DIM)
    qs, ks, gcs, gls, betas = [], [], [], [], []
    for h in heads:
        qh = act[:, hd(0, h)]
        kh = act[:, hd(GROUP_W, h)]
        qs.append(qh * lax.rsqrt(jnp.sum(qh * qh, axis=-1, keepdims=True) + EPS) * HEAD_DIM ** -0.5)
        ks.append(kh * lax.rsqrt(jnp.sum(kh * kh, axis=-1, keepdims=True) + EPS))
        betas.append(_col(beta_all, LANE_BETA + h))
        gcs.append(_col(gcum_all, LANE_A + h))
        gls.append(_col(glast_all, LANE_A + h))
    k_ts, a_mats, qks = {}, [], {}
    for st in subs:
        R = slice(st * SUB, (st + 1) * SUB)
        for h in heads:
            gr = gcum_t[LANE_A + h:LANE_A + h + 1, R]
            decay = jnp.where(causal, jnp.exp(jnp.where(causal, gcs[h][R] - gr, 0.0)), 0.0)
            k_t = ks[h][R].T.astype(BF16)
            a_mats.append(jnp.where(strict, _bdot(ks[h][R] * betas[h][R], k_t) * decay, 0.0))
            qks[st, h] = jnp.where(causal, _bdot(qs[h][R], k_t) * decay, 0.0).astype(BF16)
            k_ts[st, h] = k_t
    tinvs = _unit_lower_inverses(a_mats, SUB)
    us, ws = {}, {}
    for st in subs:
        R = slice(st * SUB, (st + 1) * SUB)
        for h in heads:
            vb = act[R, hd(2 * GROUP_W, h)] * betas[h][R]
            kbe = ks[h][R] * (betas[h][R] * jnp.exp(gcs[h][R]))
            uw = _bdot(tinvs[st * N_HEADS + h], jnp.concatenate([vb, kbe], axis=1))
            us[st, h] = uw[:, :HEAD_DIM]
            ws[st, h] = uw[:, HEAD_DIM:].astype(BF16)
    q_decs = [(qs[h] * jnp.exp(gcs[h])).astype(BF16) for h in heads]
    tails = [jnp.exp(gls[h] - gcs[h]) for h in heads]
    s_hs = [s_ref[h] for h in heads]
    outs = [[] for _ in heads]
    for c in range(nc):
        r = slice(c * CHUNK, (c + 1) * CHUNK)
        st, cl = divmod(c, cps)
        rl = slice(cl * CHUNK, (cl + 1) * CHUNK)
        for h in heads:
            s_b = s_hs[h].astype(BF16)
            v_new = us[st, h][rl] - jnp.dot(ws[st, h][rl], s_b, preferred_element_type=F32)
            outs[h].append(jnp.dot(q_decs[h][r], s_b, preferred_element_type=F32)
                           + _bdot(qks[st, h][rl], _pad_rows(v_new, cl, cps)))
            s_dec = jnp.exp(jnp.broadcast_to(gls[h][c * CHUNK:c * CHUNK + 1], (HEAD_DIM, 1)))
            s_hs[h] = s_hs[h] * s_dec + _bdot(k_ts[st, h], _pad_rows(v_new * tails[h][r], cl, cps))
    ys = []
    for h in heads:
        s_ref[h] = s_hs[h]
        o = jnp.concatenate(outs[h], axis=0)
        o = o * lax.rsqrt(jnp.mean(o * o, axis=-1, keepdims=True) + EPS) * ng_ref[...]
        ys.append(o * _silu(z_ref[:, hd(0, h)]))
    y_ref[...] = jnp.concatenate(ys, axis=1).astype(y_ref.dtype)
    return qkv[tb - 8:, :]


def _mlstm_step(q_ref, k_ref, v_ref, og_ref, gt_ref, bi_ref, bf_ref, ng_ref, y_ref, c_ref, nm_ref, *, tb):
    nc = tb // CHUNK

    gates = gt_ref[...]
    li_all = gates + bi_ref[...]
    lf_all = -_softplus(-(gates + bf_ref[...]))
    f_all = _chunk_cumsum(lf_all)
    flast_all = _chunk_last(f_all)
    f_t = f_all.T
    li_t = li_all.T
    causal, _, _ = _blk_masks(tb)

    heads = range(N_HEADS)
    hd = lambda h: slice(h * HEAD_DIM, (h + 1) * HEAD_DIM)
    qs, kss, k_ts, evs, eks, intras, den_intras, w_inters, inv_floor, decs = ([] for _ in range(10))
    for h in heads:
        qh = q_ref[:, hd(h)]
        kh = k_ref[:, hd(h)] * HEAD_DIM ** -0.5
        vh = v_ref[:, hd(h)]
        fc = _col(f_all, LANE_F + h)
        fl = _col(flast_all, LANE_F + h)
        a_w = fl - fc + _col(li_all, LANE_I + h)
        fr = f_t[LANE_F + h:LANE_F + h + 1, :]
        lir = li_t[LANE_I + h:LANE_I + h + 1, :]
        dm = jnp.where(causal, fc - fr + lir, -jnp.inf)
        mx = jnp.max(dm, axis=1, keepdims=True)
        m_h = nm_ref[N_HEADS + h:N_HEADS + h + 1, 0:1]
        m_prev, m_next, dec_h = [], [], []
        for c in range(nc):
            r = slice(c * CHUNK, (c + 1) * CHUNK)
            fl_c = fl[c * CHUNK:c * CHUNK + 1]
            m_new = jnp.maximum(m_h + fl_c, jnp.max(a_w[r], axis=0, keepdims=True))
            dec_h.append(jnp.exp(m_h + fl_c - m_new))
            m_prev.append(jnp.broadcast_to(m_h, (CHUNK, 1)))
            m_next.append(jnp.broadcast_to(m_new, (CHUNK, 1)))
            m_h = m_new
        nm_ref[N_HEADS + h:N_HEADS + h + 1, :] = jnp.broadcast_to(m_h, (1, HEAD_DIM))
        b = fc + jnp.concatenate(m_prev, axis=0)
        m_t = jnp.maximum(b, mx)
        k_t = kh.T.astype(BF16)
        w_intra = jnp.exp(dm - m_t) * _bdot(qh, k_t)
        e = jnp.exp(a_w - jnp.concatenate(m_next, axis=0))
        qs.append(qh)
        k_ts.append(k_t)
        evs.append(e * vh)
        eks.append(e * kh)
        intras.append(_bdot(w_intra, vh))
        den_intras.append(jnp.sum(w_intra, axis=-1, keepdims=True))
        w_inters.append(jnp.exp(b - m_t))
        inv_floor.append(jnp.exp(-m_t))
        decs.append(dec_h)
    c_hs = [c_ref[h] for h in heads]
    n_hs = [nm_ref[h:h + 1, :] for h in heads]
    outs = [[] for _ in heads]
    for c in range(nc):
        r = slice(c * CHUNK, (c + 1) * CHUNK)
        for h in heads:
            q_c = qs[h][r]
            num = w_inters[h][r] * _bdot(q_c, c_hs[h]) + intras[h][r]
            den = w_inters[h][r] * jnp.sum(q_c * n_hs[h], axis=-1, keepdims=True) + den_intras[h][r]
            outs[h].append(num / jnp.maximum(jnp.abs(den), inv_floor[h][r]))
            c_hs[h] = c_hs[h] * decs[h][c] + _bdot(k_ts[h], _pad_rows(evs[h][r], c, nc))
            n_hs[h] = n_hs[h] * decs[h][c] + jnp.sum(eks[h][r], axis=0, keepdims=True)
    ys = []
    for h in heads:
        c_ref[h] = c_hs[h]
        nm_ref[h:h + 1, :] = n_hs[h]
        hh = jnp.concatenate(outs[h], axis=0)
        hh = _sigmoid(og_ref[:, hd(h)]) * hh
        hh = hh * lax.rsqrt(jnp.mean(hh * hh, axis=-1, keepdims=True) + EPS) * ng_ref[h:h + 1, :]
        ys.append(hh)
    y_ref[...] = jnp.concatenate(ys, axis=1).astype(y_ref.dtype)


N_POOL_IN, N_DN_IN, N_ML_IN = 7, 9, 8


def _mixers_kernel(*refs, tb):
    ins = iter(refs[:N_POOL_IN + N_DN_IN + N_ML_IN])
    take = lambda n: [next(ins) for _ in range(n)]
    pool_in, dn_in, ml_in = take(N_POOL_IN), take(N_DN_IN), take(N_ML_IN)
    (ya_ref, yd_ref, yb_ref, yc_ref, ptail_ref, stail_ref, s_out_ref, ctail_ref, c_out_ref, nm_out_ref,
     hist_a, hist_d, s_ref, hist_b, c_ref, nm_ref) = refs[N_POOL_IN + N_DN_IN + N_ML_IN:]
    i = pl.program_id(1)

    @pl.when(i == 0)
    def _():
        for r in (hist_a, hist_d, s_ref, hist_b, c_ref, nm_ref):
            r[...] = jnp.zeros_like(r)

    qkv_tail = _deltanet_step(*dn_in, yb_ref, s_ref, hist_b, tb=tb)
    _mlstm_step(*ml_in, yc_ref, c_ref, nm_ref, tb=tb)
    u_tail, us_tail = _pool_sconv_step(*pool_in, ya_ref, yd_ref, hist_a, hist_d, tb=tb)

    @pl.when(i == pl.num_programs(1) - 1)
    def _():
        ptail_ref[0] = u_tail
        stail_ref[0] = us_tail
        s_out_ref[0] = s_ref[...]
        ctail_ref[0] = qkv_tail
        c_out_ref[0] = c_ref[...]
        nm_out_ref[0] = nm_ref[...]


def mixers_prompt(P, B, T, lw):
    nT = T // TB
    M = B * T
    col = lambda c: pl.BlockSpec((TB, GROUP_W), lambda b, i: (b * nT + i, c))
    gates = pl.BlockSpec((TB, 128), lambda b, i: (b * nT + i, GATE_COL0 // 128))
    const = lambda *shape: pl.BlockSpec(shape, lambda b, i: (0,) * len(shape))
    lane_row = lambda v, lane: jnp.zeros((1, 128), F32).at[0, lane:lane + N_HEADS].set(v)
    y_spec = pl.BlockSpec((TB, GROUP_W), lambda b, i: (b * nT + i, 0))
    per_seq = lambda *shape: pl.BlockSpec((1,) + shape, lambda b, i: (b,) + (0,) * len(shape))
    mat = (N_HEADS, HEAD_DIM, HEAD_DIM)
    pool_specs = [col(COL_A), col(COL_DB), col(COL_DC), col(COL_DH),
                  const(4, 128, 128), const(1, GROUP_W), const(SC_WIDTH, GROUP_W)]
    dn_specs = [col(COL_BQ), col(COL_BK), col(COL_BV), col(COL_BZ), gates,
                const(DN_CONV, 3 * GROUP_W), const(1, 128), const(1, 128), const(1, HEAD_DIM)]
    ml_specs = [col(COL_CQ), col(COL_CK), col(COL_CV), col(COL_CO), gates,
                const(1, 128), const(1, 128), const(N_HEADS, HEAD_DIM)]
    assert (len(pool_specs), len(dn_specs), len(ml_specs)) == (N_POOL_IN, N_DN_IN, N_ML_IN)
    y_shape = jax.ShapeDtypeStruct((M, GROUP_W), BF16)
    state = lambda *shape: jax.ShapeDtypeStruct((B,) + shape, F32)
    return pl.pallas_call(
        functools.partial(_mixers_kernel, tb=TB),
        grid=(B, nT),
        in_specs=pool_specs + dn_specs + ml_specs,
        out_specs=[y_spec] * 4 + [per_seq(16, GROUP_W), per_seq(8, GROUP_W), per_seq(*mat),
                                  per_seq(8, 3 * GROUP_W), per_seq(*mat), per_seq(2 * N_HEADS, HEAD_DIM)],
        out_shape=[y_shape] * 4 + [state(16, GROUP_W), state(8, GROUP_W), state(*mat),
                                   state(8, 3 * GROUP_W), state(*mat), state(2 * N_HEADS, HEAD_DIM)],
        scratch_shapes=[pltpu.VMEM((16, GROUP_W), F32), pltpu.VMEM((8, GROUP_W), F32),
                        pltpu.VMEM(mat, F32), pltpu.VMEM((8, 3 * GROUP_W), F32),
                        pltpu.VMEM(mat, F32), pltpu.VMEM((2 * N_HEADS, HEAD_DIM), F32)],
        compiler_params=_cparams(("parallel", "arbitrary"), VMEM_LIMIT),
    )(P, P, P, P, lw['pool_w'], lw['pool_scale'].reshape(1, GROUP_W), lw['sc_conv_w'],
      P, P, P, P, P, lw['dn_conv_w'], lane_row(lw['dn_A_log'], LANE_A), lane_row(lw['dn_dt_bias'], LANE_A),
      lw['dn_norm_g'].reshape(1, HEAD_DIM),
      P, P, P, P, P, lane_row(lw['ml_b_i'], LANE_I), lane_row(lw['ml_b_f'], LANE_F), lw['ml_norm_g'])


def _xattn_prompt_kernel(x_ref, gpre_ref, wq_ref, kv_ref, wo_ref, gpost_ref, o_ref, h_ref):
    _norm_into(x_ref, gpre_ref, h_ref)
    q = jnp.dot(h_ref[...], wq_ref[...], preferred_element_type=F32)
    heads = range(N_HEADS)
    hd = lambda base, h: slice(base + h * HEAD_DIM, base + (h + 1) * HEAD_DIM)
    ss = [_bdot_nt(q[:, hd(0, h)], kv_ref[:, hd(0, h)]) * HEAD_DIM ** -0.5 for h in heads]
    es = [jnp.exp(s - jnp.max(s, axis=-1, keepdims=True)) for s in ss]
    outs = [_bdot(e, kv_ref[:, hd(GROUP_W, h)]) / jnp.sum(e, axis=-1, keepdims=True)
            for h, e in zip(heads, es)]
    o = jnp.concatenate(outs, axis=1).astype(BF16)
    y = jnp.dot(o, wo_ref[...], preferred_element_type=F32)
    ms = jnp.mean(y * y, axis=-1, keepdims=True)
    o_ref[...] = x_ref[...] + y * lax.rsqrt(ms + EPS) * gpost_ref[...]


def xattn_prompt(x, B, T, g_pre, wq, kv, wo, layer, g_post, *, tq=512):
    nq = T // tq
    M = B * T
    return pl.pallas_call(
        _xattn_prompt_kernel,
        grid=(B, nq),
        in_specs=[pl.BlockSpec((tq, D_MODEL), lambda b, i: (b * nq + i, 0)),
                  pl.BlockSpec((1, D_MODEL), lambda b, i: (0, 0)),
                  pl.BlockSpec((None, D_MODEL, GROUP_W), lambda b, i: (layer, 0, 0)),
                  pl.BlockSpec((MEM_LEN, 2 * GROUP_W), lambda b, i: (b, 0)),
                  pl.BlockSpec((None, GROUP_W, D_MODEL), lambda b, i: (layer, 0, 0)),
                  pl.BlockSpec((1, D_MODEL), lambda b, i: (0, 0))],
        out_specs=pl.BlockSpec((tq, D_MODEL), lambda b, i: (b * nq + i, 0)),
        out_shape=jax.ShapeDtypeStruct((M, D_MODEL), F32),
        scratch_shapes=[pltpu.VMEM((tq, D_MODEL), BF16)],
        compiler_params=_cparams(("parallel", "parallel"), VMEM_LIMIT),
    )(x, g_pre.reshape(1, D_MODEL), wq, kv, wo, g_post.reshape(1, D_MODEL))


SC_BETA, SC_EG, SC_QKB, SC_LI, SC_LF, SC_QKC = 0, 4, 8, 12, 16, 20


def _sample_pre_kernel(p_ref, pool_ref, dnc_ref, scc_ref, pw_ref, ps_ref, cw_ref, alog_ref, dtb_ref,
                       bi_ref, bf_ref, sw_ref,
                       ya_ref, yd_ref, pool_out_ref, dnc_out_ref, scc_out_ref,
                       rows_b_ref, rows_c_ref, scal_ref, cols_ref, *, start_pos):
    nb = p_ref.shape[0]
    blk = lambda c: p_ref[:, c * GROUP_W:(c + 1) * GROUP_W]

    u = blk(COL_A)
    ys = []
    for g, w in enumerate(POOL_WINDOWS):
        lanes = slice(g * 128, (g + 1) * 128)
        tot = u[:, lanes]
        for r in range(POOL_STATE + 1 - w, POOL_STATE):
            tot = tot + pool_ref[:, r * GROUP_W + g * 128:r * GROUP_W + (g + 1) * 128]
        d = tot / float(min(start_pos + 1, w)) - u[:, lanes]
        ys.append(_bdot(d, pw_ref[g]))
    ya_ref[...] = (jnp.concatenate(ys, axis=1) * ps_ref[...]).astype(ya_ref.dtype)
    pool_out_ref[:, :(POOL_STATE - 1) * GROUP_W] = pool_ref[:, GROUP_W:]
    pool_out_ref[:, (POOL_STATE - 1) * GROUP_W:] = u

    us = blk(COL_DC) * blk(COL_DH)
    sw = sw_ref[...]
    y = sw[0:1] * scc_ref[:, :GROUP_W] + sw[1:2] * scc_ref[:, GROUP_W:] + sw[2:3] * us
    yd_ref[...] = (blk(COL_DB) * y).astype(yd_ref.dtype)
    scc_out_ref[:, :GROUP_W] = scc_ref[:, GROUP_W:]
    scc_out_ref[:, GROUP_W:] = us

    qkv = p_ref[:, COL_BQ * GROUP_W:(COL_BV + 1) * GROUP_W]
    cw = cw_ref[...]
    W3 = 3 * GROUP_W
    conv = cw[DN_CONV - 1:DN_CONV] * qkv
    for j in range(DN_CONV - 1):
        conv = conv + cw[j:j + 1] * dnc_ref[:, j * W3:(j + 1) * W3]
    act = _silu(conv)
    dnc_out_ref[:, :(DN_CONV - 2) * W3] = dnc_ref[:, W3:]
    dnc_out_ref[:, (DN_CONV - 2) * W3:] = qkv

    gates = p_ref[:, GATE_COL0:GATE_COL0 + 128]
    beta_all = _sigmoid(gates)
    eg_all = jnp.exp(-jnp.exp(alog_ref[...]) * _softplus(gates + dtb_ref[...]))
    li_all = gates + bi_ref[...]
    lf_all = -_softplus(-(gates + bf_ref[...]))

    lane = lax.broadcasted_iota(jnp.int32, (nb, 128), 1)
    scal = jnp.zeros((nb, 128), F32)

    def put(tab, lane_idx, colv):
        return jnp.where(lane == lane_idx, colv, tab)

    qs, ks = [], []
    for h in range(N_HEADS):
        sl = slice(h * HEAD_DIM, (h + 1) * HEAD_DIM)
        qh = act[:, sl]
        kh = act[:, GROUP_W + h * HEAD_DIM:GROUP_W + (h + 1) * HEAD_DIM]
        qh = qh * lax.rsqrt(jnp.sum(qh * qh, axis=-1, keepdims=True) + EPS) * HEAD_DIM ** -0.5
        kh = kh * lax.rsqrt(jnp.sum(kh * kh, axis=-1, keepdims=True) + EPS)
        qs.append(qh)
        ks.append(kh)
        scal = put(scal, SC_BETA + h, _col(beta_all, LANE_BETA + h))
        scal = put(scal, SC_EG + h, _col(eg_all, LANE_A + h))
        scal = put(scal, SC_QKB + h, jnp.sum(qh * kh, axis=-1, keepdims=True))
        cols_ref[(0 * N_HEADS + h) * HEAD_DIM:(0 * N_HEADS + h + 1) * HEAD_DIM, :] = qh.T.astype(BF16)
        cols_ref[(1 * N_HEADS + h) * HEAD_DIM:(1 * N_HEADS + h + 1) * HEAD_DIM, :] = kh.T.astype(BF16)
    rows_b_ref[...] = jnp.concatenate(qs + ks + [act[:, 2 * GROUP_W:]], axis=1)

    qc = blk(COL_CQ)
    kc = blk(COL_CK) * HEAD_DIM ** -0.5
    for h in range(N_HEADS):
        sl = slice(h * HEAD_DIM, (h + 1) * HEAD_DIM)
        scal = put(scal, SC_LI + h, _col(li_all, LANE_I + h))
        scal = put(scal, SC_LF + h, _col(lf_all, LANE_F + h))
        scal = put(scal, SC_QKC + h, jnp.sum(qc[:, sl] * kc[:, sl], axis=-1, keepdims=True))
        cols_ref[(2 * N_HEADS + h) * HEAD_DIM:(2 * N_HEADS + h + 1) * HEAD_DIM, :] = qc[:, sl].T.astype(BF16)
        cols_ref[(3 * N_HEADS + h) * HEAD_DIM:(3 * N_HEADS + h + 1) * HEAD_DIM, :] = kc[:, sl].T.astype(BF16)
    rows_c_ref[...] = jnp.concatenate([qc, kc, blk(COL_CV)], axis=1)
    scal_ref[...] = scal


def sample_pre(P, pool_st, dnc_st, scc_st, pool_w, pool_scale, conv_w, a_log, dt_bias, b_i, b_f, sc_w,
               start_pos):
    nb = P.shape[0]
    lane_row = lambda v, lane: jnp.zeros((1, 128), F32).at[0, lane:lane + N_HEADS].set(v)
    W3 = 3 * GROUP_W
    out_shape = [jax.ShapeDtypeStruct((nb, GROUP_W), BF16),
                 jax.ShapeDtypeStruct((nb, GROUP_W), BF16),
                 jax.ShapeDtypeStruct(pool_st.shape, F32),
                 jax.ShapeDtypeStruct(dnc_st.shape, F32),
                 jax.ShapeDtypeStruct(scc_st.shape, F32),
                 jax.ShapeDtypeStruct((nb, W3), F32),
                 jax.ShapeDtypeStruct((nb, W3), F32),
                 jax.ShapeDtypeStruct((nb, 128), F32),
                 jax.ShapeDtypeStruct((4 * N_HEADS * HEAD_DIM, nb), BF16)]
    return pl.pallas_call(
        functools.partial(_sample_pre_kernel, start_pos=start_pos),
        out_shape=out_shape,
        compiler_params=pltpu.CompilerParams(vmem_limit_bytes=VMEM_LIMIT),
    )(P, pool_st, dnc_st, scc_st, pool_w, pool_scale.reshape(1, GROUP_W), conv_w,
      lane_row(a_log, LANE_A), lane_row(dt_bias, LANE_A), lane_row(b_i, LANE_I), lane_row(b_f, LANE_F), sc_w)


def _sample_rec_kernel(cols_ref, rows_b_ref, rows_c_ref, scal_ref, z_ref, og_ref, n_ref, m_ref,
                       dng_ref, mlg_ref, s_ref, c_ref, s_acc_ref, c_acc_ref,
                       yb_ref, yc_ref, s_out_ref, c_out_ref, n_out_ref, m_out_ref,
                       ob_ref, hc_ref, *, tb):
    del s_acc_ref, c_acc_ref
    i = pl.program_id(0)
    nb = cols_ref.shape[1]
    row_id = lax.broadcasted_iota(jnp.int32, (nb, 128), 0)
    lane_id = lax.broadcasted_iota(jnp.int32, (1, 128), 1)

    def body(j, carry):
        b = i * tb + j
        onehot = (row_id == b).astype(BF16)
        cols = jnp.dot(cols_ref[...], onehot, preferred_element_type=F32)
        scal = scal_ref[pl.ds(b, 1), :]
        sc = lambda idx: _col(scal, idx)
        m_row = m_ref[pl.ds(b, 1), :]
        rb = rows_b_ref[pl.ds(b, 1), :]
        rc = rows_c_ref[pl.ds(b, 1), :]
        n_all = n_ref[pl.ds(b, 1), :]
        m_new_row = jnp.zeros((1, 128), F32)
        o_rows, h_rows, n_rows = [], [], []
        for h in range(N_HEADS):
            sl = slice(h * HEAD_DIM, (h + 1) * HEAD_DIM)
            colblk = lambda v: cols[(v * N_HEADS + h) * HEAD_DIM:(v * N_HEADS + h + 1) * HEAD_DIM, :]
            s = s_ref[j, h]
            ks = jnp.sum(colblk(1) * s, axis=0, keepdims=True)
            qs = jnp.sum(colblk(0) * s, axis=0, keepdims=True)
            beta, eg, qk = sc(SC_BETA + h), sc(SC_EG + h), sc(SC_QKB + h)
            v_row = rb[:, 2 * GROUP_W + h * HEAD_DIM:2 * GROUP_W + (h + 1) * HEAD_DIM]
            v_new = beta * v_row - (beta * eg) * ks
            o_rows.append(eg * qs + qk * v_new)
            s_out_ref[j, h] = s * eg + colblk(1) * v_new
            cm = c_ref[j, h]
            qc = jnp.sum(colblk(2) * cm, axis=0, keepdims=True)
            q_row = rc[:, sl]
            k_row = rc[:, GROUP_W + h * HEAD_DIM:GROUP_W + (h + 1) * HEAD_DIM]
            vc_row = rc[:, 2 * GROUP_W + h * HEAD_DIM:2 * GROUP_W + (h + 1) * HEAD_DIM]
            n_row = n_all[:, sl]
            li, lf, qkc = sc(SC_LI + h), sc(SC_LF + h), sc(SC_QKC + h)
            m_old = _col(m_row, h)
            bb = lf + m_old
            m_t = jnp.maximum(bb, li)
            w_intra = jnp.exp(li - m_t) * qkc
            w_inter = jnp.exp(bb - m_t)
            num = w_inter * qc + w_intra * vc_row
            den = w_inter * jnp.sum(q_row * n_row, axis=-1, keepdims=True) + w_intra
            h_rows.append(num / jnp.maximum(jnp.abs(den), jnp.exp(-m_t)))
            m_new = jnp.maximum(m_old + lf, li)
            dec = jnp.exp(m_old + lf - m_new)
            e = jnp.exp(li - m_new)
            c_out_ref[j, h] = cm * dec + colblk(3) * (e * vc_row)
            n_rows.append(n_row * dec + e * k_row)
            m_new_row = jnp.where(lane_id == h, m_new, m_new_row)
        ob_ref[pl.ds(b, 1), :] = jnp.concatenate(o_rows, axis=1)
        hc_ref[pl.ds(b, 1), :] = jnp.concatenate(h_rows, axis=1)
        n_out_ref[pl.ds(b, 1), :] = jnp.concatenate(n_rows, axis=1)
        m_out_ref[pl.ds(b, 1), :] = m_new_row
        return carry

    lax.fori_loop(0, tb, body, 0, unroll=2)

    @pl.when(i == pl.num_programs(0) - 1)
    def _():
        ys_b, ys_c = [], []
        for h in range(N_HEADS):
            sl = slice(h * HEAD_DIM, (h + 1) * HEAD_DIM)
            o = ob_ref[:, sl]
            o = o * lax.rsqrt(jnp.mean(o * o, axis=-1, keepdims=True) + EPS) * dng_ref[...]
            ys_b.append(o * _silu(z_ref[:, sl]))
            hh = _sigmoid(og_ref[:, sl]) * hc_ref[:, sl]
            hh = hh * lax.rsqrt(jnp.mean(hh * hh, axis=-1, keepdims=True) + EPS) * mlg_ref[h:h + 1, :]
            ys_c.append(hh)
        yb_ref[...] = jnp.concatenate(ys_b, axis=1).astype(yb_ref.dtype)
        yc_ref[...] = jnp.concatenate(ys_c, axis=1).astype(yc_ref.dtype)


def sample_rec(cols, rows_b, rows_c, scal, P, n_st, m_st, dn_norm_g, ml_norm_g, s_all, c_all, s_acc, c_acc,
               layer, *, tb=8):
    nb = P.shape[0]
    W3 = 3 * GROUP_W
    full = lambda shape: pl.BlockSpec(shape, lambda i: (0,) * len(shape))
    state = pl.BlockSpec((None, tb, N_HEADS, HEAD_DIM, HEAD_DIM), lambda i: (layer, i, 0, 0, 0))
    untouched = pl.BlockSpec(memory_space=pl.ANY)
    return pl.pallas_call(
        functools.partial(_sample_rec_kernel, tb=tb),
        grid=(nb // tb,),
        in_specs=[full(cols.shape), full((nb, W3)), full((nb, W3)), full((nb, 128)),
                  pl.BlockSpec((nb, GROUP_W), lambda i: (0, COL_BZ)),
                  pl.BlockSpec((nb, GROUP_W), lambda i: (0, COL_CO)),
                  full((nb, GROUP_W)), full((nb, 128)), full((1, HEAD_DIM)), full((N_HEADS, HEAD_DIM)),
                  state, state, untouched, untouched],
        out_specs=[full((nb, GROUP_W)), full((nb, GROUP_W)), state, state,
                   full((nb, GROUP_W)), full((nb, 128))],
        out_shape=[jax.ShapeDtypeStruct((nb, GROUP_W), BF16),
                   jax.ShapeDtypeStruct((nb, GROUP_W), BF16),
                   jax.ShapeDtypeStruct(s_all.shape, F32),
                   jax.ShapeDtypeStruct(c_all.shape, F32),
                   jax.ShapeDtypeStruct((nb, GROUP_W), F32),
                   jax.ShapeDtypeStruct((nb, 128), F32)],
        input_output_aliases={12: 2, 13: 3},
        scratch_shapes=[pltpu.VMEM((nb, GROUP_W), F32), pltpu.VMEM((nb, GROUP_W), F32)],
        compiler_params=_cparams(("arbitrary",), VMEM_LIMIT),
    )(cols, rows_b, rows_c, scal, P, P, n_st, m_st, dn_norm_g.reshape(1, HEAD_DIM), ml_norm_g, s_all, c_all,
      s_acc, c_acc)


def _sample_xattn_kernel(q_ref, k_ref, v_ref, o_ref, *, tb):
    i = pl.program_id(0)
    n_rows = k_ref.shape[1]
    n_rep = n_rows // (2 * N_HEADS)
    ones = jnp.ones((HEAD_DIM, HEAD_DIM), BF16)
    fold = lambda x: x + pltpu.roll(x, N_HEADS, 0)

    def body(j, carry):
        b = i * tb + j
        q_row = q_ref[pl.ds(b, 1), :]
        heads = [q_row[:, h * HEAD_DIM:(h + 1) * HEAD_DIM] for h in range(N_HEADS)]
        q8 = jnp.concatenate(heads + heads, axis=0) * HEAD_DIM ** -0.5
        k3 = k_ref[j].reshape(n_rep, 2 * N_HEADS, HEAD_DIM)
        prod = (k3 * q8[None]).reshape(n_rows, HEAD_DIM)
        s = _bdot(prod, ones)
        s3 = s.reshape(n_rep, 2 * N_HEADS, HEAD_DIM)
        mx = jnp.max(s3, axis=0)
        mx = jnp.maximum(mx, pltpu.roll(mx, N_HEADS, 0))
        e3 = jnp.exp(s3 - mx[None])
        den = fold(jnp.sum(e3, axis=0))
        v3 = v_ref[j].reshape(n_rep, 2 * N_HEADS, HEAD_DIM)
        o8 = fold(jnp.sum(e3 * v3, axis=0)) / den
        o_ref[pl.ds(b, 1), :] = jnp.concatenate([o8[h:h + 1, :] for h in range(N_HEADS)], axis=1)
        return carry

    lax.fori_loop(0, tb, body, 0, unroll=4)


def sample_xattn(q, k_all, v_all, layer, *, tb=8):
    nb = q.shape[0]
    n_rows = k_all.shape[2]
    kv = pl.BlockSpec((None, tb, n_rows, HEAD_DIM), lambda i: (layer, i, 0, 0))
    return pl.pallas_call(
        functools.partial(_sample_xattn_kernel, tb=tb),
        grid=(nb // tb,),
        in_specs=[pl.BlockSpec((nb, GROUP_W), lambda i: (0, 0)), kv, kv],
        out_specs=pl.BlockSpec((nb, GROUP_W), lambda i: (0, 0)),
        out_shape=jax.ShapeDtypeStruct((nb, GROUP_W), F32),
        compiler_params=_cparams(("arbitrary",), VMEM_LIMIT),
    )(q, k_all, v_all)


W_B0 = 5 * GROUP_W
W_C0 = W_B0 + 2 * N_HEADS
W_C1 = W_C0 + 4 * GROUP_W
W_D0 = W_C1 + 2 * N_HEADS
W_D1 = W_D0 + 3 * GROUP_W


def _w_in_prep_kernel(w_ref, o_ref):
    tk = w_ref.shape[1]
    o_ref[:W_B0, :] = w_ref[:W_B0, :].astype(BF16)
    o_ref[W_B0:W_B0 + (W_C1 - W_C0), :] = w_ref[W_C0:W_C1, :].astype(BF16)
    o_ref[W_B0 + (W_C1 - W_C0):GATE_COL0, :] = w_ref[W_D0:W_D1, :].astype(BF16)
    gates = jnp.concatenate([w_ref[W_B0:W_C0, :], w_ref[W_C1:W_D0, :],
                             jnp.zeros((128 - 4 * N_HEADS, tk), F32)], axis=0)
    o_ref[GATE_COL0:, :] = gates.astype(BF16)


def _prep_w_in(w_in, *, tk=256):
    w_t = jnp.swapaxes(w_in, 1, 2)
    depth, n_in, K = w_t.shape
    assert n_in == W_D1
    return pl.pallas_call(
        _w_in_prep_kernel,
        grid=(depth, K // tk),
        in_specs=[pl.BlockSpec((None, n_in, tk), lambda l, i: (l, 0, i))],
        out_specs=pl.BlockSpec((None, IN_W_PAD, tk), lambda l, i: (l, 0, i)),
        out_shape=jax.ShapeDtypeStruct((depth, IN_W_PAD, K), BF16),
        compiler_params=_cparams(("parallel", "parallel"), VMEM_LIMIT),
    )(w_t)


def _ffn(x, g_pre, g_post, wg, wu, wd, layer, *, tm_up, tm_down, emit_bf16=False, up_layer=None):
    up = norm_matmul(x, g_pre, [wg, wu], layer if up_layer is None else up_layer,
                     tm=tm_up, tn=FFN_TN, out_dtype=BF16, emit_bf16=emit_bf16)
    act, cast = (up[0], tuple(up[1:])) if emit_bf16 else (up, ())
    y = matmul_resnorm([act], wd, layer, g_post, x, scale=0.5, tm=tm_down)
    return (y, cast) if emit_bf16 else y


def _prompt_layer(x, mem2d, B, T, lw, layer):
    g = lw['norm_g']
    x, cast1 = _ffn(x, g[0], g[1], lw['ffn1_wg'], lw['ffn1_wu'], lw['ffn1_wd'], layer,
                    tm_up=1024, tm_down=512, emit_bf16=True)
    P = norm_matmul(x, g[2], [lw['w_in']], layer, tm=1024, tn=IN_TN, out_dtype=F32, w_is_nk=True)
    ya, yd, yb, yc, pool_tail, sc_tail, dn_s, dn_tail, ml_c, ml_nm = mixers_prompt(P, B, T, lw)
    x = matmul_resnorm([ya, yb, yc, yd], lw['w_out'], layer, g[3], x, scale=1.0, tm=512)
    kv = norm_matmul(mem2d, g[8], [lw['x_wkv']], layer, tm=1024, tn=512, out_dtype=F32)
    x = xattn_prompt(x, B, T, g[4], lw['x_wq'], kv, lw['x_wo'], layer, g[5])
    x, cast2 = _ffn(x, g[6], g[7], lw['ffn2_wg'], lw['ffn2_wu'], lw['ffn2_wd'], layer,
                    tm_up=1024, tm_down=512, emit_bf16=True)
    states = (pool_tail[:, 16 - POOL_STATE:], dn_tail[:, 8 - (DN_CONV - 1):], dn_s, ml_c,
              ml_nm[:, :N_HEADS], ml_nm[:, N_HEADS:, 0], sc_tail[:, 8 - (SC_WIDTH - 1):])
    mem_k = kv[:, :GROUP_W].reshape(B, MEM_LEN, N_HEADS, HEAD_DIM)
    mem_v = kv[:, GROUP_W:].reshape(B, MEM_LEN, N_HEADS, HEAD_DIM)
    return x, states, mem_k, mem_v, cast1 + cast2


def _sample_layer(x, st, big, acc, layer, lw, start_pos):
    pool_st, dnc_st, ml_n, ml_m, scc_st = st
    s_all, c_all, k_all, v_all = big
    s_acc, c_acc = acc
    nb = x.shape[0]
    g = lw['norm_g']
    wg1, wu1, wg2, wu2 = (w[None] for w in lw['ffn_cast'])
    x = _ffn(x, g[0], g[1], wg1, wu1, lw['ffn1_wd'], layer, tm_up=128, tm_down=128, up_layer=0)
    P = norm_matmul(x, g[2], [lw['w_in']], layer, tm=128, tn=IN_TN, out_dtype=F32, w_is_nk=True)
    (ya, yd, pool_new, dnc_new, scc_new, rows_b, rows_c, scal, cols) = sample_pre(
        P, pool_st.reshape(nb, -1), dnc_st.reshape(nb, -1), scc_st.reshape(nb, -1),
        lw['pool_w'], lw['pool_scale'], lw['dn_conv_w'], lw['dn_A_log'], lw['dn_dt_bias'],
        lw['ml_b_i'], lw['ml_b_f'], lw['sc_conv_w'], start_pos)
    m_pad = jnp.pad(ml_m, ((0, 0), (0, 128 - N_HEADS)))
    yb, yc, s_acc, c_acc, n_new, m_new = sample_rec(
        cols, rows_b, rows_c, scal, P, ml_n.reshape(nb, GROUP_W), m_pad,
        lw['dn_norm_g'], lw['ml_norm_g'], s_all, c_all, s_acc, c_acc, layer)
    x = matmul_resnorm([ya, yb, yc, yd], lw['w_out'], layer, g[3], x, scale=1.0, tm=128)
    q = norm_matmul(x, g[4], [lw['x_wq']], layer, tm=128, tn=GROUP_W, out_dtype=F32)
    o = sample_xattn(q, k_all, v_all, layer)
    x = matmul_resnorm([o], lw['x_wo'], layer, g[5], x, scale=1.0, tm=128)
    x = _ffn(x, g[6], g[7], wg2, wu2, lw['ffn2_wd'], layer, tm_up=128, tm_down=128, up_layer=0)
    states = (pool_new.reshape(pool_st.shape), dnc_new.reshape(dnc_st.shape),
              n_new.reshape(ml_n.shape), m_new[:, :N_HEADS], scc_new.reshape(scc_st.shape))
    return x, states, (s_acc, c_acc)


def kernel(x_prompt, x_sample, mem_prompt, state_pool, state_dn_conv, state_dn_S, state_ml_C, state_ml_n,
           state_ml_m, state_sc_conv, cache_mem_k, cache_mem_v, norm_g, w_in, w_out, pool_w, pool_scale,
           dn_conv_w, dn_A_log, dn_dt_bias, dn_norm_g, ml_b_i, ml_b_f, ml_norm_g, sc_conv_w,
           x_wq, x_wk, x_wv, x_wo, ffn1_wg, ffn1_wu, ffn1_wd, ffn2_wg, ffn2_wu, ffn2_wd):
    depth = norm_g.shape[0]
    B, T, _ = x_prompt.shape
    nb, t_dec, _ = x_sample.shape
    assert t_dec == 1
    start_pos = PAST_LEN

    stacked = dict(w_in=_prep_w_in(w_in), w_out=w_out.astype(BF16), x_wq=x_wq.astype(BF16),
                   x_wkv=jnp.concatenate([x_wk.astype(BF16), x_wv.astype(BF16)], axis=-1),
                   x_wo=x_wo.astype(BF16),
                   ffn1_wg=ffn1_wg, ffn1_wu=ffn1_wu, ffn1_wd=ffn1_wd.astype(BF16),
                   ffn2_wg=ffn2_wg, ffn2_wu=ffn2_wu, ffn2_wd=ffn2_wd.astype(BF16))

    def layer_weights(l):
        return dict(stacked, norm_g=norm_g[l], pool_w=pool_w[l],
                    pool_scale=pool_scale[l], dn_conv_w=dn_conv_w[l], dn_A_log=dn_A_log[l],
                    dn_dt_bias=dn_dt_bias[l], dn_norm_g=dn_norm_g[l], ml_b_i=ml_b_i[l], ml_b_f=ml_b_f[l],
                    ml_norm_g=ml_norm_g[l], sc_conv_w=sc_conv_w[l])

    mem2d = mem_prompt.reshape(B * MEM_LEN, D_MODEL)
    h = x_prompt.reshape(B * T, D_MODEL)
    p_states, mem_k_list, mem_v_list, ffn_cast = [], [], [], []
    for l in range(depth):
        h, ns, mk, mv, cast = _prompt_layer(h, mem2d, B, T, layer_weights(l), l)
        ffn_cast.append(cast)
        p_states.append(ns)
        mem_k_list.append(mk)
        mem_v_list.append(mv)
    y_prompt = h.reshape(B, T, D_MODEL)

    s_inputs = (state_pool, state_dn_conv, state_ml_n, state_ml_m, state_sc_conv)
    big = (state_dn_S, state_ml_C,
           cache_mem_k.reshape(depth, nb, MEM_LEN * N_HEADS, HEAD_DIM),
           cache_mem_v.reshape(depth, nb, MEM_LEN * N_HEADS, HEAD_DIM))
    h = x_sample.reshape(nb, D_MODEL)
    s_states = []
    acc = (lax.empty(state_dn_S.shape, F32), lax.empty(state_ml_C.shape, F32))
    for l in range(depth):
        st = tuple(s[l] for s in s_inputs)
        h, ns, acc = _sample_layer(h, st, big, acc, l, dict(layer_weights(l), ffn_cast=ffn_cast[l]), start_pos)
        s_states.append(ns)
    y_sample = h.reshape(nb, 1, D_MODEL)
    dn_S_s, ml_C_s = acc

    pool_p, dn_conv_p, dn_S_p, ml_C_p, ml_n_p, ml_m_p, sc_conv_p = [jnp.stack(z) for z in zip(*p_states)]
    pool_s, dn_conv_s, ml_n_s, ml_m_s, sc_conv_s = [jnp.stack(z) for z in zip(*s_states)]
    mem_k_p = jnp.stack(mem_k_list)
    mem_v_p = jnp.stack(mem_v_list)
    return (y_prompt, y_sample, pool_p, pool_s, dn_conv_p, dn_conv_s, dn_S_p, dn_S_s, ml_C_p, ml_C_s,
            ml_n_p, ml_n_s, ml_m_p, ml_m_s, sc_conv_p, sc_conv_s, mem_k_p, mem_v_p)
```

```python
import functools

import numpy as np
import jax
import jax.numpy as jnp
from jax import lax
from jax.experimental import pallas as pl
from jax.experimental.pallas import tpu as pltpu

F32 = jnp.float32
BF16 = jnp.bfloat16

EPS = 1e-6
D_MODEL = 2048
GROUP_W = 512
HEAD_DIM = 128
N_HEADS = GROUP_W // HEAD_DIM
CHUNK = 64
POOL_WINDOWS = (2, 4, 8, 16)
POOL_STATE = 15
DN_CONV = 4
SC_WIDTH = 3
MEM_LEN = 256
PAST_LEN = 16384
FFN_TN = 512
COL_A, COL_BQ, COL_BK, COL_BV, COL_BZ = 0, 1, 2, 3, 4
COL_CQ, COL_CK, COL_CV, COL_CO = 5, 6, 7, 8
COL_DB, COL_DC, COL_DH = 9, 10, 11
GATE_COL0 = 12 * GROUP_W
IN_W_PAD = GATE_COL0 + 128
IN_TN = 1280
LANE_BETA, LANE_A, LANE_I, LANE_F = 0, 4, 8, 12
TB = 256
SUB = 128
VMEM_LIMIT = 60 * 2**20


def _cparams(sem, vmem=None):
    return pltpu.CompilerParams(dimension_semantics=sem, vmem_limit_bytes=vmem)


def _bdot(a, b):
    return jnp.dot(a.astype(BF16), b.astype(BF16), preferred_element_type=F32)


def _bdot_nt(a, b):
    return lax.dot_general(a.astype(BF16), b.astype(BF16), (((1,), (1,)), ((), ())),
                           preferred_element_type=F32)


def _sigmoid(x):
    return 0.5 * jnp.tanh(0.5 * x) + 0.5


def _silu(x):
    return x * _sigmoid(x)


def _softplus(x):
    return jnp.maximum(x, 0.0) + jnp.log(1.0 + jnp.exp(-jnp.abs(x)))


def _col(x, idx):
    return x[:, idx:idx + 1]


def _resident(shape):
    nd = len(shape)
    return pl.BlockSpec(shape, lambda *_: (0,) * nd, pipeline_mode=pl.Buffered(1))


def _norm_into(x_ref, g_ref, h_ref):
    tm = x_ref.shape[0]
    rs = min(tm, 128)

    def body(i, carry):
        r = pl.multiple_of(i * rs, rs)
        x = x_ref[pl.ds(r, rs), :]
        ms = jnp.mean(x * x, axis=-1, keepdims=True)
        h_ref[pl.ds(r, rs), :] = (x * lax.rsqrt(ms + EPS) * g_ref[...]).astype(BF16)
        return carry

    lax.fori_loop(0, tm // rs, body, 0)


def _zero_past_width(y, n_valid):
    tn = y.shape[1]
    if n_valid % tn == 0:
        return y
    col = pl.program_id(1) * tn + lax.broadcasted_iota(jnp.int32, (1, tn), 1)
    return jnp.where(col < n_valid, y, 0.0)


def _norm_mm_kernel(x_ref, g_ref, w_ref, o_ref, h_ref, *, n_valid, w_is_nk):
    @pl.when(pl.program_id(1) == 0)
    def _():
        _norm_into(x_ref, g_ref, h_ref)

    if w_is_nk:
        y = lax.dot_general(h_ref[...], w_ref[...], (((1,), (1,)), ((), ())), preferred_element_type=F32)
    else:
        y = jnp.dot(h_ref[...], w_ref[...], preferred_element_type=F32)
    o_ref[...] = _zero_past_width(y, n_valid).astype(o_ref.dtype)


def _norm_swiglu_kernel(x_ref, g_ref, wg_ref, wu_ref, *rest, n_valid, emit_bf16, has_out_into):
    rest = rest[1:] if has_out_into else rest
    o_ref, h_ref = rest[0], rest[-1]

    @pl.when(pl.program_id(1) == 0)
    def _():
        _norm_into(x_ref, g_ref, h_ref)

    h = h_ref[...]
    wg = wg_ref[...].astype(BF16)
    wu = wu_ref[...].astype(BF16)
    a = jnp.dot(h, wg, preferred_element_type=F32)
    b = jnp.dot(h, wu, preferred_element_type=F32)
    o_ref[...] = _zero_past_width(_silu(a) * b, n_valid).astype(o_ref.dtype)
    if emit_bf16:
        wg_out_ref, wu_out_ref = rest[1:3]
        wg_out_ref[...] = wg
        wu_out_ref[...] = wu


def norm_matmul(x, g, ws, layer, *, tm, tn, out_dtype, w_is_nk=False, emit_bf16=False, row_tiles=None,
                out_into=None):
    M, K = x.shape
    n_valid = ws[0].shape[1 if w_is_nk else 2]
    n_steps = pl.cdiv(n_valid, tn)
    N = n_steps * tn
    tm = min(tm, M)
    first, count = row_tiles if row_tiles is not None else (0, M // tm)
    if len(ws) == 2:
        assert not w_is_nk and (count == 1 or not emit_bf16)
        kern = functools.partial(_norm_swiglu_kernel, n_valid=n_valid, emit_bf16=emit_bf16,
                                 has_out_into=out_into is not None)
    else:
        assert not emit_bf16 and out_into is None
        kern = functools.partial(_norm_mm_kernel, n_valid=n_valid, w_is_nk=w_is_nk)
    w_spec = (pl.BlockSpec((None, tn, K), lambda i, j: (layer, j, 0)) if w_is_nk
              else pl.BlockSpec((None, K, tn), lambda i, j: (layer, 0, j)))
    out_specs = pl.BlockSpec((tm, tn), lambda i, j: (i + first, j))
    out_shape = jax.ShapeDtypeStruct((M, N), out_dtype)
    if emit_bf16:
        w_out = pl.BlockSpec((K, tn), lambda i, j: (0, j))
        out_specs = [out_specs, w_out, w_out]
        out_shape = [out_shape] + [jax.ShapeDtypeStruct((K, n_valid), BF16)] * 2
    extra_in, extra_specs, aliases = (), [], {}
    if out_into is not None:
        extra_in, extra_specs = (out_into,), [pl.BlockSpec(memory_space=pl.ANY)]
        aliases = {2 + len(ws): 0}
    return pl.pallas_call(
        kern,
        grid=(count, n_steps),
        in_specs=[pl.BlockSpec((tm, K), lambda i, j: (i + first, 0)),
                  pl.BlockSpec((1, K), lambda i, j: (0, 0))]
        + [w_spec for _ in ws] + extra_specs,
        out_specs=out_specs,
        out_shape=out_shape,
        input_output_aliases=aliases,
        scratch_shapes=[pltpu.VMEM((tm, K), BF16)],
        compiler_params=_cparams(("parallel", "arbitrary"), VMEM_LIMIT),
    )(x, g.reshape(1, K), *ws, *extra_in)


def _mm_resnorm_kernel(*refs, n_a, scale):
    a_refs = refs[:n_a]
    w_ref, g_ref, res_ref, o_ref = refs[n_a:]
    parts = [r[...].astype(BF16) for r in a_refs]
    a = parts[0] if n_a == 1 else jnp.concatenate(parts, axis=1)
    y = jnp.dot(a, w_ref[...], preferred_element_type=F32)
    ms = jnp.mean(y * y, axis=-1, keepdims=True)
    o_ref[...] = res_ref[...] + scale * (y * lax.rsqrt(ms + EPS) * g_ref[...])


def matmul_resnorm(a_list, w, layer, g, res, *, scale, tm):
    M, N = res.shape
    tm = min(tm, M)
    K = w.shape[1]
    widths = [K] if len(a_list) == 1 else [a.shape[1] for a in a_list]
    assert sum(widths) == K
    w_spec = pl.BlockSpec((None, K, N), lambda i: (layer, 0, 0), pipeline_mode=pl.Buffered(1))
    return pl.pallas_call(
        functools.partial(_mm_resnorm_kernel, n_a=len(a_list), scale=scale),
        grid=(M // tm,),
        in_specs=[pl.BlockSpec((tm, wd), lambda i: (i, 0)) for wd in widths]
        + [w_spec, _resident((1, N)), pl.BlockSpec((tm, N), lambda i: (i, 0))],
        out_specs=pl.BlockSpec((tm, N), lambda i: (i, 0)),
        out_shape=jax.ShapeDtypeStruct((M, N), F32),
        compiler_params=_cparams(("parallel",), VMEM_LIMIT),
    )(*a_list, w, g.reshape(1, N), res)


def _chunk_cumsum(x):
    n = x.shape[0]
    row = lax.broadcasted_iota(jnp.int32, (n, 1), 0) % CHUNK
    s = 1
    while s < CHUNK:
        x = x + jnp.where(row >= s, pltpu.roll(x, s, 0), 0.0)
        s *= 2
    return x


def _chunk_last(x):
    n = x.shape[0]
    parts = [jnp.broadcast_to(x[c * CHUNK + CHUNK - 1:c * CHUNK + CHUNK, :], (CHUNK, x.shape[1]))
             for c in range(n // CHUNK)]
    return jnp.concatenate(parts, axis=0)


def _pad_rows(x, c, n_chunks):
    z = jnp.zeros_like(x)
    return jnp.concatenate([x if i == c else z for i in range(n_chunks)], axis=0)


def _blk_masks(n):
    r = lax.broadcasted_iota(jnp.int32, (n, n), 0)
    c = lax.broadcasted_iota(jnp.int32, (n, n), 1)
    same = (r ^ c) < CHUNK
    return same & (c <= r), same & (c < r), r == c


def _split(a):
    ah = a.astype(BF16)
    return ah, (a - ah.astype(F32)).astype(BF16)


def _split_dot(a, b):
    d = lambda x, y: jnp.dot(x, y, preferred_element_type=F32)
    return d(a[0], b[0]) + d(a[0], b[1]) + d(a[1], b[0])


def _unit_lower_inverses(a_list, n):
    r = lax.broadcasted_iota(jnp.int32, (n, n), 0)
    c = lax.broadcasted_iota(jnp.int32, (n, n), 1)
    rc = r ^ c
    pair = lambda s: (rc >= s) & (rc < 2 * s)
    eye = (r == c).astype(F32)
    xs = [eye - jnp.where(pair(1), a, 0.0) for a in a_list]
    s = 2
    while s < CHUNK:
        m = pair(s)
        x_sp = [_split(x) for x in xs]
        ts = [_split_dot(_split(jnp.where(m, a, 0.0)), x) for a, x in zip(a_list, x_sp)]
        xs = [x - _split_dot(xp, _split(t)) for x, xp, t in zip(xs, x_sp, ts)]
        s *= 2
    return xs


def _pool_sconv_step(ua_ref, db_ref, dc_ref, dh_ref, pw_ref, ps_ref, sw_ref, ya_ref, yd_ref, hist_a, hist_d,
                     *, tb):
    i = pl.program_id(1)

    u = ua_ref[...]
    ext = jnp.concatenate([hist_a[...], u], axis=0)
    a2 = ext + pltpu.roll(ext, 1, 0)
    a4 = a2[:, 128:] + pltpu.roll(a2[:, 128:], 2, 0)
    a8 = a4[:, 128:] + pltpu.roll(a4[:, 128:], 4, 0)
    a16 = a8[:, 128:] + pltpu.roll(a8[:, 128:], 8, 0)
    sums = (a2[16:, :128], a4[16:, :128], a8[16:, :128], a16[16:, :])
    pos = i * tb + lax.broadcasted_iota(jnp.int32, (tb, 1), 0)
    ys = []
    for g, w in enumerate(POOL_WINDOWS):
        cnt = jnp.minimum(pos + 1, w).astype(F32)
        d = sums[g] / cnt - u[:, g * 128:(g + 1) * 128]
        ys.append(_bdot(d, pw_ref[g]))
    ya_ref[...] = (jnp.concatenate(ys, axis=1) * ps_ref[...]).astype(ya_ref.dtype)
    hist_a[...] = u[tb - 16:, :]

    us = dc_ref[...] * dh_ref[...]
    ext = jnp.concatenate([hist_d[...], us], axis=0)
    sw = sw_ref[...]
    y = sw[0:1] * pltpu.roll(ext, 2, 0) + sw[1:2] * pltpu.roll(ext, 1, 0) + sw[2:3] * ext
    yd_ref[...] = (db_ref[...] * y[8:]).astype(yd_ref.dtype)
    hist_d[...] = us[tb - 8:, :]
    return u[tb - 16:, :], us[tb - 8:, :]


def _deltanet_step(q_ref, k_ref, v_ref, z_ref, gt_ref, cw_ref, alog_ref, dtb_ref, ng_ref, y_ref, s_ref, hist_ref,
                   *, tb):
    nc = tb // CHUNK

    qkv = jnp.concatenate([q_ref[...], k_ref[...], v_ref[...]], axis=1)
    ext = jnp.concatenate([hist_ref[...], qkv], axis=0)
    cw = cw_ref[...]
    conv = (cw[0:1] * pltpu.roll(ext, 3, 0) + cw[1:2] * pltpu.roll(ext, 2, 0)
            + cw[2:3] * pltpu.roll(ext, 1, 0) + cw[3:4] * ext)[8:]
    act = _silu(conv)
    hist_ref[...] = qkv[tb - 8:, :]

    gates = gt_ref[...]
    beta_all = _sigmoid(gates)
    g_all = -jnp.exp(alog_ref[...]) * _softplus(gates + dtb_ref[...])
    gcum_all = _chunk_cumsum(g_all)
    glast_all = _chunk_last(gcum_all)
    gcum_t = gcum_all.T
    causal, strict, _ = _blk_masks(SUB)

    heads = range(N_HEADS)
    subs = range(tb // SUB)
    cps = SUB // CHUNK
    hd = lambda base, h: slice(base + h * HEAD_DIM, base + (h + 1) * HEAD_DIM)
    qs, ks, gcs, gls, betas = [], [], [], [], []
    for h in heads:
        qh = act[:, hd(0, h)]
        kh = act[:, hd(GROUP_W, h)]
        qs.append(qh * lax.rsqrt(jnp.sum(qh * qh, axis=-1, keepdims=True) + EPS) * HEAD_DIM ** -0.5)
        ks.append(kh * lax.rsqrt(jnp.sum(kh * kh, axis=-1, keepdims=True) + EPS))
        betas.append(_col(beta_all, LANE_BETA + h))
        gcs.append(_col(gcum_all, LANE_A + h))
        gls.append(_col(glast_all, LANE_A + h))
    k_ts, a_mats, qks = {}, [], {}
    for st in subs:
        R = slice(st * SUB, (st + 1) * SUB)
        for h in heads:
            gr = gcum_t[LANE_A + h:LANE_A + h + 1, R]
            decay = jnp.where(causal, jnp.exp(jnp.where(causal, gcs[h][R] - gr, 0.0)), 0.0)
            k_t = ks[h][R].T.astype(BF16)
            a_mats.append(jnp.where(strict, _bdot(ks[h][R] * betas[h][R], k_t) * decay, 0.0))
            qks[st, h] = jnp.where(causal, _bdot(qs[h][R], k_t) * decay, 0.0).astype(BF16)
            k_ts[st, h] = k_t
    tinvs = _unit_lower_inverses(a_mats, SUB)
    us, ws = {}, {}
    for st in subs:
        R = slice(st * SUB, (st + 1) * SUB)
        for h in heads:
            vb = act[R, hd(2 * GROUP_W, h)] * betas[h][R]
            kbe = ks[h][R] * (betas[h][R] * jnp.exp(gcs[h][R]))
            uw = _bdot(tinvs[st * N_HEADS + h], jnp.concatenate([vb, kbe], axis=1))
            us[st, h] = uw[:, :HEAD_DIM]
            ws[st, h] = uw[:, HEAD_DIM:].astype(BF16)
    q_decs = [(qs[h] * jnp.exp(gcs[h])).astype(BF16) for h in heads]
    tails = [jnp.exp(gls[h] - gcs[h]) for h in heads]
    s_hs = [s_ref[h] for h in heads]
    outs = [[] for _ in heads]
    for c in range(nc):
        r = slice(c * CHUNK, (c + 1) * CHUNK)
        st, cl = divmod(c, cps)
        rl = slice(cl * CHUNK, (cl + 1) * CHUNK)
        for h in heads:
            s_b = s_hs[h].astype(BF16)
            v_new = us[st, h][rl] - jnp.dot(ws[st, h][rl], s_b, preferred_element_type=F32)
            outs[h].append(jnp.dot(q_decs[h][r], s_b, preferred_element_type=F32)
                           + _bdot(qks[st, h][rl], _pad_rows(v_new, cl, cps)))
            s_dec = jnp.exp(jnp.broadcast_to(gls[h][c * CHUNK:c * CHUNK + 1], (HEAD_DIM, 1)))
            s_hs[h] = s_hs[h] * s_dec + _bdot(k_ts[st, h], _pad_rows(v_new * tails[h][r], cl, cps))
    ys = []
    for h in heads:
        s_ref[h] = s_hs[h]
        o = jnp.concatenate(outs[h], axis=0)
        o = o * lax.rsqrt(jnp.mean(o * o, axis=-1, keepdims=True) + EPS) * ng_ref[...]
        ys.append(o * _silu(z_ref[:, hd(0, h)]))
    y_ref[...] = jnp.concatenate(ys, axis=1).astype(y_ref.dtype)
    return qkv[tb - 8:, :]


def _mlstm_step(q_ref, k_ref, v_ref, og_ref, gt_ref, bi_ref, bf_ref, ng_ref, y_ref, c_ref, nm_ref, *, tb):
    nc = tb // CHUNK

    gates = gt_ref[...]
    li_all = gates + bi_ref[...]
    lf_all = -_softplus(-(gates + bf_ref[...]))
    f_all = _chunk_cumsum(lf_all)
    flast_all = _chunk_last(f_all)
    f_t = f_all.T
    li_t = li_all.T
    causal, _, _ = _blk_masks(tb)

    heads = range(N_HEADS)
    hd = lambda h: slice(h * HEAD_DIM, (h + 1) * HEAD_DIM)
    qs, kss, k_ts, evs, eks, intras, den_intras, w_inters, inv_floor, decs = ([] for _ in range(10))
    for h in heads:
        qh = q_ref[:, hd(h)]
        kh = k_ref[:, hd(h)] * HEAD_DIM ** -0.5
        vh = v_ref[:, hd(h)]
        fc = _col(f_all, LANE_F + h)
        fl = _col(flast_all, LANE_F + h)
        a_w = fl - fc + _col(li_all, LANE_I + h)
        fr = f_t[LANE_F + h:LANE_F + h + 1, :]
        lir = li_t[LANE_I + h:LANE_I + h + 1, :]
        dm = jnp.where(causal, fc - fr + lir, -jnp.inf)
        mx = jnp.max(dm, axis=1, keepdims=True)
        m_h = nm_ref[N_HEADS + h:N_HEADS + h + 1, 0:1]
        m_prev, m_next, dec_h = [], [], []
        for c in range(nc):
            r = slice(c * CHUNK, (c + 1) * CHUNK)
            fl_c = fl[c * CHUNK:c * CHUNK + 1]
            m_new = jnp.maximum(m_h + fl_c, jnp.max(a_w[r], axis=0, keepdims=True))
            dec_h.append(jnp.exp(m_h + fl_c - m_new))
            m_prev.append(jnp.broadcast_to(m_h, (CHUNK, 1)))
            m_next.append(jnp.broadcast_to(m_new, (CHUNK, 1)))
            m_h = m_new
        nm_ref[N_HEADS + h:N_HEADS + h + 1, :] = jnp.broadcast_to(m_h, (1, HEAD_DIM))
        b = fc + jnp.concatenate(m_prev, axis=0)
        m_t = jnp.maximum(b, mx)
        k_t = kh.T.astype(BF16)
        w_intra = jnp.exp(dm - m_t) * _bdot(qh, k_t)
        e = jnp.exp(a_w - jnp.concatenate(m_next, axis=0))
        qs.append(qh)
        k_ts.append(k_t)
        evs.append(e * vh)
        eks.append(e * kh)
        intras.append(_bdot(w_intra, vh))
        den_intras.append(jnp.sum(w_intra, axis=-1, keepdims=True))
        w_inters.append(jnp.exp(b - m_t))
        inv_floor.append(jnp.exp(-m_t))
        decs.append(dec_h)
    c_hs = [c_ref[h] for h in heads]
    n_hs = [nm_ref[h:h + 1, :] for h in heads]
    outs = [[] for _ in heads]
    for c in range(nc):
        r = slice(c * CHUNK, (c + 1) * CHUNK)
        for h in heads:
            q_c = qs[h][r]
            num = w_inters[h][r] * _bdot(q_c, c_hs[h]) + intras[h][r]
            den = w_inters[h][r] * jnp.sum(q_c * n_hs[h], axis=-1, keepdims=True) + den_intras[h][r]
            outs[h].append(num / jnp.maximum(jnp.abs(den), inv_floor[h][r]))
            c_hs[h] = c_hs[h] * decs[h][c] + _bdot(k_ts[h], _pad_rows(evs[h][r], c, nc))
            n_hs[h] = n_hs[h] * decs[h][c] + jnp.sum(eks[h][r], axis=0, keepdims=True)
    ys = []
    for h in heads:
        c_ref[h] = c_hs[h]
        nm_ref[h:h + 1, :] = n_hs[h]
        hh = jnp.concatenate(outs[h], axis=0)
        hh = _sigmoid(og_ref[:, hd(h)]) * hh
        hh = hh * lax.rsqrt(jnp.mean(hh * hh, axis=-1, keepdims=True) + EPS) * ng_ref[h:h + 1, :]
        ys.append(hh)
    y_ref[...] = jnp.concatenate(ys, axis=1).astype(y_ref.dtype)


N_POOL_IN, N_DN_IN, N_ML_IN = 7, 9, 8


def _mixers_kernel(*refs, tb):
    ins = iter(refs[:N_POOL_IN + N_DN_IN + N_ML_IN])
    take = lambda n: [next(ins) for _ in range(n)]
    pool_in, dn_in, ml_in = take(N_POOL_IN), take(N_DN_IN), take(N_ML_IN)
    (ya_ref, yd_ref, yb_ref, yc_ref, ptail_ref, stail_ref, s_out_ref, ctail_ref, c_out_ref, nm_out_ref,
     hist_a, hist_d, s_ref, hist_b, c_ref, nm_ref) = refs[N_POOL_IN + N_DN_IN + N_ML_IN:]
    i = pl.program_id(1)

    @pl.when(i == 0)
    def _():
        for r in (hist_a, hist_d, s_ref, hist_b, c_ref, nm_ref):
            r[...] = jnp.zeros_like(r)

    qkv_tail = _deltanet_step(*dn_in, yb_ref, s_ref, hist_b, tb=tb)
    _mlstm_step(*ml_in, yc_ref, c_ref, nm_ref, tb=tb)
    u_tail, us_tail = _pool_sconv_step(*pool_in, ya_ref, yd_ref, hist_a, hist_d, tb=tb)

    @pl.when(i == pl.num_programs(1) - 1)
    def _():
        ptail_ref[0] = u_tail
        stail_ref[0] = us_tail
        s_out_ref[0] = s_ref[...]
        ctail_ref[0] = qkv_tail
        c_out_ref[0] = c_ref[...]
        nm_out_ref[0] = nm_ref[...]


def mixers_prompt(P, B, T, lw):
    nT = T // TB
    M = B * T
    col = lambda c: pl.BlockSpec((TB, GROUP_W), lambda b, i: (b * nT + i, c))
    gates = pl.BlockSpec((TB, 128), lambda b, i: (b * nT + i, GATE_COL0 // 128))
    const = lambda *shape: pl.BlockSpec(shape, lambda b, i: (0,) * len(shape))
    lane_row = lambda v, lane: jnp.zeros((1, 128), F32).at[0, lane:lane + N_HEADS].set(v)
    y_spec = pl.BlockSpec((TB, GROUP_W), lambda b, i: (b * nT + i, 0))
    per_seq = lambda *shape: pl.BlockSpec((1,) + shape, lambda b, i: (b,) + (0,) * len(shape))
    mat = (N_HEADS, HEAD_DIM, HEAD_DIM)
    pool_specs = [col(COL_A), col(COL_DB), col(COL_DC), col(COL_DH),
                  const(4, 128, 128), const(1, GROUP_W), const(SC_WIDTH, GROUP_W)]
    dn_specs = [col(COL_BQ), col(COL_BK), col(COL_BV), col(COL_BZ), gates,
                const(DN_CONV, 3 * GROUP_W), const(1, 128), const(1, 128), const(1, HEAD_DIM)]
    ml_specs = [col(COL_CQ), col(COL_CK), col(COL_CV), col(COL_CO), gates,
                const(1, 128), const(1, 128), const(N_HEADS, HEAD_DIM)]
    assert (len(pool_specs), len(dn_specs), len(ml_specs)) == (N_POOL_IN, N_DN_IN, N_ML_IN)
    y_shape = jax.ShapeDtypeStruct((M, GROUP_W), BF16)
    state = lambda *shape: jax.ShapeDtypeStruct((B,) + shape, F32)
    return pl.pallas_call(
        functools.partial(_mixers_kernel, tb=TB),
        grid=(B, nT),
        in_specs=pool_specs + dn_specs + ml_specs,
        out_specs=[y_spec] * 4 + [per_seq(16, GROUP_W), per_seq(8, GROUP_W), per_seq(*mat),
                                  per_seq(8, 3 * GROUP_W), per_seq(*mat), per_seq(2 * N_HEADS, HEAD_DIM)],
        out_shape=[y_shape] * 4 + [state(16, GROUP_W), state(8, GROUP_W), state(*mat),
                                   state(8, 3 * GROUP_W), state(*mat), state(2 * N_HEADS, HEAD_DIM)],
        scratch_shapes=[pltpu.VMEM((16, GROUP_W), F32), pltpu.VMEM((8, GROUP_W), F32),
                        pltpu.VMEM(mat, F32), pltpu.VMEM((8, 3 * GROUP_W), F32),
                        pltpu.VMEM(mat, F32), pltpu.VMEM((2 * N_HEADS, HEAD_DIM), F32)],
        compiler_params=_cparams(("parallel", "arbitrary"), VMEM_LIMIT),
    )(P, P, P, P, lw['pool_w'], lw['pool_scale'].reshape(1, GROUP_W), lw['sc_conv_w'],
      P, P, P, P, P, lw['dn_conv_w'], lane_row(lw['dn_A_log'], LANE_A), lane_row(lw['dn_dt_bias'], LANE_A),
      lw['dn_norm_g'].reshape(1, HEAD_DIM),
      P, P, P, P, P, lane_row(lw['ml_b_i'], LANE_I), lane_row(lw['ml_b_f'], LANE_F), lw['ml_norm_g'])


def _xattn_prompt_kernel(x_ref, gpre_ref, wq_ref, kv_ref, wo_ref, gpost_ref, o_ref, h_ref):
    _norm_into(x_ref, gpre_ref, h_ref)
    q = jnp.dot(h_ref[...], wq_ref[...], preferred_element_type=F32)
    heads = range(N_HEADS)
    hd = lambda base, h: slice(base + h * HEAD_DIM, base + (h + 1) * HEAD_DIM)
    ss = [_bdot_nt(q[:, hd(0, h)], kv_ref[:, hd(0, h)]) * HEAD_DIM ** -0.5 for h in heads]
    es = [jnp.exp(s - jnp.max(s, axis=-1, keepdims=True)) for s in ss]
    outs = [_bdot(e, kv_ref[:, hd(GROUP_W, h)]) / jnp.sum(e, axis=-1, keepdims=True)
            for h, e in zip(heads, es)]
    o = jnp.concatenate(outs, axis=1).astype(BF16)
    y = jnp.dot(o, wo_ref[...], preferred_element_type=F32)
    ms = jnp.mean(y * y, axis=-1, keepdims=True)
    o_ref[...] = x_ref[...] + y * lax.rsqrt(ms + EPS) * gpost_ref[...]


def xattn_prompt(x, B, T, g_pre, wq, kv, wo, layer, g_post, *, tq=512):
    nq = T // tq
    M = B * T
    return pl.pallas_call(
        _xattn_prompt_kernel,
        grid=(B, nq),
        in_specs=[pl.BlockSpec((tq, D_MODEL), lambda b, i: (b * nq + i, 0)),
                  pl.BlockSpec((1, D_MODEL), lambda b, i: (0, 0)),
                  pl.BlockSpec((None, D_MODEL, GROUP_W), lambda b, i: (layer, 0, 0)),
                  pl.BlockSpec((MEM_LEN, 2 * GROUP_W), lambda b, i: (b, 0)),
                  pl.BlockSpec((None, GROUP_W, D_MODEL), lambda b, i: (layer, 0, 0)),
                  pl.BlockSpec((1, D_MODEL), lambda b, i: (0, 0))],
        out_specs=pl.BlockSpec((tq, D_MODEL), lambda b, i: (b * nq + i, 0)),
        out_shape=jax.ShapeDtypeStruct((M, D_MODEL), F32),
        scratch_shapes=[pltpu.VMEM((tq, D_MODEL), BF16)],
        compiler_params=_cparams(("parallel", "parallel"), VMEM_LIMIT),
    )(x, g_pre.reshape(1, D_MODEL), wq, kv, wo, g_post.reshape(1, D_MODEL))


SC_BETA, SC_EG, SC_QKB, SC_LI, SC_LF, SC_QKC = 0, 4, 8, 12, 16, 20


def _sample_pre_kernel(p_ref, pool_ref, dnc_ref, scc_ref, pw_ref, ps_ref, cw_ref, alog_ref, dtb_ref,
                       bi_ref, bf_ref, sw_ref,
                       ya_ref, yd_ref, pool_out_ref, dnc_out_ref, scc_out_ref,
                       rows_b_ref, rows_c_ref, scal_ref, cols_ref, *, start_pos):
    nb = p_ref.shape[0]
    blk = lambda c: p_ref[:, c * GROUP_W:(c + 1) * GROUP_W]

    u = blk(COL_A)
    ys = []
    for g, w in enumerate(POOL_WINDOWS):
        lanes = slice(g * 128, (g + 1) * 128)
        tot = u[:, lanes]
        for r in range(POOL_STATE + 1 - w, POOL_STATE):
            tot = tot + pool_ref[:, r * GROUP_W + g * 128:r * GROUP_W + (g + 1) * 128]
        d = tot / float(min(start_pos + 1, w)) - u[:, lanes]
        ys.append(_bdot(d, pw_ref[g]))
    ya_ref[...] = (jnp.concatenate(ys, axis=1) * ps_ref[...]).astype(ya_ref.dtype)
    pool_out_ref[:, :(POOL_STATE - 1) * GROUP_W] = pool_ref[:, GROUP_W:]
    pool_out_ref[:, (POOL_STATE - 1) * GROUP_W:] = u

    us = blk(COL_DC) * blk(COL_DH)
    sw = sw_ref[...]
    y = sw[0:1] * scc_ref[:, :GROUP_W] + sw[1:2] * scc_ref[:, GROUP_W:] + sw[2:3] * us
    yd_ref[...] = (blk(COL_DB) * y).astype(yd_ref.dtype)
    scc_out_ref[:, :GROUP_W] = scc_ref[:, GROUP_W:]
    scc_out_ref[:, GROUP_W:] = us

    qkv = p_ref[:, COL_BQ * GROUP_W:(COL_BV + 1) * GROUP_W]
    cw = cw_ref[...]
    W3 = 3 * GROUP_W
    conv = cw[DN_CONV - 1:DN_CONV] * qkv
    for j in range(DN_CONV - 1):
        conv = conv + cw[j:j + 1] * dnc_ref[:, j * W3:(j + 1) * W3]
    act = _silu(conv)
    dnc_out_ref[:, :(DN_CONV - 2) * W3] = dnc_ref[:, W3:]
    dnc_out_ref[:, (DN_CONV - 2) * W3:] = qkv

    gates = p_ref[:, GATE_COL0:GATE_COL0 + 128]
    beta_all = _sigmoid(gates)
    eg_all = jnp.exp(-jnp.exp(alog_ref[...]) * _softplus(gates + dtb_ref[...]))
    li_all = gates + bi_ref[...]
    lf_all = -_softplus(-(gates + bf_ref[...]))

    lane = lax.broadcasted_iota(jnp.int32, (nb, 128), 1)
    scal = jnp.zeros((nb, 128), F32)

    def put(tab, lane_idx, colv):
        return jnp.where(lane == lane_idx, colv, tab)

    qs, ks = [], []
    for h in range(N_HEADS):
        sl = slice(h * HEAD_DIM, (h + 1) * HEAD_DIM)
        qh = act[:, sl]
        kh = act[:, GROUP_W + h * HEAD_DIM:GROUP_W + (h + 1) * HEAD_DIM]
        qh = qh * lax.rsqrt(jnp.sum(qh * qh, axis=-1, keepdims=True) + EPS) * HEAD_DIM ** -0.5
        kh = kh * lax.rsqrt(jnp.sum(kh * kh, axis=-1, keepdims=True) + EPS)
        qs.append(qh)
        ks.append(kh)
        scal = put(scal, SC_BETA + h, _col(beta_all, LANE_BETA + h))
        scal = put(scal, SC_EG + h, _col(eg_all, LANE_A + h))
        scal = put(scal, SC_QKB + h, jnp.sum(qh * kh, axis=-1, keepdims=True))
        cols_ref[(0 * N_HEADS + h) * HEAD_DIM:(0 * N_HEADS + h + 1) * HEAD_DIM, :] = qh.T.astype(BF16)
        cols_ref[(1 * N_HEADS + h) * HEAD_DIM:(1 * N_HEADS + h + 1) * HEAD_DIM, :] = kh.T.astype(BF16)
    rows_b_ref[...] = jnp.concatenate(qs + ks + [act[:, 2 * GROUP_W:]], axis=1)

    qc = blk(COL_CQ)
    kc = blk(COL_CK) * HEAD_DIM ** -0.5
    for h in range(N_HEADS):
        sl = slice(h * HEAD_DIM, (h + 1) * HEAD_DIM)
        scal = put(scal, SC_LI + h, _col(li_all, LANE_I + h))
        scal = put(scal, SC_LF + h, _col(lf_all, LANE_F + h))
        scal = put(scal, SC_QKC + h, jnp.sum(qc[:, sl] * kc[:, sl], axis=-1, keepdims=True))
        cols_ref[(2 * N_HEADS + h) * HEAD_DIM:(2 * N_HEADS + h + 1) * HEAD_DIM, :] = qc[:, sl].T.astype(BF16)
        cols_ref[(3 * N_HEADS + h) * HEAD_DIM:(3 * N_HEADS + h + 1) * HEAD_DIM, :] = kc[:, sl].T.astype(BF16)
    rows_c_ref[...] = jnp.concatenate([qc, kc, blk(COL_CV)], axis=1)
    scal_ref[...] = scal


def sample_pre(P, pool_st, dnc_st, scc_st, pool_w, pool_scale, conv_w, a_log, dt_bias, b_i, b_f, sc_w,
               start_pos):
    nb = P.shape[0]
    lane_row = lambda v, lane: jnp.zeros((1, 128), F32).at[0, lane:lane + N_HEADS].set(v)
    W3 = 3 * GROUP_W
    out_shape = [jax.ShapeDtypeStruct((nb, GROUP_W), BF16),
                 jax.ShapeDtypeStruct((nb, GROUP_W), BF16),
                 jax.ShapeDtypeStruct(pool_st.shape, F32),
                 jax.ShapeDtypeStruct(dnc_st.shape, F32),
                 jax.ShapeDtypeStruct(scc_st.shape, F32),
                 jax.ShapeDtypeStruct((nb, W3), F32),
                 jax.ShapeDtypeStruct((nb, W3), F32),
                 jax.ShapeDtypeStruct((nb, 128), F32),
                 jax.ShapeDtypeStruct((4 * N_HEADS * HEAD_DIM, nb), BF16)]
    return pl.pallas_call(
        functools.partial(_sample_pre_kernel, start_pos=start_pos),
        out_shape=out_shape,
        compiler_params=pltpu.CompilerParams(vmem_limit_bytes=VMEM_LIMIT),
    )(P, pool_st, dnc_st, scc_st, pool_w, pool_scale.reshape(1, GROUP_W), conv_w,
      lane_row(a_log, LANE_A), lane_row(dt_bias, LANE_A), lane_row(b_i, LANE_I), lane_row(b_f, LANE_F), sc_w)


def _sample_rec_kernel(cols_ref, rows_b_ref, rows_c_ref, scal_ref, z_ref, og_ref, n_ref, m_ref,
                       dng_ref, mlg_ref, s_ref, c_ref, s_acc_ref, c_acc_ref,
                       yb_ref, yc_ref, s_out_ref, c_out_ref, n_out_ref, m_out_ref,
                       ob_ref, hc_ref, *, tb):
    del s_acc_ref, c_acc_ref
    i = pl.program_id(0)
    nb = cols_ref.shape[1]
    row_id = lax.broadcasted_iota(jnp.int32, (nb, 128), 0)
    lane_id = lax.broadcasted_iota(jnp.int32, (1, 128), 1)

    def body(j, carry):
        b = i * tb + j
        onehot = (row_id == b).astype(BF16)
        cols = jnp.dot(cols_ref[...], onehot, preferred_element_type=F32)
        scal = scal_ref[pl.ds(b, 1), :]
        sc = lambda idx: _col(scal, idx)
        m_row = m_ref[pl.ds(b, 1), :]
        rb = rows_b_ref[pl.ds(b, 1), :]
        rc = rows_c_ref[pl.ds(b, 1), :]
        n_all = n_ref[pl.ds(b, 1), :]
        m_new_row = jnp.zeros((1, 128), F32)
        o_rows, h_rows, n_rows = [], [], []
        for h in range(N_HEADS):
            sl = slice(h * HEAD_DIM, (h + 1) * HEAD_DIM)
            colblk = lambda v: cols[(v * N_HEADS + h) * HEAD_DIM:(v * N_HEADS + h + 1) * HEAD_DIM, :]
            s = s_ref[j, h]
            ks = jnp.sum(colblk(1) * s, axis=0, keepdims=True)
            qs = jnp.sum(colblk(0) * s, axis=0, keepdims=True)
            beta, eg, qk = sc(SC_BETA + h), sc(SC_EG + h), sc(SC_QKB + h)
            v_row = rb[:, 2 * GROUP_W + h * HEAD_DIM:2 * GROUP_W + (h + 1) * HEAD_DIM]
            v_new = beta * v_row - (beta * eg) * ks
            o_rows.append(eg * qs + qk * v_new)
            s_out_ref[j, h] = s * eg + colblk(1) * v_new
            cm = c_ref[j, h]
            qc = jnp.sum(colblk(2) * cm, axis=0, keepdims=True)
            q_row = rc[:, sl]
            k_row = rc[:, GROUP_W + h * HEAD_DIM:GROUP_W + (h + 1) * HEAD_DIM]
            vc_row = rc[:, 2 * GROUP_W + h * HEAD_DIM:2 * GROUP_W + (h + 1) * HEAD_DIM]
            n_row = n_all[:, sl]
            li, lf, qkc = sc(SC_LI + h), sc(SC_LF + h), sc(SC_QKC + h)
            m_old = _col(m_row, h)
            bb = lf + m_old
            m_t = jnp.maximum(bb, li)
            w_intra = jnp.exp(li - m_t) * qkc
            w_inter = jnp.exp(bb - m_t)
            num = w_inter * qc + w_intra * vc_row
            den = w_inter * jnp.sum(q_row * n_row, axis=-1, keepdims=True) + w_intra
            h_rows.append(num / jnp.maximum(jnp.abs(den), jnp.exp(-m_t)))
            m_new = jnp.maximum(m_old + lf, li)
            dec = jnp.exp(m_old + lf - m_new)
            e = jnp.exp(li - m_new)
            c_out_ref[j, h] = cm * dec + colblk(3) * (e * vc_row)
            n_rows.append(n_row * dec + e * k_row)
            m_new_row = jnp.where(lane_id == h, m_new, m_new_row)
        ob_ref[pl.ds(b, 1), :] = jnp.concatenate(o_rows, axis=1)
        hc_ref[pl.ds(b, 1), :] = jnp.concatenate(h_rows, axis=1)
        n_out_ref[pl.ds(b, 1), :] = jnp.concatenate(n_rows, axis=1)
        m_out_ref[pl.ds(b, 1), :] = m_new_row
        return carry

    lax.fori_loop(0, tb, body, 0, unroll=2)

    @pl.when(i == pl.num_programs(0) - 1)
    def _():
        ys_b, ys_c = [], []
        for h in range(N_HEADS):
            sl = slice(h * HEAD_DIM, (h + 1) * HEAD_DIM)
            o = ob_ref[:, sl]
            o = o * lax.rsqrt(jnp.mean(o * o, axis=-1, keepdims=True) + EPS) * dng_ref[...]
            ys_b.append(o * _silu(z_ref[:, sl]))
            hh = _sigmoid(og_ref[:, sl]) * hc_ref[:, sl]
            hh = hh * lax.rsqrt(jnp.mean(hh * hh, axis=-1, keepdims=True) + EPS) * mlg_ref[h:h + 1, :]
            ys_c.append(hh)
        yb_ref[...] = jnp.concatenate(ys_b, axis=1).astype(yb_ref.dtype)
        yc_ref[...] = jnp.concatenate(ys_c, axis=1).astype(yc_ref.dtype)


def sample_rec(cols, rows_b, rows_c, scal, P, n_st, m_st, dn_norm_g, ml_norm_g, s_all, c_all, s_acc, c_acc,
               layer, *, tb=8):
    nb = P.shape[0]
    W3 = 3 * GROUP_W
    full = lambda shape: pl.BlockSpec(shape, lambda i: (0,) * len(shape))
    state = pl.BlockSpec((None, tb, N_HEADS, HEAD_DIM, HEAD_DIM), lambda i: (layer, i, 0, 0, 0))
    untouched = pl.BlockSpec(memory_space=pl.ANY)
    return pl.pallas_call(
        functools.partial(_sample_rec_kernel, tb=tb),
        grid=(nb // tb,),
        in_specs=[full(cols.shape), full((nb, W3)), full((nb, W3)), full((nb, 128)),
                  pl.BlockSpec((nb, GROUP_W), lambda i: (0, COL_BZ)),
                  pl.BlockSpec((nb, GROUP_W), lambda i: (0, COL_CO)),
                  full((nb, GROUP_W)), full((nb, 128)), full((1, HEAD_DIM)), full((N_HEADS, HEAD_DIM)),
                  state, state, untouched, untouched],
        out_specs=[full((nb, GROUP_W)), full((nb, GROUP_W)), state, state,
                   full((nb, GROUP_W)), full((nb, 128))],
        out_shape=[jax.ShapeDtypeStruct((nb, GROUP_W), BF16),
                   jax.ShapeDtypeStruct((nb, GROUP_W), BF16),
                   jax.ShapeDtypeStruct(s_all.shape, F32),
                   jax.ShapeDtypeStruct(c_all.shape, F32),
                   jax.ShapeDtypeStruct((nb, GROUP_W), F32),
                   jax.ShapeDtypeStruct((nb, 128), F32)],
        input_output_aliases={12: 2, 13: 3},
        scratch_shapes=[pltpu.VMEM((nb, GROUP_W), F32), pltpu.VMEM((nb, GROUP_W), F32)],
        compiler_params=_cparams(("arbitrary",), VMEM_LIMIT),
    )(cols, rows_b, rows_c, scal, P, P, n_st, m_st, dn_norm_g.reshape(1, HEAD_DIM), ml_norm_g, s_all, c_all,
      s_acc, c_acc)


def _sample_xattn_kernel(q_ref, k_ref, v_ref, o_ref, *, tb):
    i = pl.program_id(0)
    n_rows = k_ref.shape[1]
    n_rep = n_rows // (2 * N_HEADS)
    ones = jnp.ones((HEAD_DIM, HEAD_DIM), BF16)
    fold = lambda x: x + pltpu.roll(x, N_HEADS, 0)

    def body(j, carry):
        b = i * tb + j
        q_row = q_ref[pl.ds(b, 1), :]
        heads = [q_row[:, h * HEAD_DIM:(h + 1) * HEAD_DIM] for h in range(N_HEADS)]
        q8 = jnp.concatenate(heads + heads, axis=0) * HEAD_DIM ** -0.5
        k3 = k_ref[j].reshape(n_rep, 2 * N_HEADS, HEAD_DIM)
        prod = (k3 * q8[None]).reshape(n_rows, HEAD_DIM)
        s = _bdot(prod, ones)
        s3 = s.reshape(n_rep, 2 * N_HEADS, HEAD_DIM)
        mx = jnp.max(s3, axis=0)
        mx = jnp.maximum(mx, pltpu.roll(mx, N_HEADS, 0))
        e3 = jnp.exp(s3 - mx[None])
        den = fold(jnp.sum(e3, axis=0))
        v3 = v_ref[j].reshape(n_rep, 2 * N_HEADS, HEAD_DIM)
        o8 = fold(jnp.sum(e3 * v3, axis=0)) / den
        o_ref[pl.ds(b, 1), :] = jnp.concatenate([o8[h:h + 1, :] for h in range(N_HEADS)], axis=1)
        return carry

    lax.fori_loop(0, tb, body, 0, unroll=4)


def sample_xattn(q, k_all, v_all, layer, *, tb=8):
    nb = q.shape[0]
    n_rows = k_all.shape[2]
    kv = pl.BlockSpec((None, tb, n_rows, HEAD_DIM), lambda i: (layer, i, 0, 0))
    return pl.pallas_call(
        functools.partial(_sample_xattn_kernel, tb=tb),
        grid=(nb // tb,),
        in_specs=[pl.BlockSpec((nb, GROUP_W), lambda i: (0, 0)), kv, kv],
        out_specs=pl.BlockSpec((nb, GROUP_W), lambda i: (0, 0)),
        out_shape=jax.ShapeDtypeStruct((nb, GROUP_W), F32),
        compiler_params=_cparams(("arbitrary",), VMEM_LIMIT),
    )(q, k_all, v_all)


W_B0 = 5 * GROUP_W
W_C0 = W_B0 + 2 * N_HEADS
W_C1 = W_C0 + 4 * GROUP_W
W_D0 = W_C1 + 2 * N_HEADS
W_D1 = W_D0 + 3 * GROUP_W


def _w_in_prep_kernel(w_ref, o_ref):
    tk = w_ref.shape[1]
    o_ref[:W_B0, :] = w_ref[:W_B0, :].astype(BF16)
    o_ref[W_B0:W_B0 + (W_C1 - W_C0), :] = w_ref[W_C0:W_C1, :].astype(BF16)
    o_ref[W_B0 + (W_C1 - W_C0):GATE_COL0, :] = w_ref[W_D0:W_D1, :].astype(BF16)
    gates = jnp.concatenate([w_ref[W_B0:W_C0, :], w_ref[W_C1:W_D0, :],
                             jnp.zeros((128 - 4 * N_HEADS, tk), F32)], axis=0)
    o_ref[GATE_COL0:, :] = gates.astype(BF16)


def _prep_w_in(w_in, *, tk=256):
    w_t = jnp.swapaxes(w_in, 1, 2)
    depth, n_in, K = w_t.shape
    assert n_in == W_D1
    return pl.pallas_call(
        _w_in_prep_kernel,
        grid=(depth, K // tk),
        in_specs=[pl.BlockSpec((None, n_in, tk), lambda l, i: (l, 0, i))],
        out_specs=pl.BlockSpec((None, IN_W_PAD, tk), lambda l, i: (l, 0, i)),
        out_shape=jax.ShapeDtypeStruct((depth, IN_W_PAD, K), BF16),
        compiler_params=_cparams(("parallel", "parallel"), VMEM_LIMIT),
    )(w_t)


def _ffn(x, g_pre, g_post, wg, wu, wd, layer, *, tm_up, tm_down, f32_up=False, up_layer=None):
    up_layer = layer if up_layer is None else up_layer
    up = functools.partial(norm_matmul, x, g_pre, tm=tm_up, tn=FFN_TN, out_dtype=BF16)
    if not f32_up:
        return matmul_resnorm([up([wg, wu], up_layer)], wd, layer, g_post, x, scale=0.5, tm=tm_down)
    n_tiles = x.shape[0] // tm_up
    act, wg_b, wu_b = up([wg, wu], up_layer, emit_bf16=True, row_tiles=(0, 1))
    if n_tiles > 1:
        act = up([wg_b[None], wu_b[None]], 0, row_tiles=(1, n_tiles - 1), out_into=act)
    return matmul_resnorm([act], wd, layer, g_post, x, scale=0.5, tm=tm_down), (wg_b, wu_b)


def _prompt_layer(x, mem2d, B, T, lw, layer):
    g = lw['norm_g']
    x, cast1 = _ffn(x, g[0], g[1], lw['ffn1_wg'], lw['ffn1_wu'], lw['ffn1_wd'], layer,
                    tm_up=1024, tm_down=512, f32_up=True)
    P = norm_matmul(x, g[2], [lw['w_in']], layer, tm=1024, tn=IN_TN, out_dtype=F32, w_is_nk=True)
    ya, yd, yb, yc, pool_tail, sc_tail, dn_s, dn_tail, ml_c, ml_nm = mixers_prompt(P, B, T, lw)
    x = matmul_resnorm([ya, yb, yc, yd], lw['w_out'], layer, g[3], x, scale=1.0, tm=512)
    kv = norm_matmul(mem2d, g[8], [lw['x_wkv']], layer, tm=1024, tn=512, out_dtype=F32)
    x = xattn_prompt(x, B, T, g[4], lw['x_wq'], kv, lw['x_wo'], layer, g[5])
    x, cast2 = _ffn(x, g[6], g[7], lw['ffn2_wg'], lw['ffn2_wu'], lw['ffn2_wd'], layer,
                    tm_up=1024, tm_down=512, f32_up=True)
    states = (pool_tail[:, 16 - POOL_STATE:], dn_tail[:, 8 - (DN_CONV - 1):], dn_s, ml_c,
              ml_nm[:, :N_HEADS], ml_nm[:, N_HEADS:, 0], sc_tail[:, 8 - (SC_WIDTH - 1):])
    mem_k = kv[:, :GROUP_W].reshape(B, MEM_LEN, N_HEADS, HEAD_DIM)
    mem_v = kv[:, GROUP_W:].reshape(B, MEM_LEN, N_HEADS, HEAD_DIM)
    return x, states, mem_k, mem_v, cast1 + cast2


def _sample_layer(x, st, big, acc, layer, lw, start_pos):
    pool_st, dnc_st, ml_n, ml_m, scc_st = st
    s_all, c_all, k_all, v_all = big
    s_acc, c_acc = acc
    nb = x.shape[0]
    g = lw['norm_g']
    wg1, wu1, wg2, wu2 = (w[None] for w in lw['ffn_cast'])
    x = _ffn(x, g[0], g[1], wg1, wu1, lw['ffn1_wd'], layer, tm_up=128, tm_down=128, up_layer=0)
    P = norm_matmul(x, g[2], [lw['w_in']], layer, tm=128, tn=IN_TN, out_dtype=F32, w_is_nk=True)
    (ya, yd, pool_new, dnc_new, scc_new, rows_b, rows_c, scal, cols) = sample_pre(
        P, pool_st.reshape(nb, -1), dnc_st.reshape(nb, -1), scc_st.reshape(nb, -1),
        lw['pool_w'], lw['pool_scale'], lw['dn_conv_w'], lw['dn_A_log'], lw['dn_dt_bias'],
        lw['ml_b_i'], lw['ml_b_f'], lw['sc_conv_w'], start_pos)
    m_pad = jnp.pad(ml_m, ((0, 0), (0, 128 - N_HEADS)))
    yb, yc, s_acc, c_acc, n_new, m_new = sample_rec(
        cols, rows_b, rows_c, scal, P, ml_n.reshape(nb, GROUP_W), m_pad,
        lw['dn_norm_g'], lw['ml_norm_g'], s_all, c_all, s_acc, c_acc, layer)
    x = matmul_resnorm([ya, yb, yc, yd], lw['w_out'], layer, g[3], x, scale=1.0, tm=128)
    q = norm_matmul(x, g[4], [lw['x_wq']], layer, tm=128, tn=GROUP_W, out_dtype=F32)
    o = sample_xattn(q, k_all, v_all, layer)
    x = matmul_resnorm([o], lw['x_wo'], layer, g[5], x, scale=1.0, tm=128)
    x = _ffn(x, g[6], g[7], wg2, wu2, lw['ffn2_wd'], layer, tm_up=128, tm_down=128, up_layer=0)
    states = (pool_new.reshape(pool_st.shape), dnc_new.reshape(dnc_st.shape),
              n_new.reshape(ml_n.shape), m_new[:, :N_HEADS], scc_new.reshape(scc_st.shape))
    return x, states, (s_acc, c_acc)


def kernel(x_prompt, x_sample, mem_prompt, state_pool, state_dn_conv, state_dn_S, state_ml_C, state_ml_n,
           state_ml_m, state_sc_conv, cache_mem_k, cache_mem_v, norm_g, w_in, w_out, pool_w, pool_scale,
           dn_conv_w, dn_A_log, dn_dt_bias, dn_norm_g, ml_b_i, ml_b_f, ml_norm_g, sc_conv_w,
           x_wq, x_wk, x_wv, x_wo, ffn1_wg, ffn1_wu, ffn1_wd, ffn2_wg, ffn2_wu, ffn2_wd):
    depth = norm_g.shape[0]
    B, T, _ = x_prompt.shape
    nb, t_dec, _ = x_sample.shape
    assert t_dec == 1
    start_pos = PAST_LEN

    stacked = dict(w_in=_prep_w_in(w_in), w_out=w_out.astype(BF16), x_wq=x_wq.astype(BF16),
                   x_wkv=jnp.concatenate([x_wk.astype(BF16), x_wv.astype(BF16)], axis=-1),
                   x_wo=x_wo.astype(BF16),
                   ffn1_wg=ffn1_wg, ffn1_wu=ffn1_wu, ffn1_wd=ffn1_wd.astype(BF16),
                   ffn2_wg=ffn2_wg, ffn2_wu=ffn2_wu, ffn2_wd=ffn2_wd.astype(BF16))

    def layer_weights(l):
        return dict(stacked, norm_g=norm_g[l], pool_w=pool_w[l],
                    pool_scale=pool_scale[l], dn_conv_w=dn_conv_w[l], dn_A_log=dn_A_log[l],
                    dn_dt_bias=dn_dt_bias[l], dn_norm_g=dn_norm_g[l], ml_b_i=ml_b_i[l], ml_b_f=ml_b_f[l],
                    ml_norm_g=ml_norm_g[l], sc_conv_w=sc_conv_w[l])

    mem2d = mem_prompt.reshape(B * MEM_LEN, D_MODEL)
    h = x_prompt.reshape(B * T, D_MODEL)
    p_states, mem_k_list, mem_v_list, ffn_cast = [], [], [], []
    for l in range(depth):
        h, ns, mk, mv, cast = _prompt_layer(h, mem2d, B, T, layer_weights(l), l)
        ffn_cast.append(cast)
        p_states.append(ns)
        mem_k_list.append(mk)
        mem_v_list.append(mv)
    y_prompt = h.reshape(B, T, D_MODEL)

    s_inputs = (state_pool, state_dn_conv, state_ml_n, state_ml_m, state_sc_conv)
    big = (state_dn_S, state_ml_C,
           cache_mem_k.reshape(depth, nb, MEM_LEN * N_HEADS, HEAD_DIM),
           cache_mem_v.reshape(depth, nb, MEM_LEN * N_HEADS, HEAD_DIM))
    h = x_sample.reshape(nb, D_MODEL)
    s_states = []
    acc = (lax.empty(state_dn_S.shape, F32), lax.empty(state_ml_C.shape, F32))
    for l in range(depth):
        st = tuple(s[l] for s in s_inputs)
        h, ns, acc = _sample_layer(h, st, big, acc, l, dict(layer_weights(l), ffn_cast=ffn_cast[l]), start_pos)
        s_states.append(ns)
    y_sample = h.reshape(nb, 1, D_MODEL)
    dn_S_s, ml_C_s = acc

    pool_p, dn_conv_p, dn_S_p, ml_C_p, ml_n_p, ml_m_p, sc_conv_p = [jnp.stack(z) for z in zip(*p_states)]
    pool_s, dn_conv_s, ml_n_s, ml_m_s, sc_conv_s = [jnp.stack(z) for z in zip(*s_states)]
    mem_k_p = jnp.stack(mem_k_list)
    mem_v_p = jnp.stack(mem_v_list)
    return (y_prompt, y_sample, pool_p, pool_s, dn_conv_p, dn_conv_s, dn_S_p, dn_S_s, ml_C_p, ml_C_s,
            ml_n_p, ml_n_s, ml_m_p, ml_m_s, sc_conv_p, sc_conv_s, mem_k_p, mem_v_p)
```

```python
import functools

import jax
import jax.numpy as jnp
from jax import lax
from jax.experimental import pallas as pl
from jax.experimental.pallas import tpu as pltpu

F32 = jnp.float32
BF16 = jnp.bfloat16

EPS = 1e-6
D_MODEL = 2048
GROUP_W = 512
HEAD_DIM = 128
N_HEADS = GROUP_W // HEAD_DIM
CHUNK = 64
POOL_WINDOWS = (2, 4, 8, 16)
POOL_STATE = 15
DN_CONV = 4
SC_WIDTH = 3
MEM_LEN = 256
PAST_LEN = 16384
FFN_TN = 512
COL_A, COL_BQ, COL_BK, COL_BV, COL_BZ = 0, 1, 2, 3, 4
COL_CQ, COL_CK, COL_CV, COL_CO = 5, 6, 7, 8
COL_DB, COL_DC, COL_DH = 9, 10, 11
GATE_COL0 = 12 * GROUP_W
IN_W_PAD = GATE_COL0 + 128
IN_TN = 1280
LANE_BETA, LANE_A, LANE_I, LANE_F = 0, 4, 8, 12
TB = 256
SUB = 128
VMEM_LIMIT = 60 * 2**20


def _cparams(sem, vmem=None):
    return pltpu.CompilerParams(dimension_semantics=sem, vmem_limit_bytes=vmem)


def _bdot(a, b):
    return jnp.dot(a.astype(BF16), b.astype(BF16), preferred_element_type=F32)


def _bdot_nt(a, b):
    return lax.dot_general(a.astype(BF16), b.astype(BF16), (((1,), (1,)), ((), ())),
                           preferred_element_type=F32)


def _sigmoid(x):
    return 0.5 * jnp.tanh(0.5 * x) + 0.5


def _silu(x):
    return x * _sigmoid(x)


def _softplus(x):
    return jnp.maximum(x, 0.0) + jnp.log(1.0 + jnp.exp(-jnp.abs(x)))


def _col(x, idx):
    return x[:, idx:idx + 1]


def _resident(shape):
    nd = len(shape)
    return pl.BlockSpec(shape, lambda *_: (0,) * nd, pipeline_mode=pl.Buffered(1))


def _norm_into(x_ref, g_ref, h_ref):
    tm = x_ref.shape[0]
    rs = min(tm, 128)

    def body(i, carry):
        r = pl.multiple_of(i * rs, rs)
        x = x_ref[pl.ds(r, rs), :]
        ms = jnp.mean(x * x, axis=-1, keepdims=True)
        h_ref[pl.ds(r, rs), :] = (x * lax.rsqrt(ms + EPS) * g_ref[...]).astype(BF16)
        return carry

    lax.fori_loop(0, tm // rs, body, 0)


def _zero_past_width(y, n_valid):
    tn = y.shape[1]
    if n_valid % tn == 0:
        return y
    col = pl.program_id(1) * tn + lax.broadcasted_iota(jnp.int32, (1, tn), 1)
    return jnp.where(col < n_valid, y, 0.0)


def _norm_mm_kernel(x_ref, g_ref, w_ref, o_ref, h_ref, *, n_valid, w_is_nk):
    @pl.when(pl.program_id(1) == 0)
    def _():
        _norm_into(x_ref, g_ref, h_ref)

    if w_is_nk:
        y = lax.dot_general(h_ref[...], w_ref[...], (((1,), (1,)), ((), ())), preferred_element_type=F32)
    else:
        y = jnp.dot(h_ref[...], w_ref[...], preferred_element_type=F32)
    o_ref[...] = _zero_past_width(y, n_valid).astype(o_ref.dtype)


def _norm_swiglu_kernel(x_ref, g_ref, wg_ref, wu_ref, *rest, n_valid, emit_bf16, has_out_into):
    rest = rest[1:] if has_out_into else rest
    o_ref, h_ref = rest[0], rest[-1]

    @pl.when(pl.program_id(1) == 0)
    def _():
        _norm_into(x_ref, g_ref, h_ref)

    h = h_ref[...]
    wg = wg_ref[...].astype(BF16)
    wu = wu_ref[...].astype(BF16)
    a = jnp.dot(h, wg, preferred_element_type=F32)
    b = jnp.dot(h, wu, preferred_element_type=F32)
    o_ref[...] = _zero_past_width(_silu(a) * b, n_valid).astype(o_ref.dtype)
    if emit_bf16:
        wg_out_ref, wu_out_ref = rest[1:3]
        wg_out_ref[...] = wg
        wu_out_ref[...] = wu


def norm_matmul(x, g, ws, layer, *, tm, tn, out_dtype, w_is_nk=False, emit_bf16=False, row_tiles=None,
                out_into=None):
    M, K = x.shape
    n_valid = ws[0].shape[1 if w_is_nk else 2]
    n_steps = pl.cdiv(n_valid, tn)
    N = n_steps * tn
    tm = min(tm, M)
    first, count = row_tiles if row_tiles is not None else (0, M // tm)
    if len(ws) == 2:
        assert not w_is_nk and (count == 1 or not emit_bf16)
        kern = functools.partial(_norm_swiglu_kernel, n_valid=n_valid, emit_bf16=emit_bf16,
                                 has_out_into=out_into is not None)
    else:
        assert not emit_bf16 and out_into is None
        kern = functools.partial(_norm_mm_kernel, n_valid=n_valid, w_is_nk=w_is_nk)
    w_spec = (pl.BlockSpec((None, tn, K), lambda i, j: (layer, j, 0)) if w_is_nk
              else pl.BlockSpec((None, K, tn), lambda i, j: (layer, 0, j)))
    out_specs = pl.BlockSpec((tm, tn), lambda i, j: (i + first, j))
    out_shape = jax.ShapeDtypeStruct((M, N), out_dtype)
    if emit_bf16:
        w_out = pl.BlockSpec((K, tn), lambda i, j: (0, j))
        out_specs = [out_specs, w_out, w_out]
        out_shape = [out_shape] + [jax.ShapeDtypeStruct((K, n_valid), BF16)] * 2
    extra_in, extra_specs, aliases = (), [], {}
    if out_into is not None:
        extra_in, extra_specs = (out_into,), [pl.BlockSpec(memory_space=pl.ANY)]
        aliases = {2 + len(ws): 0}
    return pl.pallas_call(
        kern,
        grid=(count, n_steps),
        in_specs=[pl.BlockSpec((tm, K), lambda i, j: (i + first, 0)),
                  pl.BlockSpec((1, K), lambda i, j: (0, 0))]
        + [w_spec for _ in ws] + extra_specs,
        out_specs=out_specs,
        out_shape=out_shape,
        input_output_aliases=aliases,
        scratch_shapes=[pltpu.VMEM((tm, K), BF16)],
        compiler_params=_cparams(("parallel", "arbitrary"), VMEM_LIMIT),
    )(x, g.reshape(1, K), *ws, *extra_in)


def _mm_resnorm_kernel(*refs, n_a, scale):
    a_refs = refs[:n_a]
    w_ref, g_ref, res_ref, o_ref = refs[n_a:]
    parts = [r[...].astype(BF16) for r in a_refs]
    a = parts[0] if n_a == 1 else jnp.concatenate(parts, axis=1)
    y = jnp.dot(a, w_ref[...], preferred_element_type=F32)
    ms = jnp.mean(y * y, axis=-1, keepdims=True)
    o_ref[...] = res_ref[...] + scale * (y * lax.rsqrt(ms + EPS) * g_ref[...])


def matmul_resnorm(a_list, w, layer, g, res, *, scale, tm):
    M, N = res.shape
    tm = min(tm, M)
    K = w.shape[1]
    widths = [K] if len(a_list) == 1 else [a.shape[1] for a in a_list]
    assert sum(widths) == K
    w_spec = pl.BlockSpec((None, K, N), lambda i: (layer, 0, 0), pipeline_mode=pl.Buffered(1))
    return pl.pallas_call(
        functools.partial(_mm_resnorm_kernel, n_a=len(a_list), scale=scale),
        grid=(M // tm,),
        in_specs=[pl.BlockSpec((tm, wd), lambda i: (i, 0)) for wd in widths]
        + [w_spec, _resident((1, N)), pl.BlockSpec((tm, N), lambda i: (i, 0))],
        out_specs=pl.BlockSpec((tm, N), lambda i: (i, 0)),
        out_shape=jax.ShapeDtypeStruct((M, N), F32),
        compiler_params=_cparams(("parallel",), VMEM_LIMIT),
    )(*a_list, w, g.reshape(1, N), res)


def _chunk_cumsum(x):
    n = x.shape[0]
    row = lax.broadcasted_iota(jnp.int32, (n, 1), 0) % CHUNK
    s = 1
    while s < CHUNK:
        x = x + jnp.where(row >= s, pltpu.roll(x, s, 0), 0.0)
        s *= 2
    return x


def _chunk_last(x):
    n = x.shape[0]
    parts = [jnp.broadcast_to(x[c * CHUNK + CHUNK - 1:c * CHUNK + CHUNK, :], (CHUNK, x.shape[1]))
             for c in range(n // CHUNK)]
    return jnp.concatenate(parts, axis=0)


def _pad_rows(x, c, n_chunks):
    z = jnp.zeros_like(x)
    return jnp.concatenate([x if i == c else z for i in range(n_chunks)], axis=0)


def _blk_masks(n):
    r = lax.broadcasted_iota(jnp.int32, (n, n), 0)
    c = lax.broadcasted_iota(jnp.int32, (n, n), 1)
    same = (r ^ c) < CHUNK
    return same & (c <= r), same & (c < r), r == c


def _split(a):
    ah = a.astype(BF16)
    return ah, (a - ah.astype(F32)).astype(BF16)


def _split_dot(a, b):
    d = lambda x, y: jnp.dot(x, y, preferred_element_type=F32)
    return d(a[0], b[0]) + d(a[0], b[1]) + d(a[1], b[0])


def _unit_lower_inverses(a_list, n):
    r = lax.broadcasted_iota(jnp.int32, (n, n), 0)
    c = lax.broadcasted_iota(jnp.int32, (n, n), 1)
    rc = r ^ c
    pair = lambda s: (rc >= s) & (rc < 2 * s)
    eye = (r == c).astype(F32)
    xs = [eye - jnp.where(pair(1), a, 0.0) for a in a_list]
    s = 2
    while s < CHUNK:
        m = pair(s)
        x_sp = [_split(x) for x in xs]
        ts = [_split_dot(_split(jnp.where(m, a, 0.0)), x) for a, x in zip(a_list, x_sp)]
        xs = [x - _split_dot(xp, _split(t)) for x, xp, t in zip(xs, x_sp, ts)]
        s *= 2
    return xs


def _pool_sconv_step(ua_ref, db_ref, dc_ref, dh_ref, pw_ref, ps_ref, sw_ref, ya_ref, yd_ref, hist_a, hist_d,
                     *, tb):
    i = pl.program_id(1)

    u = ua_ref[...]
    ext = jnp.concatenate([hist_a[...], u], axis=0)
    a2 = ext + pltpu.roll(ext, 1, 0)
    a4 = a2[:, 128:] + pltpu.roll(a2[:, 128:], 2, 0)
    a8 = a4[:, 128:] + pltpu.roll(a4[:, 128:], 4, 0)
    a16 = a8[:, 128:] + pltpu.roll(a8[:, 128:], 8, 0)
    sums = (a2[16:, :128], a4[16:, :128], a8[16:, :128], a16[16:, :])
    pos = i * tb + lax.broadcasted_iota(jnp.int32, (tb, 1), 0)
    ys = []
    for g, w in enumerate(POOL_WINDOWS):
        cnt = jnp.minimum(pos + 1, w).astype(F32)
        d = sums[g] / cnt - u[:, g * 128:(g + 1) * 128]
        ys.append(_bdot(d, pw_ref[g]))
    ya_ref[...] = (jnp.concatenate(ys, axis=1) * ps_ref[...]).astype(ya_ref.dtype)
    hist_a[...] = u[tb - 16:, :]

    us = dc_ref[...] * dh_ref[...]
    ext = jnp.concatenate([hist_d[...], us], axis=0)
    sw = sw_ref[...]
    y = sw[0:1] * pltpu.roll(ext, 2, 0) + sw[1:2] * pltpu.roll(ext, 1, 0) + sw[2:3] * ext
    yd_ref[...] = (db_ref[...] * y[8:]).astype(yd_ref.dtype)
    hist_d[...] = us[tb - 8:, :]
    return u[tb - 16:, :], us[tb - 8:, :]


def _deltanet_step(q_ref, k_ref, v_ref, z_ref, gt_ref, cw_ref, alog_ref, dtb_ref, ng_ref, y_ref, s_ref, hist_ref,
                   *, tb):
    nc = tb // CHUNK

    qkv = jnp.concatenate([q_ref[...], k_ref[...], v_ref[...]], axis=1)
    ext = jnp.concatenate([hist_ref[...], qkv], axis=0)
    cw = cw_ref[...]
    conv = (cw[0:1] * pltpu.roll(ext, 3, 0) + cw[1:2] * pltpu.roll(ext, 2, 0)
            + cw[2:3] * pltpu.roll(ext, 1, 0) + cw[3:4] * ext)[8:]
    act = _silu(conv)
    hist_ref[...] = qkv[tb - 8:, :]

    gates = gt_ref[...]
    beta_all = _sigmoid(gates)
    g_all = -jnp.exp(alog_ref[...]) * _softplus(gates + dtb_ref[...])
    gcum_all = _chunk_cumsum(g_all)
    glast_all = _chunk_last(gcum_all)
    gcum_t = gcum_all.T
    causal, strict, _ = _blk_masks(SUB)

    heads = range(N_HEADS)
    subs = range(tb // SUB)
    cps = SUB // CHUNK
    hd = lambda base, h: slice(base + h * HEAD_DIM, base + (h + 1) * HEAD_DIM)
    qs, ks, gcs, gls, betas = [], [], [], [], []
    for h in heads:
        qh = act[:, hd(0, h)]
        kh = act[:, hd(GROUP_W, h)]
        qs.append(qh * lax.rsqrt(jnp.sum(qh * qh, axis=-1, keepdims=True) + EPS) * HEAD_DIM ** -0.5)
        ks.append(kh * lax.rsqrt(jnp.sum(kh * kh, axis=-1, keepdims=True) + EPS))
        betas.append(_col(beta_all, LANE_BETA + h))
        gcs.append(_col(gcum_all, LANE_A + h))
        gls.append(_col(glast_all, LANE_A + h))
    k_ts, a_mats, qks = {}, [], {}
    for st in subs:
        R = slice(st * SUB, (st + 1) * SUB)
        for h in heads:
            gr = gcum_t[LANE_A + h:LANE_A + h + 1, R]
            decay = jnp.where(causal, jnp.exp(jnp.where(causal, gcs[h][R] - gr, 0.0)), 0.0)
            k_t = ks[h][R].T.astype(BF16)
            a_mats.append(jnp.where(strict, _bdot(ks[h][R] * betas[h][R], k_t) * decay, 0.0))
            qks[st, h] = jnp.where(causal, _bdot(qs[h][R], k_t) * decay, 0.0).astype(BF16)
            k_ts[st, h] = k_t
    tinvs = _unit_lower_inverses(a_mats, SUB)
    us, ws = {}, {}
    for st in subs:
        R = slice(st * SUB, (st + 1) * SUB)
        for h in heads:
            vb = act[R, hd(2 * GROUP_W, h)] * betas[h][R]
            kbe = ks[h][R] * (betas[h][R] * jnp.exp(gcs[h][R]))
            uw = _bdot(tinvs[st * N_HEADS + h], jnp.concatenate([vb, kbe], axis=1))
            us[st, h] = uw[:, :HEAD_DIM]
            ws[st, h] = uw[:, HEAD_DIM:].astype(BF16)
    q_decs = [(qs[h] * jnp.exp(gcs[h])).astype(BF16) for h in heads]
    tails = [jnp.exp(gls[h] - gcs[h]) for h in heads]
    s_hs = [s_ref[h] for h in heads]
    outs = [[] for _ in heads]
    for c in range(nc):
        r = slice(c * CHUNK, (c + 1) * CHUNK)
        st, cl = divmod(c, cps)
        rl = slice(cl * CHUNK, (cl + 1) * CHUNK)
        for h in heads:
            s_b = s_hs[h].astype(BF16)
            v_new = us[st, h][rl] - jnp.dot(ws[st, h][rl], s_b, preferred_element_type=F32)
            outs[h].append(jnp.dot(q_decs[h][r], s_b, preferred_element_type=F32)
                           + _bdot(qks[st, h][rl], _pad_rows(v_new, cl, cps)))
            s_dec = jnp.exp(jnp.broadcast_to(gls[h][c * CHUNK:c * CHUNK + 1], (HEAD_DIM, 1)))
            s_hs[h] = s_hs[h] * s_dec + _bdot(k_ts[st, h], _pad_rows(v_new * tails[h][r], cl, cps))
    ys = []
    for h in heads:
        s_ref[h] = s_hs[h]
        o = jnp.concatenate(outs[h], axis=0)
        o = o * lax.rsqrt(jnp.mean(o * o, axis=-1, keepdims=True) + EPS) * ng_ref[...]
        ys.append(o * _silu(z_ref[:, hd(0, h)]))
    y_ref[...] = jnp.concatenate(ys, axis=1).astype(y_ref.dtype)
    return qkv[tb - 8:, :]


def _mlstm_step(q_ref, k_ref, v_ref, og_ref, gt_ref, bi_ref, bf_ref, ng_ref, y_ref, c_ref, nm_ref, *, tb):
    nc = tb // CHUNK

    gates = gt_ref[...]
    li_all = gates + bi_ref[...]
    lf_all = -_softplus(-(gates + bf_ref[...]))
    f_all = _chunk_cumsum(lf_all)
    flast_all = _chunk_last(f_all)
    f_t = f_all.T
    li_t = li_all.T
    causal, _, _ = _blk_masks(SUB)

    heads = range(N_HEADS)
    subs = range(tb // SUB)
    cps = SUB // CHUNK
    hd = lambda h: slice(h * HEAD_DIM, (h + 1) * HEAD_DIM)
    qs, kss, vs, fcs, bs, evs, eks, decs = ([] for _ in range(8))
    for h in heads:
        kh = k_ref[:, hd(h)] * HEAD_DIM ** -0.5
        vh = v_ref[:, hd(h)]
        fc = _col(f_all, LANE_F + h)
        fl = _col(flast_all, LANE_F + h)
        a_w = fl - fc + _col(li_all, LANE_I + h)
        m_h = nm_ref[N_HEADS + h:N_HEADS + h + 1, 0:1]
        m_prev, m_next, dec_h = [], [], []
        for c in range(nc):
            r = slice(c * CHUNK, (c + 1) * CHUNK)
            fl_c = fl[c * CHUNK:c * CHUNK + 1]
            m_new = jnp.maximum(m_h + fl_c, jnp.max(a_w[r], axis=0, keepdims=True))
            dec_h.append(jnp.exp(m_h + fl_c - m_new))
            m_prev.append(jnp.broadcast_to(m_h, (CHUNK, 1)))
            m_next.append(jnp.broadcast_to(m_new, (CHUNK, 1)))
            m_h = m_new
        nm_ref[N_HEADS + h:N_HEADS + h + 1, :] = jnp.broadcast_to(m_h, (1, HEAD_DIM))
        e = jnp.exp(a_w - jnp.concatenate(m_next, axis=0))
        qs.append(q_ref[:, hd(h)])
        kss.append(kh)
        vs.append(vh)
        fcs.append(fc)
        bs.append(fc + jnp.concatenate(m_prev, axis=0))
        evs.append(e * vh)
        eks.append(e * kh)
        decs.append(dec_h)
    k_ts, intras, den_intras, w_inters, inv_floor = {}, {}, {}, {}, {}
    for st in subs:
        R = slice(st * SUB, (st + 1) * SUB)
        for h in heads:
            fr = f_t[LANE_F + h:LANE_F + h + 1, R]
            lir = li_t[LANE_I + h:LANE_I + h + 1, R]
            dm = jnp.where(causal, fcs[h][R] - fr + lir, -jnp.inf)
            m_t = jnp.maximum(bs[h][R], jnp.max(dm, axis=1, keepdims=True))
            k_t = kss[h][R].T.astype(BF16)
            w_intra = jnp.exp(dm - m_t) * _bdot(qs[h][R], k_t)
            k_ts[st, h] = k_t
            intras[st, h] = _bdot(w_intra, vs[h][R])
            den_intras[st, h] = jnp.sum(w_intra, axis=-1, keepdims=True)
            w_inters[st, h] = jnp.exp(bs[h][R] - m_t)
            inv_floor[st, h] = jnp.exp(-m_t)
    c_hs = [c_ref[h] for h in heads]
    n_hs = [nm_ref[h:h + 1, :] for h in heads]
    outs = [[] for _ in heads]
    for c in range(nc):
        r = slice(c * CHUNK, (c + 1) * CHUNK)
        st, cl = divmod(c, cps)
        rl = slice(cl * CHUNK, (cl + 1) * CHUNK)
        for h in heads:
            q_c = qs[h][r]
            w_inter = w_inters[st, h][rl]
            num = w_inter * _bdot(q_c, c_hs[h]) + intras[st, h][rl]
            den = w_inter * jnp.sum(q_c * n_hs[h], axis=-1, keepdims=True) + den_intras[st, h][rl]
            outs[h].append(num / jnp.maximum(jnp.abs(den), inv_floor[st, h][rl]))
            c_hs[h] = c_hs[h] * decs[h][c] + _bdot(k_ts[st, h], _pad_rows(evs[h][r], cl, cps))
            n_hs[h] = n_hs[h] * decs[h][c] + jnp.sum(eks[h][r], axis=0, keepdims=True)
    ys = []
    for h in heads:
        c_ref[h] = c_hs[h]
        nm_ref[h:h + 1, :] = n_hs[h]
        hh = jnp.concatenate(outs[h], axis=0)
        hh = _sigmoid(og_ref[:, hd(h)]) * hh
        hh = hh * lax.rsqrt(jnp.mean(hh * hh, axis=-1, keepdims=True) + EPS) * ng_ref[h:h + 1, :]
        ys.append(hh)
    y_ref[...] = jnp.concatenate(ys, axis=1).astype(y_ref.dtype)


N_POOL_IN, N_DN_IN, N_ML_IN = 7, 9, 8


def _mixers_kernel(*refs, tb):
    ins = iter(refs[:N_POOL_IN + N_DN_IN + N_ML_IN])
    take = lambda n: [next(ins) for _ in range(n)]
    pool_in, dn_in, ml_in = take(N_POOL_IN), take(N_DN_IN), take(N_ML_IN)
    (ya_ref, yd_ref, yb_ref, yc_ref, ptail_ref, stail_ref, s_out_ref, ctail_ref, c_out_ref, nm_out_ref,
     hist_a, hist_d, s_ref, hist_b, c_ref, nm_ref) = refs[N_POOL_IN + N_DN_IN + N_ML_IN:]
    i = pl.program_id(1)

    @pl.when(i == 0)
    def _():
        for r in (hist_a, hist_d, s_ref, hist_b, c_ref, nm_ref):
            r[...] = jnp.zeros_like(r)

    qkv_tail = _deltanet_step(*dn_in, yb_ref, s_ref, hist_b, tb=tb)
    _mlstm_step(*ml_in, yc_ref, c_ref, nm_ref, tb=tb)
    u_tail, us_tail = _pool_sconv_step(*pool_in, ya_ref, yd_ref, hist_a, hist_d, tb=tb)

    @pl.when(i == pl.num_programs(1) - 1)
    def _():
        ptail_ref[0] = u_tail
        stail_ref[0] = us_tail
        s_out_ref[0] = s_ref[...]
        ctail_ref[0] = qkv_tail
        c_out_ref[0] = c_ref[...]
        nm_out_ref[0] = nm_ref[...]


def mixers_prompt(P, B, T, lw):
    nT = T // TB
    M = B * T
    col = lambda c: pl.BlockSpec((TB, GROUP_W), lambda b, i: (b * nT + i, c))
    gates = pl.BlockSpec((TB, 128), lambda b, i: (b * nT + i, GATE_COL0 // 128))
    const = lambda *shape: pl.BlockSpec(shape, lambda b, i: (0,) * len(shape))
    lane_row = lambda v, lane: jnp.zeros((1, 128), F32).at[0, lane:lane + N_HEADS].set(v)
    y_spec = pl.BlockSpec((TB, GROUP_W), lambda b, i: (b * nT + i, 0))
    per_seq = lambda *shape: pl.BlockSpec((1,) + shape, lambda b, i: (b,) + (0,) * len(shape))
    mat = (N_HEADS, HEAD_DIM, HEAD_DIM)
    pool_specs = [col(COL_A), col(COL_DB), col(COL_DC), col(COL_DH),
                  const(4, 128, 128), const(1, GROUP_W), const(SC_WIDTH, GROUP_W)]
    dn_specs = [col(COL_BQ), col(COL_BK), col(COL_BV), col(COL_BZ), gates,
                const(DN_CONV, 3 * GROUP_W), const(1, 128), const(1, 128), const(1, HEAD_DIM)]
    ml_specs = [col(COL_CQ), col(COL_CK), col(COL_CV), col(COL_CO), gates,
                const(1, 128), const(1, 128), const(N_HEADS, HEAD_DIM)]
    assert (len(pool_specs), len(dn_specs), len(ml_specs)) == (N_POOL_IN, N_DN_IN, N_ML_IN)
    y_shape = jax.ShapeDtypeStruct((M, GROUP_W), BF16)
    state = lambda *shape: jax.ShapeDtypeStruct((B,) + shape, F32)
    return pl.pallas_call(
        functools.partial(_mixers_kernel, tb=TB),
        grid=(B, nT),
        in_specs=pool_specs + dn_specs + ml_specs,
        out_specs=[y_spec] * 4 + [per_seq(16, GROUP_W), per_seq(8, GROUP_W), per_seq(*mat),
                                  per_seq(8, 3 * GROUP_W), per_seq(*mat), per_seq(2 * N_HEADS, HEAD_DIM)],
        out_shape=[y_shape] * 4 + [state(16, GROUP_W), state(8, GROUP_W), state(*mat),
                                   state(8, 3 * GROUP_W), state(*mat), state(2 * N_HEADS, HEAD_DIM)],
        scratch_shapes=[pltpu.VMEM((16, GROUP_W), F32), pltpu.VMEM((8, GROUP_W), F32),
                        pltpu.VMEM(mat, F32), pltpu.VMEM((8, 3 * GROUP_W), F32),
                        pltpu.VMEM(mat, F32), pltpu.VMEM((2 * N_HEADS, HEAD_DIM), F32)],
        compiler_params=_cparams(("parallel", "arbitrary"), VMEM_LIMIT),
    )(P, P, P, P, lw['pool_w'], lw['pool_scale'].reshape(1, GROUP_W), lw['sc_conv_w'],
      P, P, P, P, P, lw['dn_conv_w'], lane_row(lw['dn_A_log'], LANE_A), lane_row(lw['dn_dt_bias'], LANE_A),
      lw['dn_norm_g'].reshape(1, HEAD_DIM),
      P, P, P, P, P, lane_row(lw['ml_b_i'], LANE_I), lane_row(lw['ml_b_f'], LANE_F), lw['ml_norm_g'])


def _xattn_prompt_kernel(x_ref, gpre_ref, wq_ref, kv_ref, wo_ref, gpost_ref, o_ref, h_ref):
    _norm_into(x_ref, gpre_ref, h_ref)
    q = jnp.dot(h_ref[...], wq_ref[...], preferred_element_type=F32)
    heads = range(N_HEADS)
    hd = lambda base, h: slice(base + h * HEAD_DIM, base + (h + 1) * HEAD_DIM)
    ss = [_bdot_nt(q[:, hd(0, h)], kv_ref[:, hd(0, h)]) * HEAD_DIM ** -0.5 for h in heads]
    es = [jnp.exp(s - jnp.max(s, axis=-1, keepdims=True)) for s in ss]
    outs = [_bdot(e, kv_ref[:, hd(GROUP_W, h)]) / jnp.sum(e, axis=-1, keepdims=True)
            for h, e in zip(heads, es)]
    o = jnp.concatenate(outs, axis=1).astype(BF16)
    y = jnp.dot(o, wo_ref[...], preferred_element_type=F32)
    ms = jnp.mean(y * y, axis=-1, keepdims=True)
    o_ref[...] = x_ref[...] + y * lax.rsqrt(ms + EPS) * gpost_ref[...]


def xattn_prompt(x, B, T, g_pre, wq, kv, wo, layer, g_post, *, tq=512):
    nq = T // tq
    M = B * T
    return pl.pallas_call(
        _xattn_prompt_kernel,
        grid=(B, nq),
        in_specs=[pl.BlockSpec((tq, D_MODEL), lambda b, i: (b * nq + i, 0)),
                  pl.BlockSpec((1, D_MODEL), lambda b, i: (0, 0)),
                  pl.BlockSpec((None, D_MODEL, GROUP_W), lambda b, i: (layer, 0, 0)),
                  pl.BlockSpec((MEM_LEN, 2 * GROUP_W), lambda b, i: (b, 0)),
                  pl.BlockSpec((None, GROUP_W, D_MODEL), lambda b, i: (layer, 0, 0)),
                  pl.BlockSpec((1, D_MODEL), lambda b, i: (0, 0))],
        out_specs=pl.BlockSpec((tq, D_MODEL), lambda b, i: (b * nq + i, 0)),
        out_shape=jax.ShapeDtypeStruct((M, D_MODEL), F32),
        scratch_shapes=[pltpu.VMEM((tq, D_MODEL), BF16)],
        compiler_params=_cparams(("parallel", "parallel"), VMEM_LIMIT),
    )(x, g_pre.reshape(1, D_MODEL), wq, kv, wo, g_post.reshape(1, D_MODEL))


SC_BETA, SC_EG, SC_QKB, SC_LI, SC_LF, SC_QKC = 0, 4, 8, 12, 16, 20


def _sample_pre_kernel(p_ref, pool_ref, dnc_ref, scc_ref, pw_ref, ps_ref, cw_ref, alog_ref, dtb_ref,
                       bi_ref, bf_ref, sw_ref,
                       ya_ref, yd_ref, pool_out_ref, dnc_out_ref, scc_out_ref,
                       rows_b_ref, rows_c_ref, scal_ref, cols_ref, *, start_pos):
    nb = p_ref.shape[0]
    blk = lambda c: p_ref[:, c * GROUP_W:(c + 1) * GROUP_W]

    u = blk(COL_A)
    ys = []
    for g, w in enumerate(POOL_WINDOWS):
        lanes = slice(g * 128, (g + 1) * 128)
        tot = u[:, lanes]
        for r in range(POOL_STATE + 1 - w, POOL_STATE):
            tot = tot + pool_ref[:, r * GROUP_W + g * 128:r * GROUP_W + (g + 1) * 128]
        d = tot / float(min(start_pos + 1, w)) - u[:, lanes]
        ys.append(_bdot(d, pw_ref[g]))
    ya_ref[...] = (jnp.concatenate(ys, axis=1) * ps_ref[...]).astype(ya_ref.dtype)
    pool_out_ref[:, :(POOL_STATE - 1) * GROUP_W] = pool_ref[:, GROUP_W:]
    pool_out_ref[:, (POOL_STATE - 1) * GROUP_W:] = u

    us = blk(COL_DC) * blk(COL_DH)
    sw = sw_ref[...]
    y = sw[0:1] * scc_ref[:, :GROUP_W] + sw[1:2] * scc_ref[:, GROUP_W:] + sw[2:3] * us
    yd_ref[...] = (blk(COL_DB) * y).astype(yd_ref.dtype)
    scc_out_ref[:, :GROUP_W] = scc_ref[:, GROUP_W:]
    scc_out_ref[:, GROUP_W:] = us

    qkv = p_ref[:, COL_BQ * GROUP_W:(COL_BV + 1) * GROUP_W]
    cw = cw_ref[...]
    W3 = 3 * GROUP_W
    conv = cw[DN_CONV - 1:DN_CONV] * qkv
    for j in range(DN_CONV - 1):
        conv = conv + cw[j:j + 1] * dnc_ref[:, j * W3:(j + 1) * W3]
    act = _silu(conv)
    dnc_out_ref[:, :(DN_CONV - 2) * W3] = dnc_ref[:, W3:]
    dnc_out_ref[:, (DN_CONV - 2) * W3:] = qkv

    gates = p_ref[:, GATE_COL0:GATE_COL0 + 128]
    beta_all = _sigmoid(gates)
    eg_all = jnp.exp(-jnp.exp(alog_ref[...]) * _softplus(gates + dtb_ref[...]))
    li_all = gates + bi_ref[...]
    lf_all = -_softplus(-(gates + bf_ref[...]))

    lane = lax.broadcasted_iota(jnp.int32, (nb, 128), 1)
    scal = jnp.zeros((nb, 128), F32)

    def put(tab, lane_idx, colv):
        return jnp.where(lane == lane_idx, colv, tab)

    qs, ks = [], []
    for h in range(N_HEADS):
        sl = slice(h * HEAD_DIM, (h + 1) * HEAD_DIM)
        qh = act[:, sl]
        kh = act[:, GROUP_W + h * HEAD_DIM:GROUP_W + (h + 1) * HEAD_DIM]
        qh = qh * lax.rsqrt(jnp.sum(qh * qh, axis=-1, keepdims=True) + EPS) * HEAD_DIM ** -0.5
        kh = kh * lax.rsqrt(jnp.sum(kh * kh, axis=-1, keepdims=True) + EPS)
        qs.append(qh)
        ks.append(kh)
        scal = put(scal, SC_BETA + h, _col(beta_all, LANE_BETA + h))
        scal = put(scal, SC_EG + h, _col(eg_all, LANE_A + h))
        scal = put(scal, SC_QKB + h, jnp.sum(qh * kh, axis=-1, keepdims=True))
        cols_ref[(0 * N_HEADS + h) * HEAD_DIM:(0 * N_HEADS + h + 1) * HEAD_DIM, :] = qh.T.astype(BF16)
        cols_ref[(1 * N_HEADS + h) * HEAD_DIM:(1 * N_HEADS + h + 1) * HEAD_DIM, :] = kh.T.astype(BF16)
    rows_b_ref[...] = jnp.concatenate(qs + ks + [act[:, 2 * GROUP_W:]], axis=1)

    qc = blk(COL_CQ)
    kc = blk(COL_CK) * HEAD_DIM ** -0.5
    for h in range(N_HEADS):
        sl = slice(h * HEAD_DIM, (h + 1) * HEAD_DIM)
        scal = put(scal, SC_LI + h, _col(li_all, LANE_I + h))
        scal = put(scal, SC_LF + h, _col(lf_all, LANE_F + h))
        scal = put(scal, SC_QKC + h, jnp.sum(qc[:, sl] * kc[:, sl], axis=-1, keepdims=True))
        cols_ref[(2 * N_HEADS + h) * HEAD_DIM:(2 * N_HEADS + h + 1) * HEAD_DIM, :] = qc[:, sl].T.astype(BF16)
        cols_ref[(3 * N_HEADS + h) * HEAD_DIM:(3 * N_HEADS + h + 1) * HEAD_DIM, :] = kc[:, sl].T.astype(BF16)
    rows_c_ref[...] = jnp.concatenate([qc, kc, blk(COL_CV)], axis=1)
    scal_ref[...] = scal


def sample_pre(P, pool_st, dnc_st, scc_st, pool_w, pool_scale, conv_w, a_log, dt_bias, b_i, b_f, sc_w,
               start_pos):
    nb = P.shape[0]
    lane_row = lambda v, lane: jnp.zeros((1, 128), F32).at[0, lane:lane + N_HEADS].set(v)
    W3 = 3 * GROUP_W
    out_shape = [jax.ShapeDtypeStruct((nb, GROUP_W), BF16),
                 jax.ShapeDtypeStruct((nb, GROUP_W), BF16),
                 jax.ShapeDtypeStruct(pool_st.shape, F32),
                 jax.ShapeDtypeStruct(dnc_st.shape, F32),
                 jax.ShapeDtypeStruct(scc_st.shape, F32),
                 jax.ShapeDtypeStruct((nb, W3), F32),
                 jax.ShapeDtypeStruct((nb, W3), F32),
                 jax.ShapeDtypeStruct((nb, 128), F32),
                 jax.ShapeDtypeStruct((4 * N_HEADS * HEAD_DIM, nb), BF16)]
    return pl.pallas_call(
        functools.partial(_sample_pre_kernel, start_pos=start_pos),
        out_shape=out_shape,
        compiler_params=pltpu.CompilerParams(vmem_limit_bytes=VMEM_LIMIT),
    )(P, pool_st, dnc_st, scc_st, pool_w, pool_scale.reshape(1, GROUP_W), conv_w,
      lane_row(a_log, LANE_A), lane_row(dt_bias, LANE_A), lane_row(b_i, LANE_I), lane_row(b_f, LANE_F), sc_w)


def _sample_rec_kernel(cols_ref, rows_b_ref, rows_c_ref, scal_ref, z_ref, og_ref, n_ref, m_ref,
                       dng_ref, mlg_ref, s_ref, c_ref, s_acc_ref, c_acc_ref,
                       yb_ref, yc_ref, s_out_ref, c_out_ref, n_out_ref, m_out_ref,
                       ob_ref, hc_ref, *, tb):
    del s_acc_ref, c_acc_ref
    i = pl.program_id(0)
    nb = cols_ref.shape[1]
    row_id = lax.broadcasted_iota(jnp.int32, (nb, 128), 0)
    lane_id = lax.broadcasted_iota(jnp.int32, (1, 128), 1)

    def body(j, carry):
        b = i * tb + j
        onehot = (row_id == b).astype(BF16)
        cols = jnp.dot(cols_ref[...], onehot, preferred_element_type=F32)
        scal = scal_ref[pl.ds(b, 1), :]
        sc = lambda idx: _col(scal, idx)
        m_row = m_ref[pl.ds(b, 1), :]
        rb = rows_b_ref[pl.ds(b, 1), :]
        rc = rows_c_ref[pl.ds(b, 1), :]
        n_all = n_ref[pl.ds(b, 1), :]
        m_new_row = jnp.zeros((1, 128), F32)
        o_rows, h_rows, n_rows = [], [], []
        for h in range(N_HEADS):
            sl = slice(h * HEAD_DIM, (h + 1) * HEAD_DIM)
            colblk = lambda v: cols[(v * N_HEADS + h) * HEAD_DIM:(v * N_HEADS + h + 1) * HEAD_DIM, :]
            s = s_ref[j, h]
            ks = jnp.sum(colblk(1) * s, axis=0, keepdims=True)
            qs = jnp.sum(colblk(0) * s, axis=0, keepdims=True)
            beta, eg, qk = sc(SC_BETA + h), sc(SC_EG + h), sc(SC_QKB + h)
            v_row = rb[:, 2 * GROUP_W + h * HEAD_DIM:2 * GROUP_W + (h + 1) * HEAD_DIM]
            v_new = beta * v_row - (beta * eg) * ks
            o_rows.append(eg * qs + qk * v_new)
            s_out_ref[j, h] = s * eg + colblk(1) * v_new
            cm = c_ref[j, h]
            qc = jnp.sum(colblk(2) * cm, axis=0, keepdims=True)
            q_row = rc[:, sl]
            k_row = rc[:, GROUP_W + h * HEAD_DIM:GROUP_W + (h + 1) * HEAD_DIM]
            vc_row = rc[:, 2 * GROUP_W + h * HEAD_DIM:2 * GROUP_W + (h + 1) * HEAD_DIM]
            n_row = n_all[:, sl]
            li, lf, qkc = sc(SC_LI + h), sc(SC_LF + h), sc(SC_QKC + h)
            m_old = _col(m_row, h)
            bb = lf + m_old
            m_t = jnp.maximum(bb, li)
            w_intra = jnp.exp(li - m_t) * qkc
            w_inter = jnp.exp(bb - m_t)
            num = w_inter * qc + w_intra * vc_row
            den = w_inter * jnp.sum(q_row * n_row, axis=-1, keepdims=True) + w_intra
            h_rows.append(num / jnp.maximum(jnp.abs(den), jnp.exp(-m_t)))
            m_new = jnp.maximum(m_old + lf, li)
            dec = jnp.exp(m_old + lf - m_new)
            e = jnp.exp(li - m_new)
            c_out_ref[j, h] = cm * dec + colblk(3) * (e * vc_row)
            n_rows.append(n_row * dec + e * k_row)
            m_new_row = jnp.where(lane_id == h, m_new, m_new_row)
        ob_ref[pl.ds(b, 1), :] = jnp.concatenate(o_rows, axis=1)
        hc_ref[pl.ds(b, 1), :] = jnp.concatenate(h_rows, axis=1)
        n_out_ref[pl.ds(b, 1), :] = jnp.concatenate(n_rows, axis=1)
        m_out_ref[pl.ds(b, 1), :] = m_new_row
        return carry

    lax.fori_loop(0, tb, body, 0, unroll=2)

    @pl.when(i == pl.num_programs(0) - 1)
    def _():
        ys_b, ys_c = [], []
        for h in range(N_HEADS):
            sl = slice(h * HEAD_DIM, (h + 1) * HEAD_DIM)
            o = ob_ref[:, sl]
            o = o * lax.rsqrt(jnp.mean(o * o, axis=-1, keepdims=True) + EPS) * dng_ref[...]
            ys_b.append(o * _silu(z_ref[:, sl]))
            hh = _sigmoid(og_ref[:, sl]) * hc_ref[:, sl]
            hh = hh * lax.rsqrt(jnp.mean(hh * hh, axis=-1, keepdims=True) + EPS) * mlg_ref[h:h + 1, :]
            ys_c.append(hh)
        yb_ref[...] = jnp.concatenate(ys_b, axis=1).astype(yb_ref.dtype)
        yc_ref[...] = jnp.concatenate(ys_c, axis=1).astype(yc_ref.dtype)


def sample_rec(cols, rows_b, rows_c, scal, P, n_st, m_st, dn_norm_g, ml_norm_g, s_all, c_all, s_acc, c_acc,
               layer, *, tb=8):
    nb = P.shape[0]
    W3 = 3 * GROUP_W
    full = lambda shape: pl.BlockSpec(shape, lambda i: (0,) * len(shape))
    state = pl.BlockSpec((None, tb, N_HEADS, HEAD_DIM, HEAD_DIM), lambda i: (layer, i, 0, 0, 0))
    untouched = pl.BlockSpec(memory_space=pl.ANY)
    return pl.pallas_call(
        functools.partial(_sample_rec_kernel, tb=tb),
        grid=(nb // tb,),
        in_specs=[full(cols.shape), full((nb, W3)), full((nb, W3)), full((nb, 128)),
                  pl.BlockSpec((nb, GROUP_W), lambda i: (0, COL_BZ)),
                  pl.BlockSpec((nb, GROUP_W), lambda i: (0, COL_CO)),
                  full((nb, GROUP_W)), full((nb, 128)), full((1, HEAD_DIM)), full((N_HEADS, HEAD_DIM)),
                  state, state, untouched, untouched],
        out_specs=[full((nb, GROUP_W)), full((nb, GROUP_W)), state, state,
                   full((nb, GROUP_W)), full((nb, 128))],
        out_shape=[jax.ShapeDtypeStruct((nb, GROUP_W), BF16),
                   jax.ShapeDtypeStruct((nb, GROUP_W), BF16),
                   jax.ShapeDtypeStruct(s_all.shape, F32),
                   jax.ShapeDtypeStruct(c_all.shape, F32),
                   jax.ShapeDtypeStruct((nb, GROUP_W), F32),
                   jax.ShapeDtypeStruct((nb, 128), F32)],
        input_output_aliases={12: 2, 13: 3},
        scratch_shapes=[pltpu.VMEM((nb, GROUP_W), F32), pltpu.VMEM((nb, GROUP_W), F32)],
        compiler_params=_cparams(("arbitrary",), VMEM_LIMIT),
    )(cols, rows_b, rows_c, scal, P, P, n_st, m_st, dn_norm_g.reshape(1, HEAD_DIM), ml_norm_g, s_all, c_all,
      s_acc, c_acc)


def _sample_xattn_kernel(q_ref, k_ref, v_ref, o_ref, *, tb):
    i = pl.program_id(0)
    n_rows = k_ref.shape[1]
    n_rep = n_rows // (2 * N_HEADS)
    ones = jnp.ones((HEAD_DIM, HEAD_DIM), BF16)
    fold = lambda x: x + pltpu.roll(x, N_HEADS, 0)

    def body(j, carry):
        b = i * tb + j
        q_row = q_ref[pl.ds(b, 1), :]
        heads = [q_row[:, h * HEAD_DIM:(h + 1) * HEAD_DIM] for h in range(N_HEADS)]
        q8 = jnp.concatenate(heads + heads, axis=0) * HEAD_DIM ** -0.5
        k3 = k_ref[j].reshape(n_rep, 2 * N_HEADS, HEAD_DIM)
        prod = (k3 * q8[None]).reshape(n_rows, HEAD_DIM)
        s = _bdot(prod, ones)
        s3 = s.reshape(n_rep, 2 * N_HEADS, HEAD_DIM)
        mx = jnp.max(s3, axis=0)
        mx = jnp.maximum(mx, pltpu.roll(mx, N_HEADS, 0))
        e3 = jnp.exp(s3 - mx[None])
        den = fold(jnp.sum(e3, axis=0))
        v3 = v_ref[j].reshape(n_rep, 2 * N_HEADS, HEAD_DIM)
        o8 = fold(jnp.sum(e3 * v3, axis=0)) / den
        o_ref[pl.ds(b, 1), :] = jnp.concatenate([o8[h:h + 1, :] for h in range(N_HEADS)], axis=1)
        return carry

    lax.fori_loop(0, tb, body, 0, unroll=4)


def sample_xattn(q, k_all, v_all, layer, *, tb=16):
    nb = q.shape[0]
    n_rows = k_all.shape[2]
    kv = pl.BlockSpec((None, tb, n_rows, HEAD_DIM), lambda i: (layer, i, 0, 0))
    return pl.pallas_call(
        functools.partial(_sample_xattn_kernel, tb=tb),
        grid=(nb // tb,),
        in_specs=[pl.BlockSpec((nb, GROUP_W), lambda i: (0, 0)), kv, kv],
        out_specs=pl.BlockSpec((nb, GROUP_W), lambda i: (0, 0)),
        out_shape=jax.ShapeDtypeStruct((nb, GROUP_W), F32),
        compiler_params=_cparams(("arbitrary",), VMEM_LIMIT),
    )(q, k_all, v_all)


W_B0 = 5 * GROUP_W
W_C0 = W_B0 + 2 * N_HEADS
W_C1 = W_C0 + 4 * GROUP_W
W_D0 = W_C1 + 2 * N_HEADS
W_D1 = W_D0 + 3 * GROUP_W


def _w_in_prep_kernel(w_ref, o_ref):
    tk = w_ref.shape[1]
    o_ref[:W_B0, :] = w_ref[:W_B0, :].astype(BF16)
    o_ref[W_B0:W_B0 + (W_C1 - W_C0), :] = w_ref[W_C0:W_C1, :].astype(BF16)
    o_ref[W_B0 + (W_C1 - W_C0):GATE_COL0, :] = w_ref[W_D0:W_D1, :].astype(BF16)
    gates = jnp.concatenate([w_ref[W_B0:W_C0, :], w_ref[W_C1:W_D0, :],
                             jnp.zeros((128 - 4 * N_HEADS, tk), F32)], axis=0)
    o_ref[GATE_COL0:, :] = gates.astype(BF16)


def _prep_w_in(w_in, *, tk=256):
    w_t = jnp.swapaxes(w_in, 1, 2)
    depth, n_in, K = w_t.shape
    assert n_in == W_D1
    return pl.pallas_call(
        _w_in_prep_kernel,
        grid=(depth, K // tk),
        in_specs=[pl.BlockSpec((None, n_in, tk), lambda l, i: (l, 0, i))],
        out_specs=pl.BlockSpec((None, IN_W_PAD, tk), lambda l, i: (l, 0, i)),
        out_shape=jax.ShapeDtypeStruct((depth, IN_W_PAD, K), BF16),
        compiler_params=_cparams(("parallel", "parallel"), VMEM_LIMIT),
    )(w_t)


def _ffn(x, g_pre, g_post, wg, wu, wd, layer, *, tm_up, tm_down, f32_up=False, up_layer=None):
    up_layer = layer if up_layer is None else up_layer
    up = functools.partial(norm_matmul, x, g_pre, tm=tm_up, tn=FFN_TN, out_dtype=BF16)
    if not f32_up:
        return matmul_resnorm([up([wg, wu], up_layer)], wd, layer, g_post, x, scale=0.5, tm=tm_down)
    n_tiles = x.shape[0] // tm_up
    act, wg_b, wu_b = up([wg, wu], up_layer, emit_bf16=True, row_tiles=(0, 1))
    if n_tiles > 1:
        act = up([wg_b[None], wu_b[None]], 0, row_tiles=(1, n_tiles - 1), out_into=act)
    return matmul_resnorm([act], wd, layer, g_post, x, scale=0.5, tm=tm_down), (wg_b, wu_b)


def _prompt_layer(x, mem2d, B, T, lw, layer):
    g = lw['norm_g']
    x, cast1 = _ffn(x, g[0], g[1], lw['ffn1_wg'], lw['ffn1_wu'], lw['ffn1_wd'], layer,
                    tm_up=1024, tm_down=512, f32_up=True)
    P = norm_matmul(x, g[2], [lw['w_in']], layer, tm=1024, tn=IN_TN, out_dtype=F32, w_is_nk=True)
    ya, yd, yb, yc, pool_tail, sc_tail, dn_s, dn_tail, ml_c, ml_nm = mixers_prompt(P, B, T, lw)
    x = matmul_resnorm([ya, yb, yc, yd], lw['w_out'], layer, g[3], x, scale=1.0, tm=512)
    kv = norm_matmul(mem2d, g[8], [lw['x_wkv']], layer, tm=1024, tn=512, out_dtype=F32)
    x = xattn_prompt(x, B, T, g[4], lw['x_wq'], kv, lw['x_wo'], layer, g[5])
    x, cast2 = _ffn(x, g[6], g[7], lw['ffn2_wg'], lw['ffn2_wu'], lw['ffn2_wd'], layer,
                    tm_up=1024, tm_down=512, f32_up=True)
    states = (pool_tail[:, 16 - POOL_STATE:], dn_tail[:, 8 - (DN_CONV - 1):], dn_s, ml_c,
              ml_nm[:, :N_HEADS], ml_nm[:, N_HEADS:, 0], sc_tail[:, 8 - (SC_WIDTH - 1):])
    mem_k = kv[:, :GROUP_W].reshape(B, MEM_LEN, N_HEADS, HEAD_DIM)
    mem_v = kv[:, GROUP_W:].reshape(B, MEM_LEN, N_HEADS, HEAD_DIM)
    return x, states, mem_k, mem_v, cast1 + cast2


def _sample_layer(x, st, big, acc, layer, lw, start_pos):
    pool_st, dnc_st, ml_n, ml_m, scc_st = st
    s_all, c_all, k_all, v_all = big
    s_acc, c_acc = acc
    nb = x.shape[0]
    g = lw['norm_g']
    wg1, wu1, wg2, wu2 = (w[None] for w in lw['ffn_cast'])
    x = _ffn(x, g[0], g[1], wg1, wu1, lw['ffn1_wd'], layer, tm_up=128, tm_down=128, up_layer=0)
    P = norm_matmul(x, g[2], [lw['w_in']], layer, tm=128, tn=IN_TN, out_dtype=F32, w_is_nk=True)
    (ya, yd, pool_new, dnc_new, scc_new, rows_b, rows_c, scal, cols) = sample_pre(
        P, pool_st.reshape(nb, -1), dnc_st.reshape(nb, -1), scc_st.reshape(nb, -1),
        lw['pool_w'], lw['pool_scale'], lw['dn_conv_w'], lw['dn_A_log'], lw['dn_dt_bias'],
        lw['ml_b_i'], lw['ml_b_f'], lw['sc_conv_w'], start_pos)
    m_pad = jnp.pad(ml_m, ((0, 0), (0, 128 - N_HEADS)))
    yb, yc, s_acc, c_acc, n_new, m_new = sample_rec(
        cols, rows_b, rows_c, scal, P, ml_n.reshape(nb, GROUP_W), m_pad,
        lw['dn_norm_g'], lw['ml_norm_g'], s_all, c_all, s_acc, c_acc, layer)
    x = matmul_resnorm([ya, yb, yc, yd], lw['w_out'], layer, g[3], x, scale=1.0, tm=128)
    q = norm_matmul(x, g[4], [lw['x_wq']], layer, tm=128, tn=GROUP_W, out_dtype=F32)
    o = sample_xattn(q, k_all, v_all, layer)
    x = matmul_resnorm([o], lw['x_wo'], layer, g[5], x, scale=1.0, tm=128)
    x = _ffn(x, g[6], g[7], wg2, wu2, lw['ffn2_wd'], layer, tm_up=128, tm_down=128, up_layer=0)
    states = (pool_new.reshape(pool_st.shape), dnc_new.reshape(dnc_st.shape),
              n_new.reshape(ml_n.shape), m_new[:, :N_HEADS], scc_new.reshape(scc_st.shape))
    return x, states, (s_acc, c_acc)


def kernel(x_prompt, x_sample, mem_prompt, state_pool, state_dn_conv, state_dn_S, state_ml_C, state_ml_n,
           state_ml_m, state_sc_conv, cache_mem_k, cache_mem_v, norm_g, w_in, w_out, pool_w, pool_scale,
           dn_conv_w, dn_A_log, dn_dt_bias, dn_norm_g, ml_b_i, ml_b_f, ml_norm_g, sc_conv_w,
           x_wq, x_wk, x_wv, x_wo, ffn1_wg, ffn1_wu, ffn1_wd, ffn2_wg, ffn2_wu, ffn2_wd):
    depth = norm_g.shape[0]
    B, T, _ = x_prompt.shape
    nb, t_dec, _ = x_sample.shape
    assert t_dec == 1
    start_pos = PAST_LEN

    stacked = dict(w_in=_prep_w_in(w_in), w_out=w_out.astype(BF16), x_wq=x_wq.astype(BF16),
                   x_wkv=jnp.concatenate([x_wk.astype(BF16), x_wv.astype(BF16)], axis=-1),
                   x_wo=x_wo.astype(BF16),
                   ffn1_wg=ffn1_wg, ffn1_wu=ffn1_wu, ffn1_wd=ffn1_wd.astype(BF16),
                   ffn2_wg=ffn2_wg, ffn2_wu=ffn2_wu, ffn2_wd=ffn2_wd.astype(BF16))

    def layer_weights(l):
        return dict(stacked, norm_g=norm_g[l], pool_w=pool_w[l],
                    pool_scale=pool_scale[l], dn_conv_w=dn_conv_w[l], dn_A_log=dn_A_log[l],
                    dn_dt_bias=dn_dt_bias[l], dn_norm_g=dn_norm_g[l], ml_b_i=ml_b_i[l], ml_b_f=ml_b_f[l],
                    ml_norm_g=ml_norm_g[l], sc_conv_w=sc_conv_w[l])

    mem2d = mem_prompt.reshape(B * MEM_LEN, D_MODEL)
    h = x_prompt.reshape(B * T, D_MODEL)
    p_states, mem_k_list, mem_v_list, ffn_cast = [], [], [], []
    for l in range(depth):
        h, ns, mk, mv, cast = _prompt_layer(h, mem2d, B, T, layer_weights(l), l)
        ffn_cast.append(cast)
        p_states.append(ns)
        mem_k_list.append(mk)
        mem_v_list.append(mv)
    y_prompt = h.reshape(B, T, D_MODEL)

    s_inputs = (state_pool, state_dn_conv, state_ml_n, state_ml_m, state_sc_conv)
    big = (state_dn_S, state_ml_C,
           cache_mem_k.reshape(depth, nb, MEM_LEN * N_HEADS, HEAD_DIM),
           cache_mem_v.reshape(depth, nb, MEM_LEN * N_HEADS, HEAD_DIM))
    h = x_sample.reshape(nb, D_MODEL)
    s_states = []
    acc = (lax.empty(state_dn_S.shape, F32), lax.empty(state_ml_C.shape, F32))
    for l in range(depth):
        st = tuple(s[l] for s in s_inputs)
        h, ns, acc = _sample_layer(h, st, big, acc, l, dict(layer_weights(l), ffn_cast=ffn_cast[l]), start_pos)
        s_states.append(ns)
    y_sample = h.reshape(nb, 1, D_MODEL)
    dn_S_s, ml_C_s = acc

    pool_p, dn_conv_p, dn_S_p, ml_C_p, ml_n_p, ml_m_p, sc_conv_p = [jnp.stack(z) for z in zip(*p_states)]
    pool_s, dn_conv_s, ml_n_s, ml_m_s, sc_conv_s = [jnp.stack(z) for z in zip(*s_states)]
    mem_k_p = jnp.stack(mem_k_list)
    mem_v_p = jnp.stack(mem_v_list)
    return (y_prompt, y_sample, pool_p, pool_s, dn_conv_p, dn_conv_s, dn_S_p, dn_S_s, ml_C_p, ml_C_s,
            ml_n_p, ml_n_s, ml_m_p, ml_m_s, sc_conv_p, sc_conv_s, mem_k_p, mem_v_p)
```

```python
import functools

import jax
import jax.numpy as jnp
from jax import lax
from jax.experimental import pallas as pl
from jax.experimental.pallas import tpu as pltpu

F32 = jnp.float32
BF16 = jnp.bfloat16

EPS = 1e-6
D_MODEL = 2048
GROUP_W = 512
HEAD_DIM = 128
N_HEADS = GROUP_W // HEAD_DIM
CHUNK = 64
POOL_WINDOWS = (2, 4, 8, 16)
POOL_STATE = 15
DN_CONV = 4
SC_WIDTH = 3
MEM_LEN = 256
PAST_LEN = 16384
FFN_TN = 512
COL_A, COL_BQ, COL_BK, COL_BV, COL_BZ = 0, 1, 2, 3, 4
COL_CQ, COL_CK, COL_CV, COL_CO = 5, 6, 7, 8
COL_DB, COL_DC, COL_DH = 9, 10, 11
GATE_COL0 = 12 * GROUP_W
IN_W_PAD = GATE_COL0 + 128
IN_TN = 1280
LANE_BETA, LANE_A, LANE_I, LANE_F = 0, 4, 8, 12
TB = 256
SUB = 128
VMEM_LIMIT = 60 * 2**20


def _cparams(sem, vmem=None):
    return pltpu.CompilerParams(dimension_semantics=sem, vmem_limit_bytes=vmem)


def _bdot(a, b):
    return jnp.dot(a.astype(BF16), b.astype(BF16), preferred_element_type=F32)


def _bdot_nt(a, b):
    return lax.dot_general(a.astype(BF16), b.astype(BF16), (((1,), (1,)), ((), ())),
                           preferred_element_type=F32)


def _sigmoid(x):
    return 0.5 * jnp.tanh(0.5 * x) + 0.5


def _silu(x):
    return x * _sigmoid(x)


def _softplus(x):
    return jnp.maximum(x, 0.0) + jnp.log(1.0 + jnp.exp(-jnp.abs(x)))


def _col(x, idx):
    return x[:, idx:idx + 1]


def _resident(shape):
    nd = len(shape)
    return pl.BlockSpec(shape, lambda *_: (0,) * nd, pipeline_mode=pl.Buffered(1))


def _norm_into(x_ref, g_ref, h_ref):
    tm = x_ref.shape[0]
    rs = min(tm, 128)

    def body(i, carry):
        r = pl.multiple_of(i * rs, rs)
        x = x_ref[pl.ds(r, rs), :]
        ms = jnp.mean(x * x, axis=-1, keepdims=True)
        h_ref[pl.ds(r, rs), :] = (x * lax.rsqrt(ms + EPS) * g_ref[...]).astype(BF16)
        return carry

    lax.fori_loop(0, tm // rs, body, 0)


def _zero_past_width(y, n_valid):
    tn = y.shape[1]
    if n_valid % tn == 0:
        return y
    col = pl.program_id(1) * tn + lax.broadcasted_iota(jnp.int32, (1, tn), 1)
    return jnp.where(col < n_valid, y, 0.0)


def _norm_mm_kernel(x_ref, g_ref, w_ref, o_ref, h_ref, *, n_valid, w_is_nk):
    @pl.when(pl.program_id(1) == 0)
    def _():
        _norm_into(x_ref, g_ref, h_ref)

    if w_is_nk:
        y = lax.dot_general(h_ref[...], w_ref[...], (((1,), (1,)), ((), ())), preferred_element_type=F32)
    else:
        y = jnp.dot(h_ref[...], w_ref[...], preferred_element_type=F32)
    o_ref[...] = _zero_past_width(y, n_valid).astype(o_ref.dtype)


def _norm_swiglu_kernel(x_ref, g_ref, wg_ref, wu_ref, *rest, n_valid, emit_bf16):
    o_ref, h_ref = rest[0], rest[-1]

    @pl.when(pl.program_id(1) == 0)
    def _():
        _norm_into(x_ref, g_ref, h_ref)

    h = h_ref[...]
    wg = wg_ref[...].astype(BF16)
    wu = wu_ref[...].astype(BF16)
    a = jnp.dot(h, wg, preferred_element_type=F32)
    b = jnp.dot(h, wu, preferred_element_type=F32)
    o_ref[...] = _zero_past_width(_silu(a) * b, n_valid).astype(o_ref.dtype)
    if emit_bf16:
        wg_out_ref, wu_out_ref = rest[1:3]
        wg_out_ref[...] = wg
        wu_out_ref[...] = wu


def norm_matmul(x, g, ws, layer, *, tm, tn, out_dtype, w_is_nk=False, emit_bf16=False, row_tiles=None):
    M, K = x.shape
    n_valid = ws[0].shape[1 if w_is_nk else 2]
    n_steps = pl.cdiv(n_valid, tn)
    N = n_steps * tn
    tm = min(tm, M)
    first, count = row_tiles if row_tiles is not None else (0, M // tm)
    if len(ws) == 2:
        assert not w_is_nk and (count == 1 or not emit_bf16)
        kern = functools.partial(_norm_swiglu_kernel, n_valid=n_valid, emit_bf16=emit_bf16)
    else:
        assert not emit_bf16
        kern = functools.partial(_norm_mm_kernel, n_valid=n_valid, w_is_nk=w_is_nk)
    w_spec = (pl.BlockSpec((None, tn, K), lambda i, j: (layer, j, 0)) if w_is_nk
              else pl.BlockSpec((None, K, tn), lambda i, j: (layer, 0, j)))
    out_specs = pl.BlockSpec((tm, tn), lambda i, j: (i, j))
    out_shape = jax.ShapeDtypeStruct((count * tm, N), out_dtype)
    if emit_bf16:
        w_out = pl.BlockSpec((K, tn), lambda i, j: (0, j))
        out_specs = [out_specs, w_out, w_out]
        out_shape = [out_shape] + [jax.ShapeDtypeStruct((K, n_valid), BF16)] * 2
    return pl.pallas_call(
        kern,
        grid=(count, n_steps),
        in_specs=[pl.BlockSpec((tm, K), lambda i, j: (i + first, 0)),
                  pl.BlockSpec((1, K), lambda i, j: (0, 0))]
        + [w_spec for _ in ws],
        out_specs=out_specs,
        out_shape=out_shape,
        scratch_shapes=[pltpu.VMEM((tm, K), BF16)],
        compiler_params=_cparams(("parallel", "arbitrary"), VMEM_LIMIT),
    )(x, g.reshape(1, K), *ws)


def _mm_resnorm_kernel(*refs, n_a, scale, top_tiles):
    a_refs = refs[:n_a]
    w_ref, g_ref, res_ref, o_ref = refs[n_a:]

    def finish(a):
        y = jnp.dot(a, w_ref[...], preferred_element_type=F32)
        ms = jnp.mean(y * y, axis=-1, keepdims=True)
        o_ref[...] = res_ref[...] + scale * (y * lax.rsqrt(ms + EPS) * g_ref[...])

    if top_tiles is None:
        parts = [r[...].astype(BF16) for r in a_refs]
        finish(parts[0] if n_a == 1 else jnp.concatenate(parts, axis=1))
    else:
        in_top = pl.program_id(0) < top_tiles
        pl.when(in_top)(lambda: finish(a_refs[0][...].astype(BF16)))
        pl.when(jnp.logical_not(in_top))(lambda: finish(a_refs[1][...].astype(BF16)))


def matmul_resnorm(a_list, w, layer, g, res, *, scale, tm, row_pieces=False):
    M, N = res.shape
    tm = min(tm, M)
    K = w.shape[1]
    top_tiles = None
    if row_pieces:
        top, bottom = a_list
        assert top.shape[0] % tm == 0 and top.shape[0] + bottom.shape[0] == M
        top_tiles = top.shape[0] // tm
        a_specs = [pl.BlockSpec((tm, K), lambda i: (jnp.minimum(i, top_tiles - 1), 0)),
                   pl.BlockSpec((tm, K), lambda i: (jnp.maximum(i - top_tiles, 0), 0))]
    else:
        widths = [K] if len(a_list) == 1 else [a.shape[1] for a in a_list]
        assert sum(widths) == K
        a_specs = [pl.BlockSpec((tm, wd), lambda i: (i, 0)) for wd in widths]
    w_spec = pl.BlockSpec((None, K, N), lambda i: (layer, 0, 0), pipeline_mode=pl.Buffered(1))
    return pl.pallas_call(
        functools.partial(_mm_resnorm_kernel, n_a=len(a_list), scale=scale, top_tiles=top_tiles),
        grid=(M // tm,),
        in_specs=a_specs
        + [w_spec, _resident((1, N)), pl.BlockSpec((tm, N), lambda i: (i, 0))],
        out_specs=pl.BlockSpec((tm, N), lambda i: (i, 0)),
        out_shape=jax.ShapeDtypeStruct((M, N), F32),
        compiler_params=_cparams(("parallel",), VMEM_LIMIT),
    )(*a_list, w, g.reshape(1, N), res)


def _chunk_cumsum(x):
    n = x.shape[0]
    row = lax.broadcasted_iota(jnp.int32, (n, 1), 0) % CHUNK
    s = 1
    while s < CHUNK:
        x = x + jnp.where(row >= s, pltpu.roll(x, s, 0), 0.0)
        s *= 2
    return x


def _chunk_last(x):
    n = x.shape[0]
    parts = [jnp.broadcast_to(x[c * CHUNK + CHUNK - 1:c * CHUNK + CHUNK, :], (CHUNK, x.shape[1]))
             for c in range(n // CHUNK)]
    return jnp.concatenate(parts, axis=0)


def _pad_rows(x, c, n_chunks):
    z = jnp.zeros_like(x)
    return jnp.concatenate([x if i == c else z for i in range(n_chunks)], axis=0)


def _blk_masks(n):
    r = lax.broadcasted_iota(jnp.int32, (n, n), 0)
    c = lax.broadcasted_iota(jnp.int32, (n, n), 1)
    same = (r ^ c) < CHUNK
    return same & (c <= r), same & (c < r), r == c


def _split(a):
    ah = a.astype(BF16)
    return ah, (a - ah.astype(F32)).astype(BF16)


def _split_dot(a, b):
    d = lambda x, y: jnp.dot(x, y, preferred_element_type=F32)
    return d(a[0], b[0]) + d(a[0], b[1]) + d(a[1], b[0])


def _unit_lower_inverses(a_list, n):
    r = lax.broadcasted_iota(jnp.int32, (n, n), 0)
    c = lax.broadcasted_iota(jnp.int32, (n, n), 1)
    rc = r ^ c
    pair = lambda s: (rc >= s) & (rc < 2 * s)
    eye = (r == c).astype(F32)
    xs = [eye - jnp.where(pair(1), a, 0.0) for a in a_list]
    s = 2
    while s < CHUNK:
        m = pair(s)
        x_sp = [_split(x) for x in xs]
        ts = [_split_dot(_split(jnp.where(m, a, 0.0)), x) for a, x in zip(a_list, x_sp)]
        xs = [x - _split_dot(xp, _split(t)) for x, xp, t in zip(xs, x_sp, ts)]
        s *= 2
    return xs


def _pool_sconv_step(ua_ref, db_ref, dc_ref, dh_ref, pw_ref, ps_ref, sw_ref, ya_ref, yd_ref, hist_a, hist_d,
                     *, tb):
    i = pl.program_id(1)

    u = ua_ref[...]
    ext = jnp.concatenate([hist_a[...], u], axis=0)
    a2 = ext + pltpu.roll(ext, 1, 0)
    a4 = a2[:, 128:] + pltpu.roll(a2[:, 128:], 2, 0)
    a8 = a4[:, 128:] + pltpu.roll(a4[:, 128:], 4, 0)
    a16 = a8[:, 128:] + pltpu.roll(a8[:, 128:], 8, 0)
    sums = (a2[16:, :128], a4[16:, :128], a8[16:, :128], a16[16:, :])
    pos = i * tb + lax.broadcasted_iota(jnp.int32, (tb, 1), 0)
    ys = []
    for g, w in enumerate(POOL_WINDOWS):
        cnt = jnp.minimum(pos + 1, w).astype(F32)
        d = sums[g] / cnt - u[:, g * 128:(g + 1) * 128]
        ys.append(_bdot(d, pw_ref[g]))
    ya_ref[...] = (jnp.concatenate(ys, axis=1) * ps_ref[...]).astype(ya_ref.dtype)
    hist_a[...] = u[tb - 16:, :]

    us = dc_ref[...] * dh_ref[...]
    ext = jnp.concatenate([hist_d[...], us], axis=0)
    sw = sw_ref[...]
    y = sw[0:1] * pltpu.roll(ext, 2, 0) + sw[1:2] * pltpu.roll(ext, 1, 0) + sw[2:3] * ext
    yd_ref[...] = (db_ref[...] * y[8:]).astype(yd_ref.dtype)
    hist_d[...] = us[tb - 8:, :]
    return u[tb - 16:, :], us[tb - 8:, :]


def _deltanet_step(q_ref, k_ref, v_ref, z_ref, gt_ref, cw_ref, alog_ref, dtb_ref, ng_ref, y_ref, s_ref, hist_ref,
                   *, tb):
    nc = tb // CHUNK

    qkv = jnp.concatenate([q_ref[...], k_ref[...], v_ref[...]], axis=1)
    ext = jnp.concatenate([hist_ref[...], qkv], axis=0)
    cw = cw_ref[...]
    conv = (cw[0:1] * pltpu.roll(ext, 3, 0) + cw[1:2] * pltpu.roll(ext, 2, 0)
            + cw[2:3] * pltpu.roll(ext, 1, 0) + cw[3:4] * ext)[8:]
    act = _silu(conv)
    hist_ref[...] = qkv[tb - 8:, :]

    gates = gt_ref[...]
    beta_all = _sigmoid(gates)
    g_all = -jnp.exp(alog_ref[...]) * _softplus(gates + dtb_ref[...])
    gcum_all = _chunk_cumsum(g_all)
    glast_all = _chunk_last(gcum_all)
    gcum_t = gcum_all.T
    causal, strict, _ = _blk_masks(SUB)

    heads = range(N_HEADS)
    subs = range(tb // SUB)
    cps = SUB // CHUNK
    hd = lambda base, h: slice(base + h * HEAD_DIM, base + (h + 1) * HEAD_DIM)
    qs, ks, gcs, gls, betas = [], [], [], [], []
    for h in heads:
        qh = act[:, hd(0, h)]
        kh = act[:, hd(GROUP_W, h)]
        qs.append(qh * lax.rsqrt(jnp.sum(qh * qh, axis=-1, keepdims=True) + EPS) * HEAD_DIM ** -0.5)
        ks.append(kh * lax.rsqrt(jnp.sum(kh * kh, axis=-1, keepdims=True) + EPS))
        betas.append(_col(beta_all, LANE_BETA + h))
        gcs.append(_col(gcum_all, LANE_A + h))
        gls.append(_col(glast_all, LANE_A + h))
    k_ts, a_mats, qks = {}, [], {}
    for st in subs:
        R = slice(st * SUB, (st + 1) * SUB)
        for h in heads:
            gr = gcum_t[LANE_A + h:LANE_A + h + 1, R]
            decay = jnp.where(causal, jnp.exp(jnp.where(causal, gcs[h][R] - gr, 0.0)), 0.0)
            k_t = ks[h][R].T.astype(BF16)
            a_mats.append(jnp.where(strict, _bdot(ks[h][R] * betas[h][R], k_t) * decay, 0.0))
            qks[st, h] = jnp.where(causal, _bdot(qs[h][R], k_t) * decay, 0.0).astype(BF16)
            k_ts[st, h] = k_t
    tinvs = _unit_lower_inverses(a_mats, SUB)
    us, ws = {}, {}
    for st in subs:
        R = slice(st * SUB, (st + 1) * SUB)
        for h in heads:
            vb = act[R, hd(2 * GROUP_W, h)] * betas[h][R]
            kbe = ks[h][R] * (betas[h][R] * jnp.exp(gcs[h][R]))
            uw = _bdot(tinvs[st * N_HEADS + h], jnp.concatenate([vb, kbe], axis=1))
            us[st, h] = uw[:, :HEAD_DIM]
            ws[st, h] = uw[:, HEAD_DIM:].astype(BF16)
    q_decs = [(qs[h] * jnp.exp(gcs[h])).astype(BF16) for h in heads]
    tails = [jnp.exp(gls[h] - gcs[h]) for h in heads]
    s_hs = [s_ref[h] for h in heads]
    outs = [[] for _ in heads]
    for c in range(nc):
        r = slice(c * CHUNK, (c + 1) * CHUNK)
        st, cl = divmod(c, cps)
        rl = slice(cl * CHUNK, (cl + 1) * CHUNK)
        for h in heads:
            s_b = s_hs[h].astype(BF16)
            v_new = us[st, h][rl] - jnp.dot(ws[st, h][rl], s_b, preferred_element_type=F32)
            outs[h].append(jnp.dot(q_decs[h][r], s_b, preferred_element_type=F32)
                           + _bdot(qks[st, h][rl], _pad_rows(v_new, cl, cps)))
            s_dec = jnp.exp(jnp.broadcast_to(gls[h][c * CHUNK:c * CHUNK + 1], (HEAD_DIM, 1)))
            s_hs[h] = s_hs[h] * s_dec + _bdot(k_ts[st, h], _pad_rows(v_new * tails[h][r], cl, cps))
    ys = []
    for h in heads:
        s_ref[h] = s_hs[h]
        o = jnp.concatenate(outs[h], axis=0)
        o = o * lax.rsqrt(jnp.mean(o * o, axis=-1, keepdims=True) + EPS) * ng_ref[...]
        ys.append(o * _silu(z_ref[:, hd(0, h)]))
    y_ref[...] = jnp.concatenate(ys, axis=1).astype(y_ref.dtype)
    return qkv[tb - 8:, :]


def _mlstm_step(q_ref, k_ref, v_ref, og_ref, gt_ref, bi_ref, bf_ref, ng_ref, y_ref, c_ref, nm_ref, *, tb):
    nc = tb // CHUNK

    gates = gt_ref[...]
    li_all = gates + bi_ref[...]
    lf_all = -_softplus(-(gates + bf_ref[...]))
    f_all = _chunk_cumsum(lf_all)
    flast_all = _chunk_last(f_all)
    f_t = f_all.T
    li_t = li_all.T
    causal, _, _ = _blk_masks(SUB)

    heads = range(N_HEADS)
    subs = range(tb // SUB)
    cps = SUB // CHUNK
    hd = lambda h: slice(h * HEAD_DIM, (h + 1) * HEAD_DIM)
    qs, kss, vs, fcs, bs, evs, eks, decs = ([] for _ in range(8))
    for h in heads:
        kh = k_ref[:, hd(h)] * HEAD_DIM ** -0.5
        vh = v_ref[:, hd(h)]
        fc = _col(f_all, LANE_F + h)
        fl = _col(flast_all, LANE_F + h)
        a_w = fl - fc + _col(li_all, LANE_I + h)
        m_h = nm_ref[N_HEADS + h:N_HEADS + h + 1, 0:1]
        m_prev, m_next, dec_h = [], [], []
        for c in range(nc):
            r = slice(c * CHUNK, (c + 1) * CHUNK)
            fl_c = fl[c * CHUNK:c * CHUNK + 1]
            m_new = jnp.maximum(m_h + fl_c, jnp.max(a_w[r], axis=0, keepdims=True))
            dec_h.append(jnp.exp(m_h + fl_c - m_new))
            m_prev.append(jnp.broadcast_to(m_h, (CHUNK, 1)))
            m_next.append(jnp.broadcast_to(m_new, (CHUNK, 1)))
            m_h = m_new
        nm_ref[N_HEADS + h:N_HEADS + h + 1, :] = jnp.broadcast_to(m_h, (1, HEAD_DIM))
        e = jnp.exp(a_w - jnp.concatenate(m_next, axis=0))
        qs.append(q_ref[:, hd(h)])
        kss.append(kh)
        vs.append(vh)
        fcs.append(fc)
        bs.append(fc + jnp.concatenate(m_prev, axis=0))
        evs.append(e * vh)
        eks.append(e * kh)
        decs.append(dec_h)
    k_ts, intras, den_intras, w_inters, inv_floor = {}, {}, {}, {}, {}
    for st in subs:
        R = slice(st * SUB, (st + 1) * SUB)
        for h in heads:
            fr = f_t[LANE_F + h:LANE_F + h + 1, R]
            lir = li_t[LANE_I + h:LANE_I + h + 1, R]
            dm = jnp.where(causal, fcs[h][R] - fr + lir, -jnp.inf)
            m_t = jnp.maximum(bs[h][R], jnp.max(dm, axis=1, keepdims=True))
            k_t = kss[h][R].T.astype(BF16)
            w_intra = jnp.exp(dm - m_t) * _bdot(qs[h][R], k_t)
            k_ts[st, h] = k_t
            intras[st, h] = _bdot(w_intra, vs[h][R])
            den_intras[st, h] = jnp.sum(w_intra, axis=-1, keepdims=True)
            w_inters[st, h] = jnp.exp(bs[h][R] - m_t)
            inv_floor[st, h] = jnp.exp(-m_t)
    c_hs = [c_ref[h] for h in heads]
    n_hs = [nm_ref[h:h + 1, :] for h in heads]
    outs = [[] for _ in heads]
    for c in range(nc):
        r = slice(c * CHUNK, (c + 1) * CHUNK)
        st, cl = divmod(c, cps)
        rl = slice(cl * CHUNK, (cl + 1) * CHUNK)
        for h in heads:
            q_c = qs[h][r]
            w_inter = w_inters[st, h][rl]
            num = w_inter * _bdot(q_c, c_hs[h]) + intras[st, h][rl]
            den = w_inter * jnp.sum(q_c * n_hs[h], axis=-1, keepdims=True) + den_intras[st, h][rl]
            outs[h].append(num / jnp.maximum(jnp.abs(den), inv_floor[st, h][rl]))
            c_hs[h] = c_hs[h] * decs[h][c] + _bdot(k_ts[st, h], _pad_rows(evs[h][r], cl, cps))
            n_hs[h] = n_hs[h] * decs[h][c] + jnp.sum(eks[h][r], axis=0, keepdims=True)
    ys = []
    for h in heads:
        c_ref[h] = c_hs[h]
        nm_ref[h:h + 1, :] = n_hs[h]
        hh = jnp.concatenate(outs[h], axis=0)
        hh = _sigmoid(og_ref[:, hd(h)]) * hh
        hh = hh * lax.rsqrt(jnp.mean(hh * hh, axis=-1, keepdims=True) + EPS) * ng_ref[h:h + 1, :]
        ys.append(hh)
    y_ref[...] = jnp.concatenate(ys, axis=1).astype(y_ref.dtype)


N_POOL_IN, N_DN_IN, N_ML_IN = 7, 9, 8


def _mixers_kernel(*refs, tb):
    ins = iter(refs[:N_POOL_IN + N_DN_IN + N_ML_IN])
    take = lambda n: [next(ins) for _ in range(n)]
    pool_in, dn_in, ml_in = take(N_POOL_IN), take(N_DN_IN), take(N_ML_IN)
    (ya_ref, yd_ref, yb_ref, yc_ref, ptail_ref, stail_ref, s_out_ref, ctail_ref, c_out_ref, nm_out_ref,
     hist_a, hist_d, s_ref, hist_b, c_ref, nm_ref) = refs[N_POOL_IN + N_DN_IN + N_ML_IN:]
    i = pl.program_id(1)

    @pl.when(i == 0)
    def _():
        for r in (hist_a, hist_d, s_ref, hist_b, c_ref, nm_ref):
            r[...] = jnp.zeros_like(r)

    qkv_tail = _deltanet_step(*dn_in, yb_ref, s_ref, hist_b, tb=tb)
    _mlstm_step(*ml_in, yc_ref, c_ref, nm_ref, tb=tb)
    u_tail, us_tail = _pool_sconv_step(*pool_in, ya_ref, yd_ref, hist_a, hist_d, tb=tb)

    @pl.when(i == pl.num_programs(1) - 1)
    def _():
        ptail_ref[0] = u_tail
        stail_ref[0] = us_tail
        s_out_ref[0] = s_ref[...]
        ctail_ref[0] = qkv_tail
        c_out_ref[0] = c_ref[...]
        nm_out_ref[0] = nm_ref[...]


def mixers_prompt(P, B, T, lw):
    nT = T // TB
    M = B * T
    col = lambda c: pl.BlockSpec((TB, GROUP_W), lambda b, i: (b * nT + i, c))
    gates = pl.BlockSpec((TB, 128), lambda b, i: (b * nT + i, GATE_COL0 // 128))
    const = lambda *shape: pl.BlockSpec(shape, lambda b, i: (0,) * len(shape))
    lane_row = lambda v, lane: jnp.zeros((1, 128), F32).at[0, lane:lane + N_HEADS].set(v)
    y_spec = pl.BlockSpec((TB, GROUP_W), lambda b, i: (b * nT + i, 0))
    per_seq = lambda *shape: pl.BlockSpec((1,) + shape, lambda b, i: (b,) + (0,) * len(shape))
    mat = (N_HEADS, HEAD_DIM, HEAD_DIM)
    pool_specs = [col(COL_A), col(COL_DB), col(COL_DC), col(COL_DH),
                  const(4, 128, 128), const(1, GROUP_W), const(SC_WIDTH, GROUP_W)]
    dn_specs = [col(COL_BQ), col(COL_BK), col(COL_BV), col(COL_BZ), gates,
                const(DN_CONV, 3 * GROUP_W), const(1, 128), const(1, 128), const(1, HEAD_DIM)]
    ml_specs = [col(COL_CQ), col(COL_CK), col(COL_CV), col(COL_CO), gates,
                const(1, 128), const(1, 128), const(N_HEADS, HEAD_DIM)]
    assert (len(pool_specs), len(dn_specs), len(ml_specs)) == (N_POOL_IN, N_DN_IN, N_ML_IN)
    y_shape = jax.ShapeDtypeStruct((M, GROUP_W), BF16)
    state = lambda *shape: jax.ShapeDtypeStruct((B,) + shape, F32)
    return pl.pallas_call(
        functools.partial(_mixers_kernel, tb=TB),
        grid=(B, nT),
        in_specs=pool_specs + dn_specs + ml_specs,
        out_specs=[y_spec] * 4 + [per_seq(16, GROUP_W), per_seq(8, GROUP_W), per_seq(*mat),
                                  per_seq(8, 3 * GROUP_W), per_seq(*mat), per_seq(2 * N_HEADS, HEAD_DIM)],
        out_shape=[y_shape] * 4 + [state(16, GROUP_W), state(8, GROUP_W), state(*mat),
                                   state(8, 3 * GROUP_W), state(*mat), state(2 * N_HEADS, HEAD_DIM)],
        scratch_shapes=[pltpu.VMEM((16, GROUP_W), F32), pltpu.VMEM((8, GROUP_W), F32),
                        pltpu.VMEM(mat, F32), pltpu.VMEM((8, 3 * GROUP_W), F32),
                        pltpu.VMEM(mat, F32), pltpu.VMEM((2 * N_HEADS, HEAD_DIM), F32)],
        compiler_params=_cparams(("parallel", "arbitrary"), VMEM_LIMIT),
    )(P, P, P, P, lw['pool_w'], lw['pool_scale'].reshape(1, GROUP_W), lw['sc_conv_w'],
      P, P, P, P, P, lw['dn_conv_w'], lane_row(lw['dn_A_log'], LANE_A), lane_row(lw['dn_dt_bias'], LANE_A),
      lw['dn_norm_g'].reshape(1, HEAD_DIM),
      P, P, P, P, P, lane_row(lw['ml_b_i'], LANE_I), lane_row(lw['ml_b_f'], LANE_F), lw['ml_norm_g'])


def _xattn_prompt_kernel(x_ref, gpre_ref, wq_ref, kv_ref, wo_ref, gpost_ref, o_ref, h_ref):
    _norm_into(x_ref, gpre_ref, h_ref)
    q = jnp.dot(h_ref[...], wq_ref[...], preferred_element_type=F32)
    heads = range(N_HEADS)
    hd = lambda base, h: slice(base + h * HEAD_DIM, base + (h + 1) * HEAD_DIM)
    ss = [_bdot_nt(q[:, hd(0, h)], kv_ref[:, hd(0, h)]) * HEAD_DIM ** -0.5 for h in heads]
    es = [jnp.exp(s - jnp.max(s, axis=-1, keepdims=True)) for s in ss]
    outs = [_bdot(e, kv_ref[:, hd(GROUP_W, h)]) / jnp.sum(e, axis=-1, keepdims=True)
            for h, e in zip(heads, es)]
    o = jnp.concatenate(outs, axis=1).astype(BF16)
    y = jnp.dot(o, wo_ref[...], preferred_element_type=F32)
    ms = jnp.mean(y * y, axis=-1, keepdims=True)
    o_ref[...] = x_ref[...] + y * lax.rsqrt(ms + EPS) * gpost_ref[...]


def xattn_prompt(x, B, T, g_pre, wq, kv, wo, layer, g_post, *, tq=512):
    nq = T // tq
    M = B * T
    return pl.pallas_call(
        _xattn_prompt_kernel,
        grid=(B, nq),
        in_specs=[pl.BlockSpec((tq, D_MODEL), lambda b, i: (b * nq + i, 0)),
                  pl.BlockSpec((1, D_MODEL), lambda b, i: (0, 0)),
                  pl.BlockSpec((None, D_MODEL, GROUP_W), lambda b, i: (layer, 0, 0)),
                  pl.BlockSpec((MEM_LEN, 2 * GROUP_W), lambda b, i: (b, 0)),
                  pl.BlockSpec((None, GROUP_W, D_MODEL), lambda b, i: (layer, 0, 0)),
                  pl.BlockSpec((1, D_MODEL), lambda b, i: (0, 0))],
        out_specs=pl.BlockSpec((tq, D_MODEL), lambda b, i: (b * nq + i, 0)),
        out_shape=jax.ShapeDtypeStruct((M, D_MODEL), F32),
        scratch_shapes=[pltpu.VMEM((tq, D_MODEL), BF16)],
        compiler_params=_cparams(("parallel", "parallel"), VMEM_LIMIT),
    )(x, g_pre.reshape(1, D_MODEL), wq, kv, wo, g_post.reshape(1, D_MODEL))


SC_BETA, SC_EG, SC_QKB, SC_LI, SC_LF, SC_QKC = 0, 4, 8, 12, 16, 20


def _sample_pre_kernel(p_ref, pool_ref, dnc_ref, scc_ref, pw_ref, ps_ref, cw_ref, alog_ref, dtb_ref,
                       bi_ref, bf_ref, sw_ref,
                       ya_ref, yd_ref, pool_out_ref, dnc_out_ref, scc_out_ref,
                       rows_b_ref, rows_c_ref, scal_ref, cols_ref, *, start_pos):
    nb = p_ref.shape[0]
    blk = lambda c: p_ref[:, c * GROUP_W:(c + 1) * GROUP_W]

    u = blk(COL_A)
    ys = []
    for g, w in enumerate(POOL_WINDOWS):
        lanes = slice(g * 128, (g + 1) * 128)
        tot = u[:, lanes]
        for r in range(POOL_STATE + 1 - w, POOL_STATE):
            tot = tot + pool_ref[:, r * GROUP_W + g * 128:r * GROUP_W + (g + 1) * 128]
        d = tot / float(min(start_pos + 1, w)) - u[:, lanes]
        ys.append(_bdot(d, pw_ref[g]))
    ya_ref[...] = (jnp.concatenate(ys, axis=1) * ps_ref[...]).astype(ya_ref.dtype)
    pool_out_ref[:, :(POOL_STATE - 1) * GROUP_W] = pool_ref[:, GROUP_W:]
    pool_out_ref[:, (POOL_STATE - 1) * GROUP_W:] = u

    us = blk(COL_DC) * blk(COL_DH)
    sw = sw_ref[...]
    y = sw[0:1] * scc_ref[:, :GROUP_W] + sw[1:2] * scc_ref[:, GROUP_W:] + sw[2:3] * us
    yd_ref[...] = (blk(COL_DB) * y).astype(yd_ref.dtype)
    scc_out_ref[:, :GROUP_W] = scc_ref[:, GROUP_W:]
    scc_out_ref[:, GROUP_W:] = us

    qkv = p_ref[:, COL_BQ * GROUP_W:(COL_BV + 1) * GROUP_W]
    cw = cw_ref[...]
    W3 = 3 * GROUP_W
    conv = cw[DN_CONV - 1:DN_CONV] * qkv
    for j in range(DN_CONV - 1):
        conv = conv + cw[j:j + 1] * dnc_ref[:, j * W3:(j + 1) * W3]
    act = _silu(conv)
    dnc_out_ref[:, :(DN_CONV - 2) * W3] = dnc_ref[:, W3:]
    dnc_out_ref[:, (DN_CONV - 2) * W3:] = qkv

    gates = p_ref[:, GATE_COL0:GATE_COL0 + 128]
    beta_all = _sigmoid(gates)
    eg_all = jnp.exp(-jnp.exp(alog_ref[...]) * _softplus(gates + dtb_ref[...]))
    li_all = gates + bi_ref[...]
    lf_all = -_softplus(-(gates + bf_ref[...]))

    lane = lax.broadcasted_iota(jnp.int32, (nb, 128), 1)
    scal = jnp.zeros((nb, 128), F32)

    def put(tab, lane_idx, colv):
        return jnp.where(lane == lane_idx, colv, tab)

    qs, ks = [], []
    for h in range(N_HEADS):
        sl = slice(h * HEAD_DIM, (h + 1) * HEAD_DIM)
        qh = act[:, sl]
        kh = act[:, GROUP_W + h * HEAD_DIM:GROUP_W + (h + 1) * HEAD_DIM]
        qh = qh * lax.rsqrt(jnp.sum(qh * qh, axis=-1, keepdims=True) + EPS) * HEAD_DIM ** -0.5
        kh = kh * lax.rsqrt(jnp.sum(kh * kh, axis=-1, keepdims=True) + EPS)
        qs.append(qh)
        ks.append(kh)
        scal = put(scal, SC_BETA + h, _col(beta_all, LANE_BETA + h))
        scal = put(scal, SC_EG + h, _col(eg_all, LANE_A + h))
        scal = put(scal, SC_QKB + h, jnp.sum(qh * kh, axis=-1, keepdims=True))
        cols_ref[(0 * N_HEADS + h) * HEAD_DIM:(0 * N_HEADS + h + 1) * HEAD_DIM, :] = qh.T.astype(BF16)
        cols_ref[(1 * N_HEADS + h) * HEAD_DIM:(1 * N_HEADS + h + 1) * HEAD_DIM, :] = kh.T.astype(BF16)
    rows_b_ref[...] = jnp.concatenate(qs + ks + [act[:, 2 * GROUP_W:]], axis=1)

    qc = blk(COL_CQ)
    kc = blk(COL_CK) * HEAD_DIM ** -0.5
    for h in range(N_HEADS):
        sl = slice(h * HEAD_DIM, (h + 1) * HEAD_DIM)
        scal = put(scal, SC_LI + h, _col(li_all, LANE_I + h))
        scal = put(scal, SC_LF + h, _col(lf_all, LANE_F + h))
        scal = put(scal, SC_QKC + h, jnp.sum(qc[:, sl] * kc[:, sl], axis=-1, keepdims=True))
        cols_ref[(2 * N_HEADS + h) * HEAD_DIM:(2 * N_HEADS + h + 1) * HEAD_DIM, :] = qc[:, sl].T.astype(BF16)
        cols_ref[(3 * N_HEADS + h) * HEAD_DIM:(3 * N_HEADS + h + 1) * HEAD_DIM, :] = kc[:, sl].T.astype(BF16)
    rows_c_ref[...] = jnp.concatenate([qc, kc, blk(COL_CV)], axis=1)
    scal_ref[...] = scal


def sample_pre(P, pool_st, dnc_st, scc_st, pool_w, pool_scale, conv_w, a_log, dt_bias, b_i, b_f, sc_w,
               start_pos):
    nb = P.shape[0]
    lane_row = lambda v, lane: jnp.zeros((1, 128), F32).at[0, lane:lane + N_HEADS].set(v)
    W3 = 3 * GROUP_W
    out_shape = [jax.ShapeDtypeStruct((nb, GROUP_W), BF16),
                 jax.ShapeDtypeStruct((nb, GROUP_W), BF16),
                 jax.ShapeDtypeStruct(pool_st.shape, F32),
                 jax.ShapeDtypeStruct(dnc_st.shape, F32),
                 jax.ShapeDtypeStruct(scc_st.shape, F32),
                 jax.ShapeDtypeStruct((nb, W3), F32),
                 jax.ShapeDtypeStruct((nb, W3), F32),
                 jax.ShapeDtypeStruct((nb, 128), F32),
                 jax.ShapeDtypeStruct((4 * N_HEADS * HEAD_DIM, nb), BF16)]
    return pl.pallas_call(
        functools.partial(_sample_pre_kernel, start_pos=start_pos),
        out_shape=out_shape,
        compiler_params=pltpu.CompilerParams(vmem_limit_bytes=VMEM_LIMIT),
    )(P, pool_st, dnc_st, scc_st, pool_w, pool_scale.reshape(1, GROUP_W), conv_w,
      lane_row(a_log, LANE_A), lane_row(dt_bias, LANE_A), lane_row(b_i, LANE_I), lane_row(b_f, LANE_F), sc_w)


def _sample_rec_kernel(cols_ref, rows_b_ref, rows_c_ref, scal_ref, z_ref, og_ref, n_ref, m_ref,
                       dng_ref, mlg_ref, s_ref, c_ref, s_acc_ref, c_acc_ref,
                       yb_ref, yc_ref, s_out_ref, c_out_ref, n_out_ref, m_out_ref,
                       ob_ref, hc_ref, *, tb):
    del s_acc_ref, c_acc_ref
    i = pl.program_id(0)
    nb = cols_ref.shape[1]
    row_id = lax.broadcasted_iota(jnp.int32, (nb, 128), 0)
    lane_id = lax.broadcasted_iota(jnp.int32, (1, 128), 1)

    def body(j, carry):
        b = i * tb + j
        onehot = (row_id == b).astype(BF16)
        cols = jnp.dot(cols_ref[...], onehot, preferred_element_type=F32)
        scal = scal_ref[pl.ds(b, 1), :]
        sc = lambda idx: _col(scal, idx)
        m_row = m_ref[pl.ds(b, 1), :]
        rb = rows_b_ref[pl.ds(b, 1), :]
        rc = rows_c_ref[pl.ds(b, 1), :]
        n_all = n_ref[pl.ds(b, 1), :]
        m_new_row = jnp.zeros((1, 128), F32)
        o_rows, h_rows, n_rows = [], [], []
        for h in range(N_HEADS):
            sl = slice(h * HEAD_DIM, (h + 1) * HEAD_DIM)
            colblk = lambda v: cols[(v * N_HEADS + h) * HEAD_DIM:(v * N_HEADS + h + 1) * HEAD_DIM, :]
            s = s_ref[j, h]
            ks = jnp.sum(colblk(1) * s, axis=0, keepdims=True)
            qs = jnp.sum(colblk(0) * s, axis=0, keepdims=True)
            beta, eg, qk = sc(SC_BETA + h), sc(SC_EG + h), sc(SC_QKB + h)
            v_row = rb[:, 2 * GROUP_W + h * HEAD_DIM:2 * GROUP_W + (h + 1) * HEAD_DIM]
            v_new = beta * v_row - (beta * eg) * ks
            o_rows.append(eg * qs + qk * v_new)
            s_out_ref[j, h] = s * eg + colblk(1) * v_new
            cm = c_ref[j, h]
            qc = jnp.sum(colblk(2) * cm, axis=0, keepdims=True)
            q_row = rc[:, sl]
            k_row = rc[:, GROUP_W + h * HEAD_DIM:GROUP_W + (h + 1) * HEAD_DIM]
            vc_row = rc[:, 2 * GROUP_W + h * HEAD_DIM:2 * GROUP_W + (h + 1) * HEAD_DIM]
            n_row = n_all[:, sl]
            li, lf, qkc = sc(SC_LI + h), sc(SC_LF + h), sc(SC_QKC + h)
            m_old = _col(m_row, h)
            bb = lf + m_old
            m_t = jnp.maximum(bb, li)
            w_intra = jnp.exp(li - m_t) * qkc
            w_inter = jnp.exp(bb - m_t)
            num = w_inter * qc + w_intra * vc_row
            den = w_inter * jnp.sum(q_row * n_row, axis=-1, keepdims=True) + w_intra
            h_rows.append(num / jnp.maximum(jnp.abs(den), jnp.exp(-m_t)))
            m_new = jnp.maximum(m_old + lf, li)
            dec = jnp.exp(m_old + lf - m_new)
            e = jnp.exp(li - m_new)
            c_out_ref[j, h] = cm * dec + colblk(3) * (e * vc_row)
            n_rows.append(n_row * dec + e * k_row)
            m_new_row = jnp.where(lane_id == h, m_new, m_new_row)
        ob_ref[pl.ds(b, 1), :] = jnp.concatenate(o_rows, axis=1)
        hc_ref[pl.ds(b, 1), :] = jnp.concatenate(h_rows, axis=1)
        n_out_ref[pl.ds(b, 1), :] = jnp.concatenate(n_rows, axis=1)
        m_out_ref[pl.ds(b, 1), :] = m_new_row
        return carry

    lax.fori_loop(0, tb, body, 0, unroll=2)

    @pl.when(i == pl.num_programs(0) - 1)
    def _():
        ys_b, ys_c = [], []
        for h in range(N_HEADS):
            sl = slice(h * HEAD_DIM, (h + 1) * HEAD_DIM)
            o = ob_ref[:, sl]
            o = o * lax.rsqrt(jnp.mean(o * o, axis=-1, keepdims=True) + EPS) * dng_ref[...]
            ys_b.append(o * _silu(z_ref[:, sl]))
            hh = _sigmoid(og_ref[:, sl]) * hc_ref[:, sl]
            hh = hh * lax.rsqrt(jnp.mean(hh * hh, axis=-1, keepdims=True) + EPS) * mlg_ref[h:h + 1, :]
            ys_c.append(hh)
        yb_ref[...] = jnp.concatenate(ys_b, axis=1).astype(yb_ref.dtype)
        yc_ref[...] = jnp.concatenate(ys_c, axis=1).astype(yc_ref.dtype)


def sample_rec(cols, rows_b, rows_c, scal, P, n_st, m_st, dn_norm_g, ml_norm_g, s_all, c_all, s_acc, c_acc,
               layer, *, tb=8):
    nb = P.shape[0]
    W3 = 3 * GROUP_W
    full = lambda shape: pl.BlockSpec(shape, lambda i: (0,) * len(shape))
    state = pl.BlockSpec((None, tb, N_HEADS, HEAD_DIM, HEAD_DIM), lambda i: (layer, i, 0, 0, 0))
    untouched = pl.BlockSpec(memory_space=pl.ANY)
    return pl.pallas_call(
        functools.partial(_sample_rec_kernel, tb=tb),
        grid=(nb // tb,),
        in_specs=[full(cols.shape), full((nb, W3)), full((nb, W3)), full((nb, 128)),
                  pl.BlockSpec((nb, GROUP_W), lambda i: (0, COL_BZ)),
                  pl.BlockSpec((nb, GROUP_W), lambda i: (0, COL_CO)),
                  full((nb, GROUP_W)), full((nb, 128)), full((1, HEAD_DIM)), full((N_HEADS, HEAD_DIM)),
                  state, state, untouched, untouched],
        out_specs=[full((nb, GROUP_W)), full((nb, GROUP_W)), state, state,
                   full((nb, GROUP_W)), full((nb, 128))],
        out_shape=[jax.ShapeDtypeStruct((nb, GROUP_W), BF16),
                   jax.ShapeDtypeStruct((nb, GROUP_W), BF16),
                   jax.ShapeDtypeStruct(s_all.shape, F32),
                   jax.ShapeDtypeStruct(c_all.shape, F32),
                   jax.ShapeDtypeStruct((nb, GROUP_W), F32),
                   jax.ShapeDtypeStruct((nb, 128), F32)],
        input_output_aliases={12: 2, 13: 3},
        scratch_shapes=[pltpu.VMEM((nb, GROUP_W), F32), pltpu.VMEM((nb, GROUP_W), F32)],
        compiler_params=_cparams(("arbitrary",), VMEM_LIMIT),
    )(cols, rows_b, rows_c, scal, P, P, n_st, m_st, dn_norm_g.reshape(1, HEAD_DIM), ml_norm_g, s_all, c_all,
      s_acc, c_acc)


def _sample_xattn_kernel(q_ref, k_ref, v_ref, o_ref, *, tb):
    i = pl.program_id(0)
    n_rows = k_ref.shape[1]
    n_rep = n_rows // (2 * N_HEADS)
    ones = jnp.ones((HEAD_DIM, HEAD_DIM), BF16)
    fold = lambda x: x + pltpu.roll(x, N_HEADS, 0)

    def body(j, carry):
        b = i * tb + j
        q_row = q_ref[pl.ds(b, 1), :]
        heads = [q_row[:, h * HEAD_DIM:(h + 1) * HEAD_DIM] for h in range(N_HEADS)]
        q8 = jnp.concatenate(heads + heads, axis=0) * HEAD_DIM ** -0.5
        k3 = k_ref[j].reshape(n_rep, 2 * N_HEADS, HEAD_DIM)
        prod = (k3 * q8[None]).reshape(n_rows, HEAD_DIM)
        s = _bdot(prod, ones)
        s3 = s.reshape(n_rep, 2 * N_HEADS, HEAD_DIM)
        mx = jnp.max(s3, axis=0)
        mx = jnp.maximum(mx, pltpu.roll(mx, N_HEADS, 0))
        e3 = jnp.exp(s3 - mx[None])
        den = fold(jnp.sum(e3, axis=0))
        v3 = v_ref[j].reshape(n_rep, 2 * N_HEADS, HEAD_DIM)
        o8 = fold(jnp.sum(e3 * v3, axis=0)) / den
        o_ref[pl.ds(b, 1), :] = jnp.concatenate([o8[h:h + 1, :] for h in range(N_HEADS)], axis=1)
        return carry

    lax.fori_loop(0, tb, body, 0, unroll=4)


def sample_xattn(q, k_all, v_all, layer, *, tb=16):
    nb = q.shape[0]
    n_rows = k_all.shape[2]
    kv = pl.BlockSpec((None, tb, n_rows, HEAD_DIM), lambda i: (layer, i, 0, 0))
    return pl.pallas_call(
        functools.partial(_sample_xattn_kernel, tb=tb),
        grid=(nb // tb,),
        in_specs=[pl.BlockSpec((nb, GROUP_W), lambda i: (0, 0)), kv, kv],
        out_specs=pl.BlockSpec((nb, GROUP_W), lambda i: (0, 0)),
        out_shape=jax.ShapeDtypeStruct((nb, GROUP_W), F32),
        compiler_params=_cparams(("arbitrary",), VMEM_LIMIT),
    )(q, k_all, v_all)


W_B0 = 5 * GROUP_W
W_C0 = W_B0 + 2 * N_HEADS
W_C1 = W_C0 + 4 * GROUP_W
W_D0 = W_C1 + 2 * N_HEADS
W_D1 = W_D0 + 3 * GROUP_W


def _w_in_prep_kernel(w_ref, o_ref):
    tk = w_ref.shape[1]
    o_ref[:W_B0, :] = w_ref[:W_B0, :].astype(BF16)
    o_ref[W_B0:W_B0 + (W_C1 - W_C0), :] = w_ref[W_C0:W_C1, :].astype(BF16)
    o_ref[W_B0 + (W_C1 - W_C0):GATE_COL0, :] = w_ref[W_D0:W_D1, :].astype(BF16)
    gates = jnp.concatenate([w_ref[W_B0:W_C0, :], w_ref[W_C1:W_D0, :],
                             jnp.zeros((128 - 4 * N_HEADS, tk), F32)], axis=0)
    o_ref[GATE_COL0:, :] = gates.astype(BF16)


def _prep_w_in(w_in, *, tk=256):
    w_t = jnp.swapaxes(w_in, 1, 2)
    depth, n_in, K = w_t.shape
    assert n_in == W_D1
    return pl.pallas_call(
        _w_in_prep_kernel,
        grid=(depth, K // tk),
        in_specs=[pl.BlockSpec((None, n_in, tk), lambda l, i: (l, 0, i))],
        out_specs=pl.BlockSpec((None, IN_W_PAD, tk), lambda l, i: (l, 0, i)),
        out_shape=jax.ShapeDtypeStruct((depth, IN_W_PAD, K), BF16),
        compiler_params=_cparams(("parallel", "parallel"), VMEM_LIMIT),
    )(w_t)


def _ffn(x, g_pre, g_post, wg, wu, wd, layer, *, tm_up, tm_down, f32_up=False, up_layer=None):
    up_layer = layer if up_layer is None else up_layer
    up = functools.partial(norm_matmul, x, g_pre, tm=tm_up, tn=FFN_TN, out_dtype=BF16)
    if not f32_up:
        return matmul_resnorm([up([wg, wu], up_layer)], wd, layer, g_post, x, scale=0.5, tm=tm_down)
    n_tiles = x.shape[0] // tm_up
    act, wg_b, wu_b = up([wg, wu], up_layer, emit_bf16=True, row_tiles=(0, 1))
    if n_tiles == 1:
        y = matmul_resnorm([act], wd, layer, g_post, x, scale=0.5, tm=tm_down)
    else:
        rest = up([wg_b[None], wu_b[None]], 0, row_tiles=(1, n_tiles - 1))
        y = matmul_resnorm([act, rest], wd, layer, g_post, x, scale=0.5, tm=tm_down, row_pieces=True)
    return y, (wg_b, wu_b)


def _prompt_layer(x, mem2d, B, T, lw, layer):
    g = lw['norm_g']
    x, cast1 = _ffn(x, g[0], g[1], lw['ffn1_wg'], lw['ffn1_wu'], lw['ffn1_wd'], layer,
                    tm_up=1024, tm_down=256, f32_up=True)
    P = norm_matmul(x, g[2], [lw['w_in']], layer, tm=1024, tn=IN_TN, out_dtype=F32, w_is_nk=True)
    ya, yd, yb, yc, pool_tail, sc_tail, dn_s, dn_tail, ml_c, ml_nm = mixers_prompt(P, B, T, lw)
    x = matmul_resnorm([ya, yb, yc, yd], lw['w_out'], layer, g[3], x, scale=1.0, tm=512)
    kv = norm_matmul(mem2d, g[8], [lw['x_wkv']], layer, tm=1024, tn=512, out_dtype=F32)
    x = xattn_prompt(x, B, T, g[4], lw['x_wq'], kv, lw['x_wo'], layer, g[5])
    x, cast2 = _ffn(x, g[6], g[7], lw['ffn2_wg'], lw['ffn2_wu'], lw['ffn2_wd'], layer,
                    tm_up=1024, tm_down=256, f32_up=True)
    states = (pool_tail[:, 16 - POOL_STATE:], dn_tail[:, 8 - (DN_CONV - 1):], dn_s, ml_c,
              ml_nm[:, :N_HEADS], ml_nm[:, N_HEADS:, 0], sc_tail[:, 8 - (SC_WIDTH - 1):])
    mem_k = kv[:, :GROUP_W].reshape(B, MEM_LEN, N_HEADS, HEAD_DIM)
    mem_v = kv[:, GROUP_W:].reshape(B, MEM_LEN, N_HEADS, HEAD_DIM)
    return x, states, mem_k, mem_v, cast1 + cast2


def _sample_layer(x, st, big, acc, layer, lw, start_pos):
    pool_st, dnc_st, ml_n, ml_m, scc_st = st
    s_all, c_all, k_all, v_all = big
    s_acc, c_acc = acc
    nb = x.shape[0]
    g = lw['norm_g']
    wg1, wu1, wg2, wu2 = (w[None] for w in lw['ffn_cast'])
    x = _ffn(x, g[0], g[1], wg1, wu1, lw['ffn1_wd'], layer, tm_up=128, tm_down=128, up_layer=0)
    P = norm_matmul(x, g[2], [lw['w_in']], layer, tm=128, tn=IN_TN, out_dtype=F32, w_is_nk=True)
    (ya, yd, pool_new, dnc_new, scc_new, rows_b, rows_c, scal, cols) = sample_pre(
        P, pool_st.reshape(nb, -1), dnc_st.reshape(nb, -1), scc_st.reshape(nb, -1),
        lw['pool_w'], lw['pool_scale'], lw['dn_conv_w'], lw['dn_A_log'], lw['dn_dt_bias'],
        lw['ml_b_i'], lw['ml_b_f'], lw['sc_conv_w'], start_pos)
    m_pad = jnp.pad(ml_m, ((0, 0), (0, 128 - N_HEADS)))
    yb, yc, s_acc, c_acc, n_new, m_new = sample_rec(
        cols, rows_b, rows_c, scal, P, ml_n.reshape(nb, GROUP_W), m_pad,
        lw['dn_norm_g'], lw['ml_norm_g'], s_all, c_all, s_acc, c_acc, layer)
    x = matmul_resnorm([ya, yb, yc, yd], lw['w_out'], layer, g[3], x, scale=1.0, tm=128)
    q = norm_matmul(x, g[4], [lw['x_wq']], layer, tm=128, tn=GROUP_W, out_dtype=F32)
    o = sample_xattn(q, k_all, v_all, layer)
    x = matmul_resnorm([o], lw['x_wo'], layer, g[5], x, scale=1.0, tm=128)
    x = _ffn(x, g[6], g[7], wg2, wu2, lw['ffn2_wd'], layer, tm_up=128, tm_down=128, up_layer=0)
    states = (pool_new.reshape(pool_st.shape), dnc_new.reshape(dnc_st.shape),
              n_new.reshape(ml_n.shape), m_new[:, :N_HEADS], scc_new.reshape(scc_st.shape))
    return x, states, (s_acc, c_acc)


def kernel(x_prompt, x_sample, mem_prompt, state_pool, state_dn_conv, state_dn_S, state_ml_C, state_ml_n,
           state_ml_m, state_sc_conv, cache_mem_k, cache_mem_v, norm_g, w_in, w_out, pool_w, pool_scale,
           dn_conv_w, dn_A_log, dn_dt_bias, dn_norm_g, ml_b_i, ml_b_f, ml_norm_g, sc_conv_w,
           x_wq, x_wk, x_wv, x_wo, ffn1_wg, ffn1_wu, ffn1_wd, ffn2_wg, ffn2_wu, ffn2_wd):
    depth = norm_g.shape[0]
    B, T, _ = x_prompt.shape
    nb, t_dec, _ = x_sample.shape
    assert t_dec == 1
    start_pos = PAST_LEN

    stacked = dict(w_in=_prep_w_in(w_in), w_out=w_out.astype(BF16), x_wq=x_wq.astype(BF16),
                   x_wkv=jnp.concatenate([x_wk.astype(BF16), x_wv.astype(BF16)], axis=-1),
                   x_wo=x_wo.astype(BF16),
                   ffn1_wg=ffn1_wg, ffn1_wu=ffn1_wu, ffn1_wd=ffn1_wd.astype(BF16),
                   ffn2_wg=ffn2_wg, ffn2_wu=ffn2_wu, ffn2_wd=ffn2_wd.astype(BF16))

    def layer_weights(l):
        return dict(stacked, norm_g=norm_g[l], pool_w=pool_w[l],
                    pool_scale=pool_scale[l], dn_conv_w=dn_conv_w[l], dn_A_log=dn_A_log[l],
                    dn_dt_bias=dn_dt_bias[l], dn_norm_g=dn_norm_g[l], ml_b_i=ml_b_i[l], ml_b_f=ml_b_f[l],
                    ml_norm_g=ml_norm_g[l], sc_conv_w=sc_conv_w[l])

    mem2d = mem_prompt.reshape(B * MEM_LEN, D_MODEL)
    h = x_prompt.reshape(B * T, D_MODEL)
    p_states, mem_k_list, mem_v_list, ffn_cast = [], [], [], []
    for l in range(depth):
        h, ns, mk, mv, cast = _prompt_layer(h, mem2d, B, T, layer_weights(l), l)
        ffn_cast.append(cast)
        p_states.append(ns)
        mem_k_list.append(mk)
        mem_v_list.append(mv)
    y_prompt = h.reshape(B, T, D_MODEL)

    s_inputs = (state_pool, state_dn_conv, state_ml_n, state_ml_m, state_sc_conv)
    big = (state_dn_S, state_ml_C,
           cache_mem_k.reshape(depth, nb, MEM_LEN * N_HEADS, HEAD_DIM),
           cache_mem_v.reshape(depth, nb, MEM_LEN * N_HEADS, HEAD_DIM))
    h = x_sample.reshape(nb, D_MODEL)
    s_states = []
    acc = (lax.empty(state_dn_S.shape, F32), lax.empty(state_ml_C.shape, F32))
    for l in range(depth):
        st = tuple(s[l] for s in s_inputs)
        h, ns, acc = _sample_layer(h, st, big, acc, l, dict(layer_weights(l), ffn_cast=ffn_cast[l]), start_pos)
        s_states.append(ns)
    y_sample = h.reshape(nb, 1, D_MODEL)
    dn_S_s, ml_C_s = acc

    pool_p, dn_conv_p, dn_S_p, ml_C_p, ml_n_p, ml_m_p, sc_conv_p = [jnp.stack(z) for z in zip(*p_states)]
    pool_s, dn_conv_s, ml_n_s, ml_m_s, sc_conv_s = [jnp.stack(z) for z in zip(*s_states)]
    mem_k_p = jnp.stack(mem_k_list)
    mem_v_p = jnp.stack(mem_v_list)
    return (y_prompt, y_sample, pool_p, pool_s, dn_conv_p, dn_conv_s, dn_S_p, dn_S_s, ml_C_p, ml_C_s,
            ml_n_p, ml_n_s, ml_m_p, ml_m_s, sc_conv_p, sc_conv_s, mem_k_p, mem_v_p)
```
